```python
import numpy as np
import jax, jax.numpy as jnp
from jax import lax

D_MODEL = 1024
BATCH = 8
SEQ = 2048
DEPTH = 1
DEC_BATCH = 128
DEC_SEQ = 4
PAST_LEN = 16384
PAGE_SIZE = 128

PLE_DIM = 256
RET_HEADS = 4
RET_DK = 256
RET_DV = 512
RET_CHUNK = 128
MLA_HEADS = 8
MLA_NOPE = 128
MLA_ROPE = 64
MLA_V = 128
Q_LORA = 384
KV_LORA = 256
Q_BLOCK = 128
ROPE_BASE = 10000.0
EPS = 1e-6

RET_QK = RET_HEADS * RET_DK
RET_VW = RET_HEADS * RET_DV
MLA_QK = MLA_NOPE + MLA_ROPE
MLA_VW = MLA_HEADS * MLA_V
SPLIT_SIZES = (RET_QK, RET_QK, RET_VW, RET_VW, Q_LORA, KV_LORA, MLA_ROPE, MLA_VW, D_MODEL, D_MODEL)
IN_WIDTH = 2 * RET_QK + 2 * RET_VW + Q_LORA + KV_LORA + MLA_ROPE + MLA_VW + 2 * D_MODEL

kernel_name = 'hybrid_retention_mla_gated_decode_step'


def rmsnorm(x, w):
    xf = x.astype(jnp.float32)
    xf = xf * lax.rsqrt(jnp.mean(xf * xf, axis=-1, keepdims=True) + EPS)
    return xf.astype(x.dtype) * w


def rope(x, pos):
    half = x.shape[-1] // 2
    inv_freq = ROPE_BASE ** (-jnp.arange(half, dtype=jnp.float32) / half)
    ang = (pos[:, None] * inv_freq[None, :]).reshape((pos.shape[0],) + (1,) * (x.ndim - 3) + (half,))
    cos, sin = jnp.cos(ang), jnp.sin(ang)
    xf = x.astype(jnp.float32)
    x1, x2 = xf[..., :half], xf[..., half:]
    return jnp.concatenate([x1 * cos - x2 * sin, x2 * cos + x1 * sin], axis=-1).astype(x.dtype)


def retention_log_decay():
    return jnp.log1p(-jnp.exp2(-5.0 - jnp.arange(RET_HEADS, dtype=jnp.float32)))


def retention_chunk(q, k, v, s_prev, log_g):
    c = q.shape[1]
    idx = jnp.arange(c, dtype=jnp.float32)
    diff = idx[:, None] - idx[None, :]
    causal = (diff >= 0)[None]
    dmask = jnp.where(causal, jnp.exp(jnp.where(causal, diff[None], 0.0) * log_g[:, None, None]), 0.0)
    scores = jnp.einsum('bihd,bjhd->bhij', q, k) * dmask[None]
    intra = jnp.einsum('bhij,bjhe->bihe', scores, v)
    q_decay = jnp.exp((idx[:, None] + 1.0) * log_g[None, :])
    cross = jnp.einsum('bihd,bhde->bihe', q, s_prev) * q_decay[None, :, :, None]
    k_decay = jnp.exp((c - 1.0 - idx)[:, None] * log_g[None, :])
    s_new = (jnp.exp(c * log_g)[None, :, None, None] * s_prev
             + jnp.einsum('bjhd,bjhe->bhde', k * k_decay[None, :, :, None], v))
    return intra + cross, s_new


def retention_prompt(q, k, v, log_g):
    b, s, h, _ = q.shape
    nc = s // RET_CHUNK

    def to_chunks(t):
        return jnp.moveaxis(t.reshape(b, nc, RET_CHUNK, h, t.shape[-1]), 1, 0)

    def step(state, qkv):
        o, state = retention_chunk(qkv[0], qkv[1], qkv[2], state, log_g)
        return state, o

    s0 = jnp.zeros((b, h, RET_DK, RET_DV), jnp.float32)
    s_fin, o = lax.scan(step, s0, (to_chunks(q), to_chunks(k), to_chunks(v)))
    return jnp.moveaxis(o, 0, 1).reshape(b, s, h, RET_DV), s_fin


def retention_out(o, rg, gn_w):
    b, s = o.shape[0], o.shape[1]
    mu = jnp.mean(o, axis=-1, keepdims=True)
    var = jnp.mean(jnp.square(o - mu), axis=-1, keepdims=True)
    on = ((o - mu) * lax.rsqrt(var + EPS)).reshape(b, s, RET_VW).astype(rg.dtype)
    return on * gn_w * jax.nn.silu(rg)


def mixer_inputs(x, pos, ln_w, w_in, q_norm_w, w_uq, kv_norm_w):
    b, s, _ = x.shape
    z = rmsnorm(x, ln_w) @ w_in
    rq, rk, rv, rg, cq, ckv, kr, mg, ga, gb = jnp.split(z, np.cumsum(SPLIT_SIZES)[:-1].tolist(), axis=-1)
    rq = rope(rq.reshape(b, s, RET_HEADS, RET_DK), pos)
    rk = rope(rk.reshape(b, s, RET_HEADS, RET_DK), pos) * (RET_DK ** -0.5)
    rv = rv.reshape(b, s, RET_HEADS, RET_DV)
    q = (rmsnorm(cq, q_norm_w) @ w_uq).reshape(b, s, MLA_HEADS, MLA_QK)
    q_nope = q[..., :MLA_NOPE]
    q_rope = rope(q[..., MLA_NOPE:], pos)
    ckv = rmsnorm(ckv, kv_norm_w)
    kr = rope(kr, pos)
    return rq, rk, rv, rg, q_nope, q_rope, ckv, kr, mg, ga, gb


def mla_prompt(q_nope, q_rope, ckv, kr, w_ukv):
    b, s = ckv.shape[0], ckv.shape[1]
    kv = (ckv @ w_ukv).reshape(b, s, MLA_HEADS, MLA_NOPE + MLA_V)
    k_nope, v = kv[..., :MLA_NOPE], kv[..., MLA_NOPE:]
    nb = s // Q_BLOCK
    kpos = jnp.arange(s)
    scale = MLA_QK ** -0.5

    def block(args):
        qn, qr, i = args
        sc = jnp.einsum('bqhd,bkhd->bhqk', qn, k_nope) + jnp.einsum('bqhr,bkr->bhqk', qr, kr)
        sc = sc.astype(jnp.float32) * scale
        qpos = i * Q_BLOCK + jnp.arange(Q_BLOCK)
        sc = jnp.where(kpos[None, :] <= qpos[:, None], sc, -jnp.inf)
        p = jax.nn.softmax(sc, axis=-1).astype(v.dtype)
        return jnp.einsum('bhqk,bkhe->bqhe', p, v)

    qn_b = jnp.moveaxis(q_nope.reshape(b, nb, Q_BLOCK, MLA_HEADS, MLA_NOPE), 1, 0)
    qr_b = jnp.moveaxis(q_rope.reshape(b, nb, Q_BLOCK, MLA_HEADS, MLA_ROPE), 1, 0)
    o = lax.map(block, (qn_b, qr_b, jnp.arange(nb)))
    return jnp.moveaxis(o, 0, 1).reshape(b, s, MLA_VW)


def mla_sample(q_nope, q_rope, ckv, kr, cache_ckv, cache_kr, page_table, w_ukv):
    b, t = ckv.shape[0], ckv.shape[1]
    w = w_ukv.reshape(KV_LORA, MLA_HEADS, MLA_NOPE + MLA_V)
    w_uk, w_uv = w[..., :MLA_NOPE], w[..., MLA_NOPE:]
    q_lat = jnp.einsum('bthn,chn->bthc', q_nope, w_uk)
    causal = jnp.arange(t)[None, :] <= jnp.arange(t)[:, None]
    scale = MLA_QK ** -0.5

    def one_seq(args):
        ql, qr, pt, c_new, r_new = args
        c_past = cache_ckv[pt].reshape(-1, KV_LORA)
        r_past = cache_kr[pt].reshape(-1, MLA_ROPE)
        s_past = jnp.einsum('thc,kc->htk', ql, c_past) + jnp.einsum('thr,kr->htk', qr, r_past)
        s_new = jnp.einsum('thc,kc->htk', ql, c_new) + jnp.einsum('thr,kr->htk', qr, r_new)
        s_new = jnp.where(causal[None], s_new.astype(jnp.float32) * scale, -jnp.inf)
        sc = jnp.concatenate([s_past.astype(jnp.float32) * scale, s_new], axis=-1)
        p = jax.nn.softmax(sc, axis=-1).astype(c_new.dtype)
        n_past = c_past.shape[0]
        return (jnp.einsum('htk,kc->thc', p[..., :n_past], c_past)
                + jnp.einsum('htk,kc->thc', p[..., n_past:], c_new))

    o_lat = lax.map(one_seq, (q_lat, q_rope, page_table, ckv, kr))
    return jnp.einsum('bthc,chv->bthv', o_lat, w_uv).reshape(b, t, MLA_VW)


def finish_layer(x, ret_o, rg, mla_o, mg, ga, gb, ple, ret_gn_w, w_down_ret, w_down_mla, w_out, w_ple_gate, w_ple_proj):
    h_ret = retention_out(ret_o, rg, ret_gn_w) @ w_down_ret
    h_mla = (mla_o * jax.nn.silu(mg)) @ w_down_mla
    merged = jax.nn.sigmoid(ga) * h_ret + jax.nn.sigmoid(gb) * h_mla
    x = x + merged @ w_out
    return x + jax.nn.sigmoid(x @ w_ple_gate) * (ple @ w_ple_proj)


def setup_inputs(seed: int = 0) -> dict:
    key = jax.random.key(seed)
    ks = jax.random.split(key, 24)
    n_pages = PAST_LEN // PAGE_SIZE
    n_used = DEC_BATCH * n_pages
    n_pool = n_used + n_used // 4

    def w(k, shape, fan_in):
        return jax.random.normal(k, shape, jnp.float32) * (fan_in ** -0.5)

    def gain(k, shape):
        return 1.0 + 0.01 * jax.random.normal(k, shape, jnp.float32)

    page_table = jax.random.permutation(ks[0], n_pool)[:n_used].reshape(DEC_BATCH, n_pages).astype(jnp.int32)
    return {
        'x_prompt': jax.random.normal(ks[1], (BATCH, SEQ, D_MODEL), jnp.float32),
        'x_sample': jax.random.normal(ks[2], (DEC_BATCH, DEC_SEQ, D_MODEL), jnp.float32),
        'cache_ckv': jax.random.normal(ks[3], (DEPTH, n_pool, PAGE_SIZE, KV_LORA), jnp.float32),
        'cache_krope': jax.random.normal(ks[4], (DEPTH, n_pool, PAGE_SIZE, MLA_ROPE), jnp.float32),
        'state_ret': 0.5 * jax.random.normal(ks[5], (DEPTH, DEC_BATCH, RET_HEADS, RET_DK, RET_DV), jnp.float32),
        'page_table': page_table,
        'p_prompt': jax.random.normal(ks[6], (DEPTH, BATCH, SEQ, PLE_DIM), jnp.float32),
        'p_sample': jax.random.normal(ks[7], (DEPTH, DEC_BATCH, DEC_SEQ, PLE_DIM), jnp.float32),
        'ln_w': gain(ks[8], (DEPTH, D_MODEL)),
        'w_in': w(ks[9], (DEPTH, D_MODEL, IN_WIDTH), D_MODEL),
        'q_norm_w': gain(ks[10], (DEPTH, Q_LORA)),
        'w_uq': w(ks[11], (DEPTH, Q_LORA, MLA_HEADS * MLA_QK), Q_LORA),
        'kv_norm_w': gain(ks[12], (DEPTH, KV_LORA)),
        'w_ukv': w(ks[13], (DEPTH, KV_LORA, MLA_HEADS * (MLA_NOPE + MLA_V)), KV_LORA),
        'ret_gn_w': gain(ks[14], (DEPTH, RET_VW)),
        'w_down_ret': w(ks[15], (DEPTH, RET_VW, D_MODEL), RET_VW),
        'w_down_mla': w(ks[16], (DEPTH, MLA_VW, D_MODEL), MLA_VW),
        'w_out': w(ks[17], (DEPTH, D_MODEL, D_MODEL), D_MODEL),
        'w_ple_gate': w(ks[18], (DEPTH, D_MODEL, D_MODEL), D_MODEL),
        'w_ple_proj': w(ks[19], (DEPTH, PLE_DIM, D_MODEL), PLE_DIM),
        'final_norm_w': gain(ks[20], (D_MODEL,)),
    }


def reference(x_prompt, x_sample, cache_ckv, cache_krope, state_ret, page_table, p_prompt, p_sample,
              ln_w, w_in, q_norm_w, w_uq, kv_norm_w, w_ukv, ret_gn_w, w_down_ret, w_down_mla, w_out,
              w_ple_gate, w_ple_proj, final_norm_w):
    log_g = retention_log_decay()
    past_len = page_table.shape[1] * cache_ckv.shape[2]
    pos_p = jnp.arange(x_prompt.shape[1], dtype=jnp.float32)
    pos_s = past_len + jnp.arange(x_sample.shape[1], dtype=jnp.float32)
    y_p, y_s = x_prompt, x_sample
    ckv_p_l, kr_p_l, ret_p_l, ckv_s_l, kr_s_l, ret_s_l = [], [], [], [], [], []
    for i in range(DEPTH):
        rq, rk, rv, rg, qn, qr, ckv, kr, mg, ga, gb = mixer_inputs(y_p, pos_p, ln_w[i], w_in[i], q_norm_w[i], w_uq[i], kv_norm_w[i])
        ret_o, s_fin = retention_prompt(rq.astype(jnp.float32), rk.astype(jnp.float32), rv.astype(jnp.float32), log_g)
        mla_o = mla_prompt(qn, qr, ckv, kr, w_ukv[i])
        y_p = finish_layer(y_p, ret_o, rg, mla_o, mg, ga, gb, p_prompt[i], ret_gn_w[i], w_down_ret[i], w_down_mla[i],
                           w_out[i], w_ple_gate[i], w_ple_proj[i])
        ckv_p_l.append(ckv)
        kr_p_l.append(kr)
        ret_p_l.append(s_fin.astype(x_prompt.dtype))
        rq, rk, rv, rg, qn, qr, ckv, kr, mg, ga, gb = mixer_inputs(y_s, pos_s, ln_w[i], w_in[i], q_norm_w[i], w_uq[i], kv_norm_w[i])
        ret_o, s_new = retention_chunk(rq.astype(jnp.float32), rk.astype(jnp.float32), rv.astype(jnp.float32),
                                       state_ret[i].astype(jnp.float32), log_g)
        mla_o = mla_sample(qn, qr, ckv, kr, cache_ckv[i], cache_krope[i], page_table, w_ukv[i])
        y_s = finish_layer(y_s, ret_o, rg, mla_o, mg, ga, gb, p_sample[i], ret_gn_w[i], w_down_ret[i], w_down_mla[i],
                           w_out[i], w_ple_gate[i], w_ple_proj[i])
        ckv_s_l.append(ckv)
        kr_s_l.append(kr)
        ret_s_l.append(s_new.astype(state_ret.dtype))
    y_prompt = rmsnorm(y_p, final_norm_w)
    y_sample = rmsnorm(y_s, final_norm_w)
    ckv_prompt = jnp.stack(ckv_p_l)
    krope_prompt = jnp.stack(kr_p_l)
    ret_prompt = jnp.stack(ret_p_l)
    ckv_sample = jnp.stack(ckv_s_l)
    krope_sample = jnp.stack(kr_s_l)
    ret_sample = jnp.stack(ret_s_l)
    return (y_prompt, y_sample, ckv_prompt, krope_prompt, ret_prompt, ckv_sample, krope_sample, ret_sample)
```

```python
import functools

import jax
import jax.numpy as jnp
from jax import lax
from jax.experimental import pallas as pl
from jax.experimental.pallas import tpu as pltpu

F32 = jnp.float32
BF16 = jnp.bfloat16

D_MODEL = 1024
PLE_DIM = 256
RET_HEADS = 4
RET_DK = 256
RET_DV = 512
RET_CHUNK = 128
MLA_HEADS = 8
MLA_NOPE = 128
MLA_ROPE = 64
MLA_V = 128
Q_LORA = 384
KV_LORA = 256
ROPE_BASE = 10000.0
EPS = 1e-6

RET_QK = RET_HEADS * RET_DK
RET_VW = RET_HEADS * RET_DV
MLA_QK = MLA_NOPE + MLA_ROPE
MLA_VW = MLA_HEADS * MLA_V
MLA_QPAD = 256
LANE = 128
PAGES_PER_STEP = 16
VMEM_LIMIT = 48 * 1024 * 1024

_OFF_RQ = 0
_OFF_RK = _OFF_RQ + RET_QK
_OFF_RV = _OFF_RK + RET_QK
_OFF_RG = _OFF_RV + RET_VW
_OFF_CQ = _OFF_RG + RET_VW
_OFF_CKV = _OFF_CQ + Q_LORA
_OFF_KR = _OFF_CKV + KV_LORA
_OFF_MG = _OFF_KR + MLA_ROPE
_OFF_GA = _OFF_MG + MLA_VW
_OFF_GB = _OFF_GA + D_MODEL
_G_RG = 0
_G_MG = RET_VW
_G_GA = _G_MG + MLA_VW
_G_GB = _G_GA + D_MODEL
_G_W = _G_GB + D_MODEL


def _params(sem):
    return pltpu.CompilerParams(dimension_semantics=sem, vmem_limit_bytes=VMEM_LIMIT)


def _rms(x, w):
    return x * lax.rsqrt(jnp.mean(x * x, axis=-1, keepdims=True) + EPS) * w


def _silu(x):
    return x * jax.nn.sigmoid(x)


def _inproj_kernel(x_ref, lnw_ref, w_ref, cos_ref, sin_ref, o_ref, xn_ref, *, n_rope, n_q, k_scale, tn):
    j = pl.program_id(1)

    @pl.when(j == 0)
    def _():
        xn_ref[...] = _rms(x_ref[...], lnw_ref[...]).astype(BF16)

    acc = jnp.dot(xn_ref[...], w_ref[...], preferred_element_type=F32)

    @pl.when(j < n_rope)
    def _():
        cos = cos_ref[...]
        sin = sin_ref[...]
        scale = jnp.where(j >= n_q, k_scale, 1.0).astype(F32)
        for h in range(tn // RET_DK):
            lo = h * RET_DK
            a1 = acc[:, lo:lo + LANE]
            a2 = acc[:, lo + LANE:lo + RET_DK]
            o_ref[:, lo:lo + LANE] = ((a1 * cos - a2 * sin) * scale).astype(o_ref.dtype)
            o_ref[:, lo + LANE:lo + RET_DK] = ((a2 * cos + a1 * sin) * scale).astype(o_ref.dtype)

    @pl.when(j >= n_rope)
    def _():
        o_ref[...] = acc.astype(o_ref.dtype)


def _inproj(x, ln_w, w, cos, sin, *, tm, tn, out_dtype, n_rope, n_q, k_scale):
    m, d = x.shape
    n = w.shape[1]
    p_tiles = cos.shape[0] // tm
    kern = functools.partial(_inproj_kernel, n_rope=n_rope, n_q=n_q, k_scale=k_scale, tn=tn)
    return pl.pallas_call(
        kern,
        grid=(m // tm, n // tn),
        in_specs=[
            pl.BlockSpec((tm, d), lambda i, j: (i, 0)),
            pl.BlockSpec((1, d), lambda i, j: (0, 0)),
            pl.BlockSpec((d, tn), lambda i, j: (0, j)),
            pl.BlockSpec((tm, LANE), lambda i, j: (i % p_tiles, 0)),
            pl.BlockSpec((tm, LANE), lambda i, j: (i % p_tiles, 0)),
        ],
        out_specs=pl.BlockSpec((tm, tn), lambda i, j: (i, j)),
        out_shape=jax.ShapeDtypeStruct((m, n), out_dtype),
        scratch_shapes=[pltpu.VMEM((tm, d), BF16)],
        compiler_params=_params(("arbitrary", "arbitrary")),
        name="inproj",
    )(x, ln_w, w, cos, sin)


def _mla_prep_kernel(x_ref, lnw_ref, wc_ref, qnw_ref, wqa_ref, wqb_ref, kvnw_ref, cos_ref, sin_ref, *rest,
                     with_kv):
    if with_kv:
        wkv_ref, q_ref, ckv_ref, kr_ref, krp_ref, kv_ref = rest
    else:
        q_ref, ckv_ref, kr_ref = rest
    xn = _rms(x_ref[...], lnw_ref[...]).astype(BF16)
    zc = jnp.dot(xn, wc_ref[...], preferred_element_type=F32)
    cos = cos_ref[...]
    sin = sin_ref[...]
    cqn = _rms(zc[:, :Q_LORA], qnw_ref[...]).astype(BF16)
    qa = jnp.dot(cqn, wqa_ref[...], preferred_element_type=F32)
    qb = jnp.dot(cqn, wqb_ref[...], preferred_element_type=F32)
    for h in range(MLA_HEADS):
        lo = h * MLA_QPAD
        q_ref[:, lo:lo + LANE] = qa[:, lo:lo + LANE].astype(q_ref.dtype)
        q_ref[:, lo + LANE:lo + MLA_QPAD] = (
            qa[:, lo + LANE:lo + MLA_QPAD] * cos + qb[:, h * LANE:(h + 1) * LANE] * sin).astype(q_ref.dtype)
    c0 = Q_LORA
    ckvn = _rms(zc[:, c0:c0 + KV_LORA], kvnw_ref[...])
    ckv_ref[...] = ckvn
    k0 = c0 + KV_LORA
    krp = zc[:, k0:k0 + LANE] * cos + zc[:, k0 + LANE:k0 + 2 * LANE] * sin
    kr_ref[...] = krp[:, :MLA_ROPE]
    if with_kv:
        krp_ref[...] = krp.astype(BF16)
        kv_ref[...] = jnp.dot(ckvn.astype(BF16), wkv_ref[...], preferred_element_type=F32).astype(BF16)


def _mla_prep(x, ln_w, wc, qnw, wqa, wqb, kvnw, cos, sin, wkv, *, tm, with_kv):
    m, d = x.shape
    p_tiles = cos.shape[0] // tm
    row = lambda i: (i, 0)
    fixed = lambda i: (0, 0)
    in_specs = [
        pl.BlockSpec((tm, d), row),
        pl.BlockSpec((1, d), fixed),
        pl.BlockSpec(wc.shape, fixed),
        pl.BlockSpec((1, Q_LORA), fixed),
        pl.BlockSpec(wqa.shape, fixed),
        pl.BlockSpec(wqb.shape, fixed),
        pl.BlockSpec((1, KV_LORA), fixed),
        pl.BlockSpec((tm, LANE), lambda i: (i % p_tiles, 0)),
        pl.BlockSpec((tm, LANE), lambda i: (i % p_tiles, 0)),
    ]
    args = [x, ln_w, wc, qnw, wqa, wqb, kvnw, cos, sin]
    qw = MLA_HEADS * MLA_QPAD
    out_specs = [pl.BlockSpec((tm, qw), row), pl.BlockSpec((tm, KV_LORA), row), pl.BlockSpec((tm, MLA_ROPE), row)]
    out_shape = [jax.ShapeDtypeStruct((m, qw), BF16), jax.ShapeDtypeStruct((m, KV_LORA), F32),
                 jax.ShapeDtypeStruct((m, MLA_ROPE), F32)]
    if with_kv:
        in_specs.append(pl.BlockSpec(wkv.shape, fixed))
        args.append(wkv)
        out_specs += [pl.BlockSpec((tm, LANE), row), pl.BlockSpec((tm, wkv.shape[1]), row)]
        out_shape += [jax.ShapeDtypeStruct((m, LANE), BF16), jax.ShapeDtypeStruct((m, wkv.shape[1]), BF16)]
    return pl.pallas_call(
        functools.partial(_mla_prep_kernel, with_kv=with_kv),
        grid=(m // tm,),
        in_specs=in_specs,
        out_specs=out_specs,
        out_shape=out_shape,
        compiler_params=_params(("arbitrary",)),
        name="mla_prep",
    )(*args)


def _ret_chunk(q, k, v, s_prev, dmask, qdec, kdec, gc):
    sc = lax.dot_general(q, k, (((1,), (1,)), ((), ())), preferred_element_type=F32) * dmask
    intra = jnp.dot(sc.astype(BF16), v, preferred_element_type=F32)
    cross = jnp.dot(q, s_prev.astype(BF16), preferred_element_type=F32) * qdec
    kd = (k.astype(F32) * kdec).astype(BF16)
    s_new = gc * s_prev + lax.dot_general(kd, v, (((0,), (0,)), ((), ())), preferred_element_type=F32)
    return intra + cross, s_new


def _ret_act(o, rg, gnw):
    mu = jnp.mean(o, axis=-1, keepdims=True)
    d = o - mu
    var = jnp.mean(d * d, axis=-1, keepdims=True)
    return d * lax.rsqrt(var + EPS) * gnw * _silu(rg)


def _ret_prompt_kernel(q_ref, k_ref, v_ref, rg_ref, gnw_ref, dm_ref, qd_ref, kd_ref, gc_ref,
                       o_ref, sfin_ref, s_ref, *, n_chunks):
    s_ref[...] = jnp.zeros_like(s_ref)
    dmask = dm_ref[0]
    qdec = qd_ref[0]
    kdec = kd_ref[0]
    gc = gc_ref[0][:, :1]
    gnw = gnw_ref[...]

    def chunk(c, carry):
        r = pl.multiple_of(c * RET_CHUNK, RET_CHUNK)
        rows = pl.ds(r, RET_CHUNK)
        o, s_new = _ret_chunk(q_ref[rows, :], k_ref[rows, :], v_ref[rows, :], s_ref[...], dmask, qdec, kdec, gc)
        s_ref[...] = s_new
        o_ref[rows, :] = _ret_act(o, rg_ref[rows, :], gnw).astype(o_ref.dtype)
        return carry

    lax.fori_loop(0, n_chunks, chunk, 0)
    sfin_ref[0, 0] = s_ref[...]


def _ret_prompt(qkv, gates, gn_w, dmask, qdec, kdec, gcs, *, batch, seq):
    h = RET_HEADS
    nq = RET_QK // RET_DK
    nv = (2 * RET_QK) // RET_DV
    return pl.pallas_call(
        functools.partial(_ret_prompt_kernel, n_chunks=seq // RET_CHUNK),
        grid=(batch, h),
        in_specs=[
            pl.BlockSpec((seq, RET_DK), lambda b, hh: (b, hh)),
            pl.BlockSpec((seq, RET_DK), lambda b, hh: (b, nq + hh)),
            pl.BlockSpec((seq, RET_DV), lambda b, hh: (b, nv + hh)),
            pl.BlockSpec((seq, RET_DV), lambda b, hh: (b, hh)),
            pl.BlockSpec((1, RET_DV), lambda b, hh: (0, hh)),
            pl.BlockSpec((1, RET_CHUNK, RET_CHUNK), lambda b, hh: (hh, 0, 0)),
            pl.BlockSpec((1, RET_CHUNK, 1), lambda b, hh: (hh, 0, 0)),
            pl.BlockSpec((1, RET_CHUNK, 1), lambda b, hh: (hh, 0, 0)),
            pl.BlockSpec((1, 1, LANE), lambda b, hh: (hh, 0, 0)),
        ],
        out_specs=[
            pl.BlockSpec((seq, RET_DV), lambda b, hh: (b, hh)),
            pl.BlockSpec((1, 1, RET_DK, RET_DV), lambda b, hh: (b, hh, 0, 0)),
        ],
        out_shape=[
            jax.ShapeDtypeStruct((batch * seq, RET_VW), BF16),
            jax.ShapeDtypeStruct((batch, h, RET_DK, RET_DV), F32),
        ],
        scratch_shapes=[pltpu.VMEM((RET_DK, RET_DV), F32)],
        compiler_params=_params(("arbitrary", "arbitrary")),
        name="ret_prompt",
    )(qkv, qkv, qkv, gates, gn_w, dmask, qdec, kdec, gcs)


def _ret_sample_kernel(qkv_ref, rg_ref, gnw_ref, st_ref, dm_ref, qd_ref, kd_ref, gc_ref,
                       o_ref, snew_ref, qp_ref, kp_ref, vp_ref, *, t):
    qp_ref[...] = jnp.zeros_like(qp_ref)
    kp_ref[...] = jnp.zeros_like(kp_ref)
    vp_ref[...] = jnp.zeros_like(vp_ref)
    for h in range(RET_HEADS):
        qp_ref[0:t, :] = qkv_ref[0, :, h * RET_DK:(h + 1) * RET_DK].astype(F32)
        kp_ref[0:t, :] = qkv_ref[0, :, RET_QK + h * RET_DK:RET_QK + (h + 1) * RET_DK].astype(F32)
        vp_ref[0:t, :] = qkv_ref[0, :, 2 * RET_QK + h * RET_DV:2 * RET_QK + (h + 1) * RET_DV].astype(F32)
        o, s_new = _ret_chunk(qp_ref[...].astype(BF16), kp_ref[...].astype(BF16), vp_ref[...].astype(BF16),
                              st_ref[0, h], dm_ref[h], qd_ref[h], kd_ref[h], gc_ref[h][:, :1])
        snew_ref[0, h] = s_new
        act = _ret_act(o[0:t, :], rg_ref[0, :, h * RET_DV:(h + 1) * RET_DV], gnw_ref[:, h * RET_DV:(h + 1) * RET_DV])
        o_ref[0, :, h * RET_DV:(h + 1) * RET_DV] = act.astype(o_ref.dtype)


def _ret_sample(qkv, gates, gn_w, state, dmask, qdec, kdec, gcs, *, batch, t):
    h = RET_HEADS
    qkv3 = qkv.reshape(batch, t, qkv.shape[1])
    gates3 = gates.reshape(batch, t, gates.shape[1])
    whole = lambda b: (0, 0, 0)
    out, s_new = pl.pallas_call(
        functools.partial(_ret_sample_kernel, t=t),
        grid=(batch,),
        in_specs=[
            pl.BlockSpec((1, t, qkv.shape[1]), lambda b: (b, 0, 0)),
            pl.BlockSpec((1, t, RET_VW), lambda b: (b, 0, 0)),
            pl.BlockSpec((1, RET_VW), lambda b: (0, 0)),
            pl.BlockSpec((1, h, RET_DK, RET_DV), lambda b: (b, 0, 0, 0)),
            pl.BlockSpec(dmask.shape, whole),
            pl.BlockSpec(qdec.shape, whole),
            pl.BlockSpec(kdec.shape, whole),
            pl.BlockSpec(gcs.shape, whole),
        ],
        out_specs=[
            pl.BlockSpec((1, t, RET_VW), lambda b: (b, 0, 0)),
            pl.BlockSpec((1, h, RET_DK, RET_DV), lambda b: (b, 0, 0, 0)),
        ],
        out_shape=[
            jax.ShapeDtypeStruct((batch, t, RET_VW), BF16),
            jax.ShapeDtypeStruct(state.shape, state.dtype),
        ],
        scratch_shapes=[pltpu.VMEM((RET_CHUNK, RET_DK), F32), pltpu.VMEM((RET_CHUNK, RET_DK), F32),
                        pltpu.VMEM((RET_CHUNK, RET_DV), F32)],
        compiler_params=_params(("arbitrary",)),
        name="ret_sample",
    )(qkv3, gates3, gn_w, state, dmask, qdec, kdec, gcs)
    return out.reshape(batch * t, RET_VW), s_new


def _mla_prompt_kernel(q_ref, kv_ref, krp_ref, mg_ref, o_ref, m_ref, l_ref, acc_ref, *, tq, scale):
    qi = pl.program_id(2)
    q = q_ref[...]

    def scores(kb):
        rows = pl.ds(pl.multiple_of(kb * tq, tq), tq)
        k = jnp.concatenate([kv_ref[rows, 0:MLA_NOPE], krp_ref[rows, :]], axis=1)
        s = lax.dot_general(q, k, (((1,), (1,)), ((), ())), preferred_element_type=F32) * scale
        return s, kv_ref[rows, MLA_NOPE:MLA_NOPE + MLA_V]

    s, v = scores(qi)
    r_id = lax.broadcasted_iota(jnp.int32, (tq, tq), 0)
    c_id = lax.broadcasted_iota(jnp.int32, (tq, tq), 1)
    s = jnp.where(c_id <= r_id, s, -jnp.inf)
    m0 = jnp.max(s, axis=-1, keepdims=True)
    p = jnp.exp(s - m0)
    m_ref[...] = m0
    l_ref[...] = jnp.sum(p, axis=-1, keepdims=True)
    acc_ref[...] = jnp.dot(p.astype(BF16), v, preferred_element_type=F32)

    def body(kb, carry):
        s, v = scores(kb)
        m_old = m_ref[...]
        m_new = jnp.maximum(m_old, jnp.max(s, axis=-1, keepdims=True))
        alpha = jnp.exp(m_old - m_new)
        p = jnp.exp(s - m_new)
        m_ref[...] = m_new
        l_ref[...] = alpha * l_ref[...] + jnp.sum(p, axis=-1, keepdims=True)
        acc_ref[...] = alpha * acc_ref[...] + jnp.dot(p.astype(BF16), v, preferred_element_type=F32)
        return carry

    lax.fori_loop(0, qi, body, 0)
    o_ref[...] = (acc_ref[...] / l_ref[...] * _silu(mg_ref[...])).astype(o_ref.dtype)


def _mla_prompt(q, kv, krp, gates, *, batch, seq, tq):
    nq = seq // tq
    mg_blk = _G_MG // MLA_V
    return pl.pallas_call(
        functools.partial(_mla_prompt_kernel, tq=tq, scale=MLA_QK ** -0.5),
        grid=(batch, MLA_HEADS, nq),
        in_specs=[
            pl.BlockSpec((tq, MLA_QPAD), lambda b, h, i: (b * nq + i, h)),
            pl.BlockSpec((seq, MLA_NOPE + MLA_V), lambda b, h, i: (b, h)),
            pl.BlockSpec((seq, LANE), lambda b, h, i: (b, 0)),
            pl.BlockSpec((tq, MLA_V), lambda b, h, i: (b * nq + i, mg_blk + h)),
        ],
        out_specs=pl.BlockSpec((tq, MLA_V), lambda b, h, i: (b * nq + i, h)),
        out_shape=jax.ShapeDtypeStruct((batch * seq, MLA_VW), BF16),
        scratch_shapes=[pltpu.VMEM((tq, 1), F32), pltpu.VMEM((tq, 1), F32), pltpu.VMEM((tq, MLA_V), F32)],
        compiler_params=_params(("arbitrary", "arbitrary", "arbitrary")),
        name="mla_prompt",
    )(q, kv, krp, gates)


def _headproj_kernel(x_ref, w_ref, *rest, gated):
    if gated:
        g_ref, o_ref = rest
    else:
        (o_ref,) = rest
    y = jnp.dot(x_ref[...], w_ref[0], preferred_element_type=F32)
    if gated:
        y = y * _silu(g_ref[...])
    o_ref[...] = y.astype(o_ref.dtype)


def _headproj(x, w, *, x_blk, x_stride, gate=None, gate_blk0=0):
    m = x.shape[0]
    nh, kdim, n = w.shape
    in_specs = [pl.BlockSpec((m, kdim), lambda h: (0, h * x_stride + x_blk)),
                pl.BlockSpec((1, kdim, n), lambda h: (h, 0, 0))]
    args = [x, w]
    if gate is not None:
        in_specs.append(pl.BlockSpec((m, n), lambda h: (0, gate_blk0 + h)))
        args.append(gate)
    return pl.pallas_call(
        functools.partial(_headproj_kernel, gated=gate is not None),
        grid=(nh,),
        in_specs=in_specs,
        out_specs=pl.BlockSpec((m, n), lambda h: (0, h)),
        out_shape=jax.ShapeDtypeStruct((m, nh * n), BF16),
        compiler_params=_params(("arbitrary",)),
        name="headproj",
    )(*args)


def _mla_sample_kernel(pt_ref, ql_ref, qr_ref, cn_ref, rn_ref, *rest, t, page, scale):
    del pt_ref
    g = PAGES_PER_STEP
    c_refs = rest[:g]
    r_refs = rest[g:2 * g]
    o_ref, m_ref, l_ref, acc_ref = rest[2 * g:]
    step = pl.program_id(1)
    ql = ql_ref[0]
    qr = qr_ref[0]
    rows = ql.shape[0]

    @pl.when(step == 0)
    def _():
        qlf = ql.astype(F32)
        qrf = qr.astype(F32)
        tok = lax.broadcasted_iota(jnp.int32, (rows, 1), 0) % t
        cn = cn_ref[0].astype(BF16).astype(F32)
        rn = rn_ref[0].astype(BF16).astype(F32)
        s_cols = []
        for j in range(t):
            sj = (jnp.sum(qlf * cn[j:j + 1, :], axis=-1, keepdims=True)
                  + jnp.sum(qrf * rn[j:j + 1, :], axis=-1, keepdims=True)) * scale
            s_cols.append(jnp.where(tok >= j, sj, -jnp.inf))
        m0 = s_cols[0]
        for j in range(1, t):
            m0 = jnp.maximum(m0, s_cols[j])
        l0 = jnp.zeros((rows, 1), F32)
        a0 = jnp.zeros((rows, KV_LORA), F32)
        for j in range(t):
            pj = jnp.exp(s_cols[j] - m0)
            l0 = l0 + pj
            a0 = a0 + pj.astype(BF16).astype(F32) * cn[j:j + 1, :]
        m_ref[...] = m0
        l_ref[...] = l0
        acc_ref[...] = a0

    cs = [c_refs[i][0].astype(BF16) for i in range(g)]
    s = jnp.concatenate(
        [lax.dot_general(ql, cs[i], (((1,), (1,)), ((), ())), preferred_element_type=F32)
         + lax.dot_general(qr, r_refs[i][0].astype(BF16), (((1,), (1,)), ((), ())), preferred_element_type=F32)
         for i in range(g)], axis=1) * scale
    m_old = m_ref[...]
    m_new = jnp.maximum(m_old, jnp.max(s, axis=-1, keepdims=True))
    alpha = jnp.exp(m_old - m_new)
    p = jnp.exp(s - m_new)
    m_ref[...] = m_new
    l_ref[...] = alpha * l_ref[...] + jnp.sum(p, axis=-1, keepdims=True)
    pb = p.astype(BF16)
    acc = alpha * acc_ref[...]
    for i in range(g):
        acc = acc + jnp.dot(pb[:, i * page:(i + 1) * page], cs[i], preferred_element_type=F32)
    acc_ref[...] = acc

    @pl.when(step == pl.num_programs(1) - 1)
    def _():
        o_ref[0] = acc_ref[...] / l_ref[...]


def _mla_sample(page_table, ql, qr, c_new, r_new, cache_c, cache_r, *, t):
    batch, n_pages = page_table.shape
    page = cache_c.shape[1]
    g = PAGES_PER_STEP
    rows = ql.shape[1]
    per_b = lambda b, s, pt: (b, 0, 0)

    def page_spec(width, i):
        return pl.BlockSpec((1, page, width), lambda b, s, pt: (pt[b, s * g + i], 0, 0))

    grid_spec = pltpu.PrefetchScalarGridSpec(
        num_scalar_prefetch=1,
        grid=(batch, n_pages // g),
        in_specs=[
            pl.BlockSpec((1, rows, KV_LORA), per_b),
            pl.BlockSpec((1, rows, MLA_ROPE), per_b),
            pl.BlockSpec((1, t, KV_LORA), per_b),
            pl.BlockSpec((1, t, MLA_ROPE), per_b),
        ] + [page_spec(KV_LORA, i) for i in range(g)] + [page_spec(MLA_ROPE, i) for i in range(g)],
        out_specs=pl.BlockSpec((1, rows, KV_LORA), per_b),
        scratch_shapes=[pltpu.VMEM((rows, 1), F32), pltpu.VMEM((rows, 1), F32), pltpu.VMEM((rows, KV_LORA), F32)],
    )
    return pl.pallas_call(
        functools.partial(_mla_sample_kernel, t=t, page=page, scale=MLA_QK ** -0.5),
        grid_spec=grid_spec,
        out_shape=jax.ShapeDtypeStruct((batch, rows, KV_LORA), F32),
        compiler_params=_params(("arbitrary", "arbitrary")),
        name="mla_sample",
    )(page_table, ql, qr, c_new, r_new, *([cache_c] * g), *([cache_r] * g))


def _finish_kernel(x_ref, ar_ref, am_ref, ga_ref, gb_ref, ple_ref, wdr_ref, wdm_ref, wo_ref, wpg_ref, wpp_ref,
                   fw_ref, o_ref, *, final):
    h_ret = jnp.dot(ar_ref[...], wdr_ref[...], preferred_element_type=F32)
    h_mla = jnp.dot(am_ref[...], wdm_ref[...], preferred_element_type=F32)
    merged = jax.nn.sigmoid(ga_ref[...]) * h_ret + jax.nn.sigmoid(gb_ref[...]) * h_mla
    x2 = x_ref[...] + jnp.dot(merged.astype(BF16), wo_ref[...], preferred_element_type=F32)
    gate = jax.nn.sigmoid(jnp.dot(x2.astype(BF16), wpg_ref[...], preferred_element_type=F32))
    y = x2 + gate * jnp.dot(ple_ref[...].astype(BF16), wpp_ref[...], preferred_element_type=F32)
    if final:
        y = _rms(y, fw_ref[...])
    o_ref[...] = y


def _finish(x, act_ret, act_mla, gates, ple, wdr, wdm, wo, wpg, wpp, fw, *, tm, final):
    m, d = x.shape
    row = lambda i: (i, 0)
    fixed = lambda i: (0, 0)
    return pl.pallas_call(
        functools.partial(_finish_kernel, final=final),
        grid=(m // tm,),
        in_specs=[
            pl.BlockSpec((tm, d), row),
            pl.BlockSpec((tm, RET_VW), row),
            pl.BlockSpec((tm, MLA_VW), row),
            pl.BlockSpec((tm, d), lambda i: (i, _G_GA // D_MODEL)),
            pl.BlockSpec((tm, d), lambda i: (i, _G_GB // D_MODEL)),
            pl.BlockSpec((tm, PLE_DIM), row),
            pl.BlockSpec(wdr.shape, fixed),
            pl.BlockSpec(wdm.shape, fixed),
            pl.BlockSpec(wo.shape, fixed),
            pl.BlockSpec(wpg.shape, fixed),
            pl.BlockSpec(wpp.shape, fixed),
            pl.BlockSpec((1, d), fixed),
        ],
        out_specs=pl.BlockSpec((tm, d), row),
        out_shape=jax.ShapeDtypeStruct((m, d), F32),
        compiler_params=_params(("arbitrary",)),
        name="finish",
    )(x, act_ret, act_mla, gates, gates, ple, wdr, wdm, wo, wpg, wpp, fw)


def _rope_tables(pos, half):
    inv_freq = ROPE_BASE ** (-jnp.arange(half, dtype=F32) / half)
    ang = pos[:, None] * inv_freq[None, :]
    return jnp.cos(ang), jnp.sin(ang)


def _mla_rope_tables(pos):
    cos, sin = _rope_tables(pos, MLA_ROPE // 2)
    z = jnp.zeros((pos.shape[0], LANE - MLA_ROPE), F32)
    return jnp.concatenate([cos, cos, z], axis=1), jnp.concatenate([sin, sin, z], axis=1)


def _decay_tables(c):
    log_g = jnp.log1p(-jnp.exp2(-5.0 - jnp.arange(RET_HEADS, dtype=F32)))
    idx = jnp.arange(c, dtype=F32)
    diff = idx[:, None] - idx[None, :]
    causal = (diff >= 0)[None]
    dmask = jnp.where(causal, jnp.exp(jnp.where(causal, diff[None], 0.0) * log_g[:, None, None]), 0.0)
    qdec = jnp.exp((idx[None, :] + 1.0) * log_g[:, None])[:, :, None]
    kdec = jnp.exp((c - 1.0 - idx)[None, :] * log_g[:, None])[:, :, None]
    gcs = jnp.broadcast_to(jnp.exp(c * log_g)[:, None, None], (RET_HEADS, 1, LANE))
    return dmask, qdec, kdec, gcs


def _pad_rows(a, rows):
    return jnp.pad(a, ((0, 0), (0, rows - a.shape[1])) + ((0, 0),) * (a.ndim - 2))


def _rot_cols(w):
    half = w.shape[-1] // 2
    return jnp.concatenate([-w[..., half:], w[..., :half]], axis=-1)


def _layer_weights(w_in, w_uq, w_ukv, w_down_ret, w_down_mla, w_out, w_ple_gate, w_ple_proj):
    d = w_in.shape[0]
    w_qkv = w_in[:, _OFF_RQ:_OFF_RG].astype(BF16)
    w_gates = jnp.concatenate([w_in[:, _OFF_RG:_OFF_CQ], w_in[:, _OFF_MG:]], axis=1).astype(BF16)
    kr = w_in[:, _OFF_KR:_OFF_MG]
    zpad = jnp.zeros((d, LANE - MLA_ROPE), F32)
    w_c = jnp.concatenate([w_in[:, _OFF_CQ:_OFF_KR], kr, zpad, _rot_cols(kr), zpad], axis=1).astype(BF16)
    uq = w_uq.reshape(Q_LORA, MLA_HEADS, MLA_QK)
    uq_rope = uq[:, :, MLA_NOPE:]
    zq = jnp.zeros((Q_LORA, MLA_HEADS, MLA_QPAD - MLA_QK), F32)
    w_qa = jnp.concatenate([uq, zq], axis=2).reshape(Q_LORA, MLA_HEADS * MLA_QPAD).astype(BF16)
    w_qb = jnp.concatenate([_rot_cols(uq_rope), zq], axis=2).reshape(Q_LORA, MLA_HEADS * LANE).astype(BF16)
    ukv = w_ukv.reshape(KV_LORA, MLA_HEADS, MLA_NOPE + MLA_V)
    w_uk_t = jnp.transpose(ukv[:, :, :MLA_NOPE], (1, 2, 0)).astype(BF16)
    w_uv = jnp.transpose(ukv[:, :, MLA_NOPE:], (1, 0, 2)).astype(BF16)
    return dict(w_qkv=w_qkv, w_gates=w_gates, w_c=w_c, w_qa=w_qa, w_qb=w_qb, w_kv=w_ukv.astype(BF16),
                w_uk_t=w_uk_t, w_uv=w_uv, wdr=w_down_ret.astype(BF16), wdm=w_down_mla.astype(BF16),
                wo=w_out.astype(BF16), wpg=w_ple_gate.astype(BF16), wpp=w_ple_proj.astype(BF16))


def _row_tile(m, cap):
    t = min(m, cap)
    assert m % t == 0
    return t


def _project(x, ln_w, lw, cos_r, sin_r):
    m = x.shape[0]
    tm = _row_tile(cos_r.shape[0], 1024)
    qkv = _inproj(x, ln_w, lw["w_qkv"], cos_r, sin_r, tm=tm, tn=512, out_dtype=BF16,
                  n_rope=(2 * RET_QK) // 512, n_q=RET_QK // 512, k_scale=RET_DK ** -0.5)
    gates = _inproj(x, ln_w, lw["w_gates"], cos_r, sin_r, tm=tm, tn=512, out_dtype=F32,
                    n_rope=0, n_q=0, k_scale=1.0)
    assert m % tm == 0
    return qkv, gates


def kernel(x_prompt, x_sample, cache_ckv, cache_krope, state_ret, page_table, p_prompt, p_sample, ln_w, w_in,
           q_norm_w, w_uq, kv_norm_w, w_ukv, ret_gn_w, w_down_ret, w_down_mla, w_out, w_ple_gate, w_ple_proj,
           final_norm_w):
    depth = w_in.shape[0]
    b, s, d = x_prompt.shape
    bd, t, _ = x_sample.shape
    n_pages = page_table.shape[1]
    page = cache_ckv.shape[2]
    past_len = n_pages * page
    assert s % RET_CHUNK == 0 and n_pages % PAGES_PER_STEP == 0

    pos_p = jnp.arange(s, dtype=F32)
    pos_s = jnp.tile(past_len + jnp.arange(t, dtype=F32), bd)
    cos_rp, sin_rp = _rope_tables(pos_p, RET_DK // 2)
    cos_rs, sin_rs = _rope_tables(pos_s, RET_DK // 2)
    cos_mp, sin_mp = _mla_rope_tables(pos_p)
    cos_ms, sin_ms = _mla_rope_tables(pos_s)
    dec_p = _decay_tables(RET_CHUNK)
    dec_s = _decay_tables(t)
    dmask_s = jnp.pad(dec_s[0], ((0, 0), (0, RET_CHUNK - t), (0, RET_CHUNK - t)))
    qdec_s = _pad_rows(dec_s[1], RET_CHUNK)
    kdec_s = _pad_rows(dec_s[2], RET_CHUNK)
    gcs_s = dec_s[3]

    y_p = x_prompt.reshape(b * s, d)
    y_s = x_sample.reshape(bd * t, d)
    tq = _row_tile(s, 512)
    outs = [[] for _ in range(6)]
    for i in range(depth):
        lw = _layer_weights(w_in[i], w_uq[i], w_ukv[i], w_down_ret[i], w_down_mla[i], w_out[i], w_ple_gate[i],
                            w_ple_proj[i])
        lnw = ln_w[i][None, :]
        qnw = q_norm_w[i][None, :]
        kvnw = kv_norm_w[i][None, :]
        gnw = ret_gn_w[i][None, :]
        fw = final_norm_w[None, :]
        final = i == depth - 1

        qkv, gates = _project(y_p, lnw, lw, cos_rp, sin_rp)
        q, ckv, kr, krp, kv = _mla_prep(y_p, lnw, lw["w_c"], qnw, lw["w_qa"], lw["w_qb"], kvnw, cos_mp, sin_mp,
                                        lw["w_kv"], tm=_row_tile(s, 512), with_kv=True)
        act_ret, s_fin = _ret_prompt(qkv, gates, gnw, *dec_p, batch=b, seq=s)
        act_mla = _mla_prompt(q, kv, krp, gates, batch=b, seq=s, tq=tq)
        y_p = _finish(y_p, act_ret, act_mla, gates, p_prompt[i].reshape(b * s, -1), lw["wdr"], lw["wdm"], lw["wo"],
                      lw["wpg"], lw["wpp"], fw, tm=_row_tile(b * s, 256), final=final)
        outs[0].append(ckv.reshape(b, s, KV_LORA))
        outs[1].append(kr.reshape(b, s, MLA_ROPE))
        outs[2].append(s_fin.astype(x_prompt.dtype))

        ms = bd * t
        qkv, gates = _project(y_s, lnw, lw, cos_rs, sin_rs)
        q, ckv, kr = _mla_prep(y_s, lnw, lw["w_c"], qnw, lw["w_qa"], lw["w_qb"], kvnw, cos_ms, sin_ms, None,
                               tm=ms, with_kv=False)
        act_ret, s_new = _ret_sample(qkv, gates, gnw, state_ret[i], dmask_s, qdec_s, kdec_s, gcs_s, batch=bd, t=t)
        q_lat = _headproj(q, lw["w_uk_t"], x_blk=0, x_stride=MLA_QPAD // LANE)
        q_lat = q_lat.reshape(bd, t, MLA_HEADS, KV_LORA).transpose(0, 2, 1, 3).reshape(bd, MLA_HEADS * t, KV_LORA)
        q_rope = q.reshape(bd, t, MLA_HEADS, MLA_QPAD)[..., MLA_NOPE:MLA_QK]
        q_rope = q_rope.transpose(0, 2, 1, 3).reshape(bd, MLA_HEADS * t, MLA_ROPE)
        o_lat = _mla_sample(page_table, q_lat, q_rope, ckv.reshape(bd, t, KV_LORA), kr.reshape(bd, t, MLA_ROPE),
                            cache_ckv[i], cache_krope[i], t=t)
        o_lat = o_lat.reshape(bd, MLA_HEADS, t, KV_LORA).transpose(0, 2, 1, 3).reshape(ms, MLA_HEADS * KV_LORA)
        act_mla = _headproj(o_lat.astype(BF16), lw["w_uv"], x_blk=0, x_stride=1, gate=gates,
                            gate_blk0=_G_MG // MLA_V)
        y_s = _finish(y_s, act_ret, act_mla, gates, p_sample[i].reshape(ms, -1), lw["wdr"], lw["wdm"], lw["wo"],
                      lw["wpg"], lw["wpp"], fw, tm=_row_tile(ms, 256), final=final)
        outs[3].append(ckv.reshape(bd, t, KV_LORA))
        outs[4].append(kr.reshape(bd, t, MLA_ROPE))
        outs[5].append(s_new.astype(state_ret.dtype))

    y_prompt = y_p.reshape(b, s, d)
    y_sample = y_s.reshape(bd, t, d)
    ckv_p, kr_p, ret_p, ckv_s, kr_s, ret_s = [jnp.stack(o) for o in outs]
    return (y_prompt, y_sample, ckv_p, kr_p, ret_p, ckv_s, kr_s, ret_s)
```

```python
import functools
import math

import jax
import jax.numpy as jnp
from jax import lax
from jax.experimental import pallas as pl
from jax.experimental.pallas import tpu as pltpu

F32 = jnp.float32
BF16 = jnp.bfloat16

D_MODEL = 1024
PLE_DIM = 256
RET_HEADS = 4
RET_DK = 256
RET_DV = 512
RET_CHUNK = 128
MLA_HEADS = 8
MLA_NOPE = 128
MLA_ROPE = 64
MLA_V = 128
Q_LORA = 384
KV_LORA = 256
ROPE_BASE = 10000.0
EPS = 1e-6

RET_QK = RET_HEADS * RET_DK
RET_VW = RET_HEADS * RET_DV
MLA_QK = MLA_NOPE + MLA_ROPE
MLA_VW = MLA_HEADS * MLA_V
MLA_QPAD = 256
MLA_KVW = MLA_NOPE + MLA_V
Q_SCALE = MLA_QK ** -0.5 * math.log2(math.e)
LANE = 128
PAGES_PER_STEP = 32
N_SLOTS = 3
KEY_SPLITS = 4
HEADS_PER_STEP = 2
VMEM_LIMIT = 48 * 1024 * 1024

_OFF_RQ = 0
_OFF_RK = _OFF_RQ + RET_QK
_OFF_RV = _OFF_RK + RET_QK
_OFF_RG = _OFF_RV + RET_VW
_OFF_CQ = _OFF_RG + RET_VW
_OFF_CKV = _OFF_CQ + Q_LORA
_OFF_KR = _OFF_CKV + KV_LORA
_OFF_MG = _OFF_KR + MLA_ROPE
_OFF_GA = _OFF_MG + MLA_VW
_OFF_GB = _OFF_GA + D_MODEL
_S_RV = 0
_S_RG = _S_RV + RET_VW
_S_MG = _S_RG + RET_VW
_S_GA = _S_MG + MLA_VW
_S_GB = _S_GA + D_MODEL
_S_W = _S_GB + D_MODEL

_NT = (((1,), (1,)), ((), ()))
_TN = (((0,), (0,)), ((), ()))


def _params(sem):
    return pltpu.CompilerParams(dimension_semantics=sem, vmem_limit_bytes=VMEM_LIMIT)


def _rms(x, w):
    return x * lax.rsqrt(jnp.mean(x * x, axis=-1, keepdims=True) + EPS) * w


def _silu(x):
    return x * jax.nn.sigmoid(x)


def _lane_blocks(s):
    return [s[:, c * LANE:(c + 1) * LANE] for c in range(s.shape[1] // LANE)]


def _lane_rowmax(s):
    blocks = _lane_blocks(s)
    mx = blocks[0]
    for blk in blocks[1:]:
        mx = jnp.maximum(mx, blk)
    return jnp.broadcast_to(jnp.max(mx, axis=-1, keepdims=True), mx.shape)


def _lane_exp(s, m):
    ps = [jnp.exp2(blk - m) for blk in _lane_blocks(s)]
    part = ps[0]
    for p in ps[1:]:
        part = part + p
    return jnp.concatenate([p.astype(BF16) for p in ps], axis=1), part


def _inproj_kernel(x_ref, lnw_ref, w_ref, *rest, rope, n_q, k_scale, tn):
    if rope:
        cos_ref, sin_ref, o_ref, xn_ref = rest
    else:
        o_ref, xn_ref = rest
    j = pl.program_id(1)

    @pl.when(j == 0)
    def _():
        xn_ref[...] = _rms(x_ref[...], lnw_ref[...]).astype(BF16)

    acc = jnp.dot(xn_ref[...], w_ref[...], preferred_element_type=F32)
    if not rope:
        o_ref[...] = acc.astype(o_ref.dtype)
        return
    scale = jnp.where(j >= n_q, k_scale, 1.0).astype(F32)
    cos = cos_ref[...] * scale
    sin = sin_ref[...] * scale
    for h in range(tn // RET_DK):
        lo = h * RET_DK
        a1 = acc[:, lo:lo + LANE]
        a2 = acc[:, lo + LANE:lo + RET_DK]
        o_ref[:, lo:lo + LANE] = (a1 * cos - a2 * sin).astype(o_ref.dtype)
        o_ref[:, lo + LANE:lo + RET_DK] = (a2 * cos + a1 * sin).astype(o_ref.dtype)


def _inproj(x, ln_w, w, cos=None, sin=None, *, tm, tn, n_q=0, k_scale=1.0):
    m, d = x.shape
    n = w.shape[1]
    rope = cos is not None
    in_specs = [
        pl.BlockSpec((tm, d), lambda i, j: (i, 0)),
        pl.BlockSpec((1, d), lambda i, j: (0, 0)),
        pl.BlockSpec((d, tn), lambda i, j: (0, j)),
    ]
    args = [x, ln_w, w]
    if rope:
        p_tiles = cos.shape[0] // tm
        in_specs += [pl.BlockSpec((tm, LANE), lambda i, j: (i % p_tiles, 0))] * 2
        args += [cos, sin]
    return pl.pallas_call(
        functools.partial(_inproj_kernel, rope=rope, n_q=n_q, k_scale=k_scale, tn=tn),
        grid=(m // tm, n // tn),
        in_specs=in_specs,
        out_specs=pl.BlockSpec((tm, tn), lambda i, j: (i, j)),
        out_shape=jax.ShapeDtypeStruct((m, n), BF16),
        scratch_shapes=[pltpu.VMEM((tm, d), BF16)],
        compiler_params=_params(("arbitrary", "arbitrary")),
        name="inproj_rope" if rope else "inproj",
    )(*args)


def _mla_prep_kernel(x_ref, lnw_ref, wc_ref, qnw_ref, wqa_ref, wqb_ref, kvnw_ref, cos_ref, sin_ref, *rest,
                     with_kv):
    if with_kv:
        wkv_ref, q_ref, ckv_ref, kr_ref, krp_ref, kv_ref = rest
    else:
        q_ref, ckv_ref, kr_ref = rest
    xn = _rms(x_ref[...], lnw_ref[...]).astype(BF16)
    zc = jnp.dot(xn, wc_ref[...], preferred_element_type=F32)
    cos = cos_ref[...]
    sin = sin_ref[...]
    cqn = _rms(zc[:, :Q_LORA], qnw_ref[...]).astype(BF16)
    qa = jnp.dot(cqn, wqa_ref[...], preferred_element_type=F32)
    qb = jnp.dot(cqn, wqb_ref[...], preferred_element_type=F32)
    qcos = cos * Q_SCALE
    qsin = sin * Q_SCALE
    for h in range(MLA_HEADS):
        lo = h * MLA_QPAD
        q_ref[:, lo:lo + LANE] = (qa[:, lo:lo + LANE] * Q_SCALE).astype(q_ref.dtype)
        q_ref[:, lo + LANE:lo + MLA_QPAD] = (
            qa[:, lo + LANE:lo + MLA_QPAD] * qcos + qb[:, h * LANE:(h + 1) * LANE] * qsin).astype(q_ref.dtype)
    c0 = Q_LORA
    ckvn = _rms(zc[:, c0:c0 + KV_LORA], kvnw_ref[...])
    ckv_ref[...] = ckvn
    k0 = c0 + KV_LORA
    krp = zc[:, k0:k0 + LANE] * cos + zc[:, k0 + LANE:k0 + 2 * LANE] * sin
    kr_ref[...] = krp[:, :MLA_ROPE]
    if with_kv:
        krp_ref[...] = krp.astype(BF16)
        kv_ref[...] = jnp.dot(ckvn.astype(BF16), wkv_ref[...], preferred_element_type=F32).astype(BF16)


def _mla_prep(x, ln_w, wc, qnw, wqa, wqb, kvnw, cos, sin, wkv, *, tm, with_kv):
    m, d = x.shape
    p_tiles = cos.shape[0] // tm
    row = lambda i: (i, 0)
    fixed = lambda i: (0, 0)
    in_specs = [
        pl.BlockSpec((tm, d), row),
        pl.BlockSpec((1, d), fixed),
        pl.BlockSpec(wc.shape, fixed),
        pl.BlockSpec((1, Q_LORA), fixed),
        pl.BlockSpec(wqa.shape, fixed),
        pl.BlockSpec(wqb.shape, fixed),
        pl.BlockSpec((1, KV_LORA), fixed),
        pl.BlockSpec((tm, LANE), lambda i: (i % p_tiles, 0)),
        pl.BlockSpec((tm, LANE), lambda i: (i % p_tiles, 0)),
    ]
    args = [x, ln_w, wc, qnw, wqa, wqb, kvnw, cos, sin]
    qw = MLA_HEADS * MLA_QPAD
    out_specs = [pl.BlockSpec((tm, qw), row), pl.BlockSpec((tm, KV_LORA), row), pl.BlockSpec((tm, MLA_ROPE), row)]
    out_shape = [jax.ShapeDtypeStruct((m, qw), BF16), jax.ShapeDtypeStruct((m, KV_LORA), F32),
                 jax.ShapeDtypeStruct((m, MLA_ROPE), F32)]
    if with_kv:
        in_specs.append(pl.BlockSpec(wkv.shape, fixed))
        args.append(wkv)
        out_specs += [pl.BlockSpec((tm, LANE), row), pl.BlockSpec((tm, wkv.shape[1]), row)]
        out_shape += [jax.ShapeDtypeStruct((m, LANE), BF16), jax.ShapeDtypeStruct((m, wkv.shape[1]), BF16)]
    return pl.pallas_call(
        functools.partial(_mla_prep_kernel, with_kv=with_kv),
        grid=(m // tm,),
        in_specs=in_specs,
        out_specs=out_specs,
        out_shape=out_shape,
        compiler_params=_params(("arbitrary",)),
        name="mla_prep",
    )(*args)


def _ret_chunk(q, k, v, s_prev, dmask, qdec, kdec, gc):
    sc = lax.dot_general(q, k, _NT, preferred_element_type=F32) * dmask
    intra = jnp.dot(sc.astype(BF16), v, preferred_element_type=F32)
    cross = jnp.dot(q, s_prev.astype(BF16), preferred_element_type=F32) * qdec
    kd = (k.astype(F32) * kdec).astype(BF16)
    s_new = gc * s_prev + lax.dot_general(kd, v, _TN, preferred_element_type=F32)
    return intra + cross, s_new


def _ret_act(o, rg, gnw):
    mu = jnp.mean(o, axis=-1, keepdims=True)
    d = o - mu
    var = jnp.mean(d * d, axis=-1, keepdims=True)
    return d * lax.rsqrt(var + EPS) * gnw * _silu(rg.astype(F32))


def _ret_prompt_kernel(q_ref, k_ref, v_ref, rg_ref, gnw_ref, dm_ref, qd_ref, kd_ref, gc_ref,
                       o_ref, sfin_ref, s_ref, *, n_chunks, nh):
    s_ref[...] = jnp.zeros_like(s_ref)

    def chunk(c, carry):
        rows = pl.ds(pl.multiple_of(c * RET_CHUNK, RET_CHUNK), RET_CHUNK)
        for h in range(nh):
            qk_cols = slice(h * RET_DK, (h + 1) * RET_DK)
            v_cols = slice(h * RET_DV, (h + 1) * RET_DV)
            o, s_new = _ret_chunk(q_ref[rows, qk_cols], k_ref[rows, qk_cols], v_ref[rows, v_cols], s_ref[h],
                                  dm_ref[h], qd_ref[h], kd_ref[h], gc_ref[h])
            s_ref[h] = s_new
            o_ref[rows, v_cols] = _ret_act(o, rg_ref[rows, v_cols], gnw_ref[:, v_cols]).astype(o_ref.dtype)
        return carry

    lax.fori_loop(0, n_chunks, chunk, 0)
    sfin_ref[0] = s_ref[...]


def _ret_prompt(qk, slab, gn_w, dmask, qdec, kdec, gcs, *, batch, seq):
    nh = HEADS_PER_STEP
    qw, vw = nh * RET_DK, nh * RET_DV
    k_blk = RET_QK // qw
    rg_blk = _S_RG // vw
    return pl.pallas_call(
        functools.partial(_ret_prompt_kernel, n_chunks=seq // RET_CHUNK, nh=nh),
        grid=(batch, RET_HEADS // nh),
        in_specs=[
            pl.BlockSpec((seq, qw), lambda b, hp: (b, hp)),
            pl.BlockSpec((seq, qw), lambda b, hp: (b, k_blk + hp)),
            pl.BlockSpec((seq, vw), lambda b, hp: (b, hp)),
            pl.BlockSpec((seq, vw), lambda b, hp: (b, rg_blk + hp)),
            pl.BlockSpec((1, vw), lambda b, hp: (0, hp)),
            pl.BlockSpec((nh, RET_CHUNK, RET_CHUNK), lambda b, hp: (hp, 0, 0)),
            pl.BlockSpec((nh, RET_CHUNK, RET_DV), lambda b, hp: (hp, 0, 0)),
            pl.BlockSpec((nh, RET_CHUNK, RET_DK), lambda b, hp: (hp, 0, 0)),
            pl.BlockSpec((nh, 1, RET_DV), lambda b, hp: (hp, 0, 0)),
        ],
        out_specs=[
            pl.BlockSpec((seq, vw), lambda b, hp: (b, hp)),
            pl.BlockSpec((1, nh, RET_DK, RET_DV), lambda b, hp: (b, hp, 0, 0)),
        ],
        out_shape=[
            jax.ShapeDtypeStruct((batch * seq, RET_VW), BF16),
            jax.ShapeDtypeStruct((batch, RET_HEADS, RET_DK, RET_DV), F32),
        ],
        scratch_shapes=[pltpu.VMEM((nh, RET_DK, RET_DV), F32)],
        compiler_params=_params(("arbitrary", "arbitrary")),
        name="ret_prompt",
    )(qk, qk, slab, slab, gn_w, dmask, qdec, kdec, gcs)


def _ret_sample_kernel(qk_ref, vg_ref, gnw_ref, st_ref, dm_ref, qd_ref, kd_ref, gc_ref,
                       o_ref, snew_ref, qp_ref, kp_ref, vp_ref, *, t):
    qp_ref[...] = jnp.zeros_like(qp_ref)
    kp_ref[...] = jnp.zeros_like(kp_ref)
    vp_ref[...] = jnp.zeros_like(vp_ref)
    for h in range(RET_HEADS):
        qp_ref[0:t, :] = qk_ref[0, :, h * RET_DK:(h + 1) * RET_DK].astype(F32)
        kp_ref[0:t, :] = qk_ref[0, :, RET_QK + h * RET_DK:RET_QK + (h + 1) * RET_DK].astype(F32)
        vp_ref[0:t, :] = vg_ref[0, :, _S_RV + h * RET_DV:_S_RV + (h + 1) * RET_DV].astype(F32)
        o, s_new = _ret_chunk(qp_ref[...].astype(BF16), kp_ref[...].astype(BF16), vp_ref[...].astype(BF16),
                              st_ref[0, h], dm_ref[h], qd_ref[h], kd_ref[h], gc_ref[h])
        snew_ref[0, h] = s_new
        act = _ret_act(o[0:t, :], vg_ref[0, :, _S_RG + h * RET_DV:_S_RG + (h + 1) * RET_DV],
                       gnw_ref[:, h * RET_DV:(h + 1) * RET_DV])
        o_ref[0, :, h * RET_DV:(h + 1) * RET_DV] = act.astype(o_ref.dtype)


def _ret_sample(qk, slab, gn_w, state, dmask, qdec, kdec, gcs, *, batch, t):
    h = RET_HEADS
    qk3 = qk.reshape(batch, t, qk.shape[1])
    slab3 = slab.reshape(batch, t, slab.shape[1])
    whole = lambda b: (0, 0, 0)
    out, s_new = pl.pallas_call(
        functools.partial(_ret_sample_kernel, t=t),
        grid=(batch,),
        in_specs=[
            pl.BlockSpec((1, t, qk.shape[1]), lambda b: (b, 0, 0)),
            pl.BlockSpec((1, t, _S_MG), lambda b: (b, 0, 0)),
            pl.BlockSpec((1, RET_VW), lambda b: (0, 0)),
            pl.BlockSpec((1, h, RET_DK, RET_DV), lambda b: (b, 0, 0, 0)),
            pl.BlockSpec(dmask.shape, whole),
            pl.BlockSpec(qdec.shape, whole),
            pl.BlockSpec(kdec.shape, whole),
            pl.BlockSpec(gcs.shape, whole),
        ],
        out_specs=[
            pl.BlockSpec((1, t, RET_VW), lambda b: (b, 0, 0)),
            pl.BlockSpec((1, h, RET_DK, RET_DV), lambda b: (b, 0, 0, 0)),
        ],
        out_shape=[
            jax.ShapeDtypeStruct((batch, t, RET_VW), BF16),
            jax.ShapeDtypeStruct(state.shape, state.dtype),
        ],
        scratch_shapes=[pltpu.VMEM((RET_CHUNK, RET_DK), F32), pltpu.VMEM((RET_CHUNK, RET_DK), F32),
                        pltpu.VMEM((RET_CHUNK, RET_DV), F32)],
        compiler_params=_params(("arbitrary",)),
        name="ret_sample",
    )(qk3, slab3, gn_w, state, dmask, qdec, kdec, gcs)
    return out.reshape(batch * t, RET_VW), s_new


def _mla_prompt_kernel(q_ref, kv_ref, krp_ref, mg_ref, o_ref, m_ref, l_ref, acc_ref, *, tq, tk, nh):
    qi = pl.program_id(2)
    ratio = tq // tk

    def scores(kb, h):
        rows = pl.ds(pl.multiple_of(kb * tk, tk), tk)
        k = jnp.concatenate([kv_ref[rows, h * MLA_KVW:h * MLA_KVW + MLA_NOPE], krp_ref[rows, :]], axis=1)
        s = lax.dot_general(q_ref[:, h * MLA_QPAD:(h + 1) * MLA_QPAD], k, _NT, preferred_element_type=F32)
        return s, kv_ref[rows, h * MLA_KVW + MLA_NOPE:(h + 1) * MLA_KVW]

    def update(h, s, v, first=False):
        m_cur = _lane_rowmax(s)
        if first:
            m_new = m_cur
        else:
            m_old = m_ref[h]
            m_new = jnp.maximum(m_old, m_cur)
            alpha = jnp.exp2(m_old - m_new)
        p, l_part = _lane_exp(s, m_new)
        pv = jnp.dot(p, v, preferred_element_type=F32)
        m_ref[h] = m_new
        l_ref[h] = l_part if first else alpha * l_ref[h] + l_part
        acc_ref[h] = pv if first else alpha * acc_ref[h] + pv

    r_id = lax.broadcasted_iota(jnp.int32, (tq, tk), 0)
    c_id = lax.broadcasted_iota(jnp.int32, (tq, tk), 1)
    for d in range(ratio):
        for h in range(nh):
            s, v = scores(qi * ratio + d, h)
            update(h, jnp.where(c_id + d * tk <= r_id, s, -jnp.inf), v, first=d == 0)

    def body(kb, carry):
        for h in range(nh):
            s, v = scores(kb, h)
            update(h, s, v)
        return carry

    lax.fori_loop(0, qi * ratio, body, 0)
    for h in range(nh):
        cols = slice(h * MLA_V, (h + 1) * MLA_V)
        l = jnp.sum(l_ref[h], axis=-1, keepdims=True)
        o_ref[:, cols] = (acc_ref[h] / l * _silu(mg_ref[:, cols].astype(F32))).astype(o_ref.dtype)


def _mla_prompt(q, kv, krp, slab, *, batch, seq, tq, tk):
    nq = seq // tq
    nh = HEADS_PER_STEP
    mg_blk = _S_MG // (nh * MLA_V)
    return pl.pallas_call(
        functools.partial(_mla_prompt_kernel, tq=tq, tk=tk, nh=nh),
        grid=(batch, MLA_HEADS // nh, nq),
        in_specs=[
            pl.BlockSpec((tq, nh * MLA_QPAD), lambda b, hp, i: (b * nq + i, hp)),
            pl.BlockSpec((seq, nh * MLA_KVW), lambda b, hp, i: (b, hp)),
            pl.BlockSpec((seq, LANE), lambda b, hp, i: (b, 0)),
            pl.BlockSpec((tq, nh * MLA_V), lambda b, hp, i: (b * nq + i, mg_blk + hp)),
        ],
        out_specs=pl.BlockSpec((tq, nh * MLA_V), lambda b, hp, i: (b * nq + i, hp)),
        out_shape=jax.ShapeDtypeStruct((batch * seq, MLA_VW), BF16),
        scratch_shapes=[pltpu.VMEM((nh, tq, LANE), F32), pltpu.VMEM((nh, tq, LANE), F32),
                        pltpu.VMEM((nh, tq, MLA_V), F32)],
        compiler_params=_params(("arbitrary", "arbitrary", "arbitrary")),
        name="mla_prompt",
    )(q, kv, krp, slab)


def _headproj_kernel(x_ref, w_ref, *rest, gated):
    if gated:
        g_ref, o_ref = rest
    else:
        (o_ref,) = rest
    y = jnp.dot(x_ref[...], w_ref[0], preferred_element_type=F32)
    if gated:
        y = y * _silu(g_ref[...].astype(F32))
    o_ref[...] = y.astype(o_ref.dtype)


def _headproj(x, w, *, x_stride, gate=None, gate_blk0=0):
    m = x.shape[0]
    nh, kdim, n = w.shape
    in_specs = [pl.BlockSpec((m, kdim), lambda h: (0, h * x_stride)),
                pl.BlockSpec((1, kdim, n), lambda h: (h, 0, 0))]
    args = [x, w]
    if gate is not None:
        in_specs.append(pl.BlockSpec((m, n), lambda h: (0, gate_blk0 + h)))
        args.append(gate)
    return pl.pallas_call(
        functools.partial(_headproj_kernel, gated=gate is not None),
        grid=(nh,),
        in_specs=in_specs,
        out_specs=pl.BlockSpec((m, n), lambda h: (0, h)),
        out_shape=jax.ShapeDtypeStruct((m, nh * n), BF16),
        compiler_params=_params(("arbitrary",)),
        name="headproj",
    )(*args)


def _page_copies(pt_ref, cache_c, cache_rt, cbuf, rbuf, sem, idx, slot, *, g, n_steps, page):
    b = idx // n_steps
    first = (idx % n_steps) * g
    out = []
    for j in range(g):
        p = pt_ref[b, first + j]
        out.append(pltpu.make_async_copy(cache_c.at[p], cbuf.at[slot, pl.ds(j * page, page), :], sem.at[slot, 0]))
        out.append(pltpu.make_async_copy(cache_rt.at[p], rbuf.at[slot, :, pl.ds(j * page, page)], sem.at[slot, 1]))
    return out


def _mla_sample_kernel(pt_ref, ql_ref, qr_ref, cn_ref, rn_ref, cache_c, cache_rt, o_ref,
                       cbuf, rbuf, sem, m_ref, l_ref, acc_ref, *, t, page, g, n_steps):
    i = pl.program_id(0)
    last = pl.num_programs(0) - 1
    step = i % n_steps
    slot = i % N_SLOTS
    copies = functools.partial(_page_copies, pt_ref, cache_c, cache_rt, cbuf, rbuf, sem,
                               g=g, n_steps=n_steps, page=page)

    @pl.when(i == 0)
    def _():
        for ahead in range(N_SLOTS - 1):
            for c in copies(jnp.minimum(ahead, last), ahead):
                c.start()

    ql = ql_ref[0]
    qr = qr_ref[0]
    rows = ql.shape[0]

    @pl.when(step == 0)
    def _():
        qlf = ql.astype(F32)
        qrf = qr.astype(F32)
        tok = lax.broadcasted_iota(jnp.int32, (rows, 1), 0) % t
        cn = cn_ref[0].astype(BF16).astype(F32)
        rn = rn_ref[0].astype(BF16).astype(F32)
        s_cols = []
        for j in range(t):
            sj = (jnp.sum(qlf * cn[j:j + 1, :], axis=-1, keepdims=True)
                  + jnp.sum(qrf * rn[j:j + 1, :], axis=-1, keepdims=True))
            s_cols.append(jnp.where(tok >= j, sj, -jnp.inf))
        m0 = s_cols[0]
        for j in range(1, t):
            m0 = jnp.maximum(m0, s_cols[j])
        l0 = jnp.zeros((rows, 1), F32)
        a0 = jnp.zeros((rows, KV_LORA), F32)
        for j in range(t):
            pj = jnp.exp2(s_cols[j] - m0)
            l0 = l0 + pj
            a0 = a0 + pj.astype(BF16).astype(F32) * cn[j:j + 1, :]
        lane = lax.broadcasted_iota(jnp.int32, (rows, LANE), 1)
        m_ref[...] = jnp.broadcast_to(m0, (rows, LANE))
        l_ref[...] = jnp.where(lane == 0, l0, 0.0)
        acc_ref[...] = a0

    for c in copies(i, slot):
        c.wait()
    sub = g * page // KEY_SPLITS
    cbs, parts = [], []
    for k in range(KEY_SPLITS):
        cb = cbuf[slot, k * sub:(k + 1) * sub, :].astype(BF16)
        rb = rbuf[slot, :, k * sub:(k + 1) * sub].astype(BF16)
        cbs.append(cb)
        parts.append(lax.dot_general(ql, cb, _NT, preferred_element_type=F32)
                     + jnp.dot(qr, rb, preferred_element_type=F32))
    s = jnp.concatenate(parts, axis=1)
    m_old = m_ref[...]
    m_new = jnp.maximum(m_old, _lane_rowmax(s))
    alpha = jnp.exp2(m_old - m_new)
    pb, l_part = _lane_exp(s, m_new)
    m_ref[...] = m_new
    l_ref[...] = alpha * l_ref[...] + l_part
    alpha_w = jnp.concatenate([alpha] * (KV_LORA // LANE), axis=1)
    acc = alpha_w * acc_ref[...]
    for k in range(KEY_SPLITS):
        acc = acc + jnp.dot(pb[:, k * sub:(k + 1) * sub], cbs[k], preferred_element_type=F32)
    acc_ref[...] = acc

    ahead = N_SLOTS - 1
    for c in copies(jnp.minimum(i + ahead, last), (i + ahead) % N_SLOTS):
        c.start()

    @pl.when(step == n_steps - 1)
    def _():
        o_ref[0] = acc_ref[...] / jnp.sum(l_ref[...], axis=-1, keepdims=True)

    @pl.when(i == last)
    def _():
        for extra in range(1, N_SLOTS):
            for c in copies(last, (last + extra) % N_SLOTS):
                c.wait()


def _mla_sample(page_table, ql, qr, c_new, r_new, cache_c, cache_rt, *, t):
    batch, n_pages = page_table.shape
    page = cache_c.shape[1]
    g = PAGES_PER_STEP
    n_steps = n_pages // g
    rows = ql.shape[1]
    per_b = lambda i, pt: (i // n_steps, 0, 0)
    grid_spec = pltpu.PrefetchScalarGridSpec(
        num_scalar_prefetch=1,
        grid=(batch * n_steps,),
        in_specs=[
            pl.BlockSpec((1, rows, KV_LORA), per_b),
            pl.BlockSpec((1, rows, MLA_ROPE), per_b),
            pl.BlockSpec((1, t, KV_LORA), per_b),
            pl.BlockSpec((1, t, MLA_ROPE), per_b),
            pl.BlockSpec(memory_space=pl.ANY),
            pl.BlockSpec(memory_space=pl.ANY),
        ],
        out_specs=pl.BlockSpec((1, rows, KV_LORA), per_b),
        scratch_shapes=[
            pltpu.VMEM((N_SLOTS, g * page, KV_LORA), F32),
            pltpu.VMEM((N_SLOTS, MLA_ROPE, g * page), F32),
            pltpu.SemaphoreType.DMA((N_SLOTS, 2)),
            pltpu.VMEM((rows, LANE), F32), pltpu.VMEM((rows, LANE), F32), pltpu.VMEM((rows, KV_LORA), F32),
        ],
    )
    return pl.pallas_call(
        functools.partial(_mla_sample_kernel, t=t, page=page, g=g, n_steps=n_steps),
        grid_spec=grid_spec,
        out_shape=jax.ShapeDtypeStruct((batch, rows, KV_LORA), F32),
        compiler_params=_params(("arbitrary",)),
        name="mla_sample",
    )(page_table, ql, qr, c_new, r_new, cache_c, cache_rt)


def _finish_kernel(x_ref, ar_ref, am_ref, ga_ref, gb_ref, ple_ref, wdr_ref, wdm_ref, wo_ref, wpg_ref, wpp_ref,
                   fw_ref, o_ref, *, final):
    h_ret = jnp.dot(ar_ref[...], wdr_ref[...], preferred_element_type=F32)
    h_mla = jnp.dot(am_ref[...], wdm_ref[...], preferred_element_type=F32)
    merged = (jax.nn.sigmoid(ga_ref[...].astype(F32)) * h_ret
              + jax.nn.sigmoid(gb_ref[...].astype(F32)) * h_mla)
    x2 = x_ref[...] + jnp.dot(merged.astype(BF16), wo_ref[...], preferred_element_type=F32)
    gate = jax.nn.sigmoid(jnp.dot(x2.astype(BF16), wpg_ref[...], preferred_element_type=F32))
    y = x2 + gate * jnp.dot(ple_ref[...].astype(BF16), wpp_ref[...], preferred_element_type=F32)
    if final:
        y = _rms(y, fw_ref[...])
    o_ref[...] = y


def _finish(x, act_ret, act_mla, slab, ple, wdr, wdm, wo, wpg, wpp, fw, *, tm, final):
    m, d = x.shape
    row = lambda i: (i, 0)
    fixed = lambda i: (0, 0)
    return pl.pallas_call(
        functools.partial(_finish_kernel, final=final),
        grid=(m // tm,),
        in_specs=[
            pl.BlockSpec((tm, d), row),
            pl.BlockSpec((tm, RET_VW), row),
            pl.BlockSpec((tm, MLA_VW), row),
            pl.BlockSpec((tm, d), lambda i: (i, _S_GA // D_MODEL)),
            pl.BlockSpec((tm, d), lambda i: (i, _S_GB // D_MODEL)),
            pl.BlockSpec((tm, PLE_DIM), row),
            pl.BlockSpec(wdr.shape, fixed),
            pl.BlockSpec(wdm.shape, fixed),
            pl.BlockSpec(wo.shape, fixed),
            pl.BlockSpec(wpg.shape, fixed),
            pl.BlockSpec(wpp.shape, fixed),
            pl.BlockSpec((1, d), fixed),
        ],
        out_specs=pl.BlockSpec((tm, d), row),
        out_shape=jax.ShapeDtypeStruct((m, d), F32),
        compiler_params=_params(("arbitrary",)),
        name="finish",
    )(x, act_ret, act_mla, slab, slab, ple, wdr, wdm, wo, wpg, wpp, fw)


def _rope_tables(pos, half):
    inv_freq = ROPE_BASE ** (-jnp.arange(half, dtype=F32) / half)
    ang = pos[:, None] * inv_freq[None, :]
    return jnp.cos(ang), jnp.sin(ang)


def _mla_rope_tables(pos):
    cos, sin = _rope_tables(pos, MLA_ROPE // 2)
    z = jnp.zeros((pos.shape[0], LANE - MLA_ROPE), F32)
    return jnp.concatenate([cos, cos, z], axis=1), jnp.concatenate([sin, sin, z], axis=1)


def _decay_tables(c):
    log_g = jnp.log1p(-jnp.exp2(-5.0 - jnp.arange(RET_HEADS, dtype=F32)))
    idx = jnp.arange(c, dtype=F32)
    diff = idx[:, None] - idx[None, :]
    causal = (diff >= 0)[None]
    dmask = jnp.where(causal, jnp.exp(jnp.where(causal, diff[None], 0.0) * log_g[:, None, None]), 0.0)
    qdec = jnp.exp((idx[None, :] + 1.0) * log_g[:, None])[:, :, None]
    kdec = jnp.exp((c - 1.0 - idx)[None, :] * log_g[:, None])[:, :, None]
    qdec = jnp.broadcast_to(qdec, (RET_HEADS, c, RET_DV))
    kdec = jnp.broadcast_to(kdec, (RET_HEADS, c, RET_DK))
    gcs = jnp.broadcast_to(jnp.exp(c * log_g)[:, None, None], (RET_HEADS, 1, RET_DV))
    return dmask, qdec, kdec, gcs


def _pad_rows(a, rows):
    return jnp.pad(a, ((0, 0), (0, rows - a.shape[1])) + ((0, 0),) * (a.ndim - 2))


def _rot_cols(w):
    half = w.shape[-1] // 2
    return jnp.concatenate([-w[..., half:], w[..., :half]], axis=-1)


def _layer_weights(w_in, w_uq, w_ukv, w_down_ret, w_down_mla, w_out, w_ple_gate, w_ple_proj):
    d = w_in.shape[0]
    w_qk = w_in[:, _OFF_RQ:_OFF_RV].astype(BF16)
    w_slab = jnp.concatenate([w_in[:, _OFF_RV:_OFF_CQ], w_in[:, _OFF_MG:]], axis=1).astype(BF16)
    kr = w_in[:, _OFF_KR:_OFF_MG]
    zpad = jnp.zeros((d, LANE - MLA_ROPE), F32)
    w_c = jnp.concatenate([w_in[:, _OFF_CQ:_OFF_KR], kr, zpad, _rot_cols(kr), zpad], axis=1).astype(BF16)
    uq = w_uq.reshape(Q_LORA, MLA_HEADS, MLA_QK)
    uq_rope = uq[:, :, MLA_NOPE:]
    zq = jnp.zeros((Q_LORA, MLA_HEADS, MLA_QPAD - MLA_QK), F32)
    w_qa = jnp.concatenate([uq, zq], axis=2).reshape(Q_LORA, MLA_HEADS * MLA_QPAD).astype(BF16)
    w_qb = jnp.concatenate([_rot_cols(uq_rope), zq], axis=2).reshape(Q_LORA, MLA_HEADS * LANE).astype(BF16)
    ukv = w_ukv.reshape(KV_LORA, MLA_HEADS, MLA_KVW)
    w_uk_t = jnp.transpose(ukv[:, :, :MLA_NOPE], (1, 2, 0)).astype(BF16)
    w_uv = jnp.transpose(ukv[:, :, MLA_NOPE:], (1, 0, 2)).astype(BF16)
    return dict(w_qk=w_qk, w_slab=w_slab, w_c=w_c, w_qa=w_qa, w_qb=w_qb, w_kv=w_ukv.astype(BF16),
                w_uk_t=w_uk_t, w_uv=w_uv, wdr=w_down_ret.astype(BF16), wdm=w_down_mla.astype(BF16),
                wo=w_out.astype(BF16), wpg=w_ple_gate.astype(BF16), wpp=w_ple_proj.astype(BF16))


def _row_tile(m, cap):
    t = min(m, cap)
    assert m % t == 0
    return t


def _project(x, ln_w, lw, cos_r, sin_r):
    tm = _row_tile(cos_r.shape[0], 1024)
    assert x.shape[0] % tm == 0
    qk = _inproj(x, ln_w, lw["w_qk"], cos_r, sin_r, tm=tm, tn=512, n_q=RET_QK // 512, k_scale=RET_DK ** -0.5)
    slab = _inproj(x, ln_w, lw["w_slab"], tm=tm, tn=1024)
    return qk, slab


def kernel(x_prompt, x_sample, cache_ckv, cache_krope, state_ret, page_table, p_prompt, p_sample, ln_w, w_in,
           q_norm_w, w_uq, kv_norm_w, w_ukv, ret_gn_w, w_down_ret, w_down_mla, w_out, w_ple_gate, w_ple_proj,
           final_norm_w):
    depth = w_in.shape[0]
    b, s, d = x_prompt.shape
    bd, t, _ = x_sample.shape
    n_pages = page_table.shape[1]
    page = cache_ckv.shape[2]
    past_len = n_pages * page
    assert s % RET_CHUNK == 0 and n_pages % PAGES_PER_STEP == 0

    pos_p = jnp.arange(s, dtype=F32)
    pos_s = jnp.tile(past_len + jnp.arange(t, dtype=F32), bd)
    cos_rp, sin_rp = _rope_tables(pos_p, RET_DK // 2)
    cos_rs, sin_rs = _rope_tables(pos_s, RET_DK // 2)
    cos_mp, sin_mp = _mla_rope_tables(pos_p)
    cos_ms, sin_ms = _mla_rope_tables(pos_s)
    dec_p = _decay_tables(RET_CHUNK)
    dec_s = _decay_tables(t)
    dmask_s = jnp.pad(dec_s[0], ((0, 0), (0, RET_CHUNK - t), (0, RET_CHUNK - t)))
    qdec_s = _pad_rows(dec_s[1], RET_CHUNK)
    kdec_s = _pad_rows(dec_s[2], RET_CHUNK)
    gcs_s = dec_s[3]

    y_p = x_prompt.reshape(b * s, d)
    y_s = x_sample.reshape(bd * t, d)
    tq = _row_tile(s, 512)
    tk = tq
    outs = [[] for _ in range(6)]
    for i in range(depth):
        lw = _layer_weights(w_in[i], w_uq[i], w_ukv[i], w_down_ret[i], w_down_mla[i], w_out[i], w_ple_gate[i],
                            w_ple_proj[i])
        lnw = ln_w[i][None, :]
        qnw = q_norm_w[i][None, :]
        kvnw = kv_norm_w[i][None, :]
        gnw = ret_gn_w[i][None, :]
        fw = final_norm_w[None, :]
        final = i == depth - 1

        qk, slab = _project(y_p, lnw, lw, cos_rp, sin_rp)
        q, ckv, kr, krp, kv = _mla_prep(y_p, lnw, lw["w_c"], qnw, lw["w_qa"], lw["w_qb"], kvnw, cos_mp, sin_mp,
                                        lw["w_kv"], tm=_row_tile(s, 512), with_kv=True)
        act_ret, s_fin = _ret_prompt(qk, slab, gnw, *dec_p, batch=b, seq=s)
        act_mla = _mla_prompt(q, kv, krp, slab, batch=b, seq=s, tq=tq, tk=tk)
        y_p = _finish(y_p, act_ret, act_mla, slab, p_prompt[i].reshape(b * s, -1), lw["wdr"], lw["wdm"], lw["wo"],
                      lw["wpg"], lw["wpp"], fw, tm=_row_tile(b * s, 256), final=final)
        outs[0].append(ckv.reshape(b, s, KV_LORA))
        outs[1].append(kr.reshape(b, s, MLA_ROPE))
        outs[2].append(s_fin.astype(x_prompt.dtype))

        ms = bd * t
        qk, slab = _project(y_s, lnw, lw, cos_rs, sin_rs)
        q, ckv, kr = _mla_prep(y_s, lnw, lw["w_c"], qnw, lw["w_qa"], lw["w_qb"], kvnw, cos_ms, sin_ms, None,
                               tm=ms, with_kv=False)
        act_ret, s_new = _ret_sample(qk, slab, gnw, state_ret[i], dmask_s, qdec_s, kdec_s, gcs_s, batch=bd, t=t)
        q_lat = _headproj(q, lw["w_uk_t"], x_stride=MLA_QPAD // LANE)
        q_lat = q_lat.reshape(bd, t, MLA_HEADS, KV_LORA).transpose(0, 2, 1, 3).reshape(bd, MLA_HEADS * t, KV_LORA)
        q_rope = q.reshape(bd, t, MLA_HEADS, MLA_QPAD)[..., MLA_NOPE:MLA_QK]
        q_rope = q_rope.transpose(0, 2, 1, 3).reshape(bd, MLA_HEADS * t, MLA_ROPE)
        cache_rt = jnp.swapaxes(cache_krope[i], 1, 2)
        o_lat = _mla_sample(page_table, q_lat, q_rope, ckv.reshape(bd, t, KV_LORA), kr.reshape(bd, t, MLA_ROPE),
                            cache_ckv[i], cache_rt, t=t)
        o_lat = o_lat.reshape(bd, MLA_HEADS, t, KV_LORA).transpose(0, 2, 1, 3).reshape(ms, MLA_HEADS * KV_LORA)
        act_mla = _headproj(o_lat.astype(BF16), lw["w_uv"], x_stride=1, gate=slab, gate_blk0=_S_MG // MLA_V)
        y_s = _finish(y_s, act_ret, act_mla, slab, p_sample[i].reshape(ms, -1), lw["wdr"], lw["wdm"], lw["wo"],
                      lw["wpg"], lw["wpp"], fw, tm=_row_tile(ms, 256), final=final)
        outs[3].append(ckv.reshape(bd, t, KV_LORA))
        outs[4].append(kr.reshape(bd, t, MLA_ROPE))
        outs[5].append(s_new.astype(state_ret.dtype))

    y_prompt = y_p.reshape(b, s, d)
    y_sample = y_s.reshape(bd, t, d)
    ckv_p, kr_p, ret_p, ckv_s, kr_s, ret_s = [jnp.stack(o) for o in outs]
    return (y_prompt, y_sample, ckv_p, kr_p, ret_p, ckv_s, kr_s, ret_s)
```

```python
import functools
import math

import numpy as np
import jax
import jax.numpy as jnp
from jax import lax
from jax.experimental import pallas as pl
from jax.experimental.pallas import tpu as pltpu

F32 = jnp.float32
BF16 = jnp.bfloat16

D_MODEL = 1024
PLE_DIM = 256
RET_HEADS = 4
RET_DK = 256
RET_DV = 512
RET_CHUNK = 128
MLA_HEADS = 8
MLA_NOPE = 128
MLA_ROPE = 64
MLA_V = 128
Q_LORA = 384
KV_LORA = 256
ROPE_BASE = 10000.0
EPS = 1e-6

RET_QK = RET_HEADS * RET_DK
RET_VW = RET_HEADS * RET_DV
MLA_QK = MLA_NOPE + MLA_ROPE
MLA_VW = MLA_HEADS * MLA_V
MLA_QPAD = 256
MLA_KVW = MLA_NOPE + MLA_V
Q_SCALE = MLA_QK ** -0.5 * math.log2(math.e)
LANE = 128
PAGES_PER_STEP = 32
N_SLOTS = 3
KEY_SPLITS = 4
HEADS_PER_STEP = 2
WT_TILE = 512
WT_ROWS = 64
RET_SAMPLE_SEQS = 2
VMEM_LIMIT = 48 * 1024 * 1024

_OFF_RQ = 0
_OFF_RK = _OFF_RQ + RET_QK
_OFF_RV = _OFF_RK + RET_QK
_OFF_RG = _OFF_RV + RET_VW
_OFF_CQ = _OFF_RG + RET_VW
_OFF_CKV = _OFF_CQ + Q_LORA
_OFF_KR = _OFF_CKV + KV_LORA
_OFF_MG = _OFF_KR + MLA_ROPE
_OFF_GA = _OFF_MG + MLA_VW
_OFF_GB = _OFF_GA + D_MODEL
_S_RV = 0
_S_RG = _S_RV + RET_VW
_S_MG = _S_RG + RET_VW
_S_GA = _S_MG + MLA_VW
_S_GB = _S_GA + D_MODEL
_S_W = _S_GB + D_MODEL

_NT = (((1,), (1,)), ((), ()))
_TN = (((0,), (0,)), ((), ()))


def _params(sem):
    return pltpu.CompilerParams(dimension_semantics=sem, vmem_limit_bytes=VMEM_LIMIT)


def _rms(x, w):
    return x * lax.rsqrt(jnp.mean(x * x, axis=-1, keepdims=True) + EPS) * w


def _silu(x):
    return x * jax.nn.sigmoid(x)


def _lane_blocks(s):
    return [s[:, c * LANE:(c + 1) * LANE] for c in range(s.shape[1] // LANE)]


def _lane_rowmax(s):
    blocks = _lane_blocks(s)
    mx = blocks[0]
    for blk in blocks[1:]:
        mx = jnp.maximum(mx, blk)
    return jnp.broadcast_to(jnp.max(mx, axis=-1, keepdims=True), mx.shape)


def _lane_exp(s, m):
    ps = [jnp.exp2(blk - m) for blk in _lane_blocks(s)]
    part = ps[0]
    for p in ps[1:]:
        part = part + p
    return jnp.concatenate([p.astype(BF16) for p in ps], axis=1), part


def _inproj_kernel(x_ref, lnw_ref, w_ref, *rest, rope, n_q, k_scale, tn):
    if rope:
        cos_ref, sin_ref, o_ref, xn_ref = rest
    else:
        o_ref, xn_ref = rest
    j = pl.program_id(1)

    @pl.when(j == 0)
    def _():
        xn_ref[...] = _rms(x_ref[...], lnw_ref[...]).astype(BF16)

    acc = jnp.dot(xn_ref[...], w_ref[...], preferred_element_type=F32)
    if not rope:
        o_ref[...] = acc.astype(o_ref.dtype)
        return
    scale = jnp.where(j >= n_q, k_scale, 1.0).astype(F32)
    cos = cos_ref[...] * scale
    sin = sin_ref[...] * scale
    for h in range(tn // RET_DK):
        lo = h * RET_DK
        a1 = acc[:, lo:lo + LANE]
        a2 = acc[:, lo + LANE:lo + RET_DK]
        o_ref[:, lo:lo + LANE] = (a1 * cos - a2 * sin).astype(o_ref.dtype)
        o_ref[:, lo + LANE:lo + RET_DK] = (a2 * cos + a1 * sin).astype(o_ref.dtype)


def _inproj(x, ln_w, w, cos=None, sin=None, *, col0, n, tm, tn, n_q=0, k_scale=1.0):
    m, d = x.shape
    rope = cos is not None
    blk0 = col0 // tn
    in_specs = [
        pl.BlockSpec((tm, d), lambda i, j: (i, 0)),
        pl.BlockSpec((1, d), lambda i, j: (0, 0)),
        pl.BlockSpec((d, tn), lambda i, j: (0, blk0 + j)),
    ]
    args = [x, ln_w, w]
    if rope:
        p_tiles = cos.shape[0] // tm
        in_specs += [pl.BlockSpec((tm, LANE), lambda i, j: (i % p_tiles, 0))] * 2
        args += [cos, sin]
    return pl.pallas_call(
        functools.partial(_inproj_kernel, rope=rope, n_q=n_q, k_scale=k_scale, tn=tn),
        grid=(m // tm, n // tn),
        in_specs=in_specs,
        out_specs=pl.BlockSpec((tm, tn), lambda i, j: (i, j)),
        out_shape=jax.ShapeDtypeStruct((m, n), BF16),
        scratch_shapes=[pltpu.VMEM((tm, d), BF16)],
        compiler_params=_params(("arbitrary", "arbitrary")),
        name="inproj_rope" if rope else "inproj",
    )(*args)


def _wt_kernel(tbl_ref, *refs):
    del tbl_ref
    o_ref = refs[-1]
    rows = jnp.concatenate([r[...] for r in refs[:-1]], axis=0)
    o_ref[...] = rows.T.astype(BF16)


def _transpose_weight(w_t, src_rows):
    d = w_t.shape[1]
    parts = WT_TILE // WT_ROWS
    assert all(r % WT_ROWS == 0 for r in src_rows)
    tbl = np.asarray([r // WT_ROWS for r in src_rows], np.int32)
    grid_spec = pltpu.PrefetchScalarGridSpec(
        num_scalar_prefetch=1,
        grid=(len(src_rows),),
        in_specs=[pl.BlockSpec((WT_ROWS, d), lambda j, tbl, r=r: (tbl[j] + r, 0)) for r in range(parts)],
        out_specs=pl.BlockSpec((d, WT_TILE), lambda j, tbl: (0, j)),
    )
    return pl.pallas_call(
        _wt_kernel,
        grid_spec=grid_spec,
        out_shape=jax.ShapeDtypeStruct((d, len(src_rows) * WT_TILE), BF16),
        compiler_params=_params(("arbitrary",)),
        name="wt",
    )(tbl, *([w_t] * parts))


def _mla_prep_kernel(x_ref, lnw_ref, wc_ref, qnw_ref, wqa_ref, wqb_ref, kvnw_ref, cos_ref, sin_ref, *rest,
                     with_kv):
    if with_kv:
        wkv_ref, q_ref, ckv_ref, kr_ref, krp_ref, kv_ref = rest
    else:
        q_ref, ckv_ref, kr_ref = rest
    xn = _rms(x_ref[...], lnw_ref[...]).astype(BF16)
    zc = jnp.dot(xn, wc_ref[...], preferred_element_type=F32)
    cos = cos_ref[...]
    sin = sin_ref[...]
    cqn = _rms(zc[:, :Q_LORA], qnw_ref[...]).astype(BF16)
    qa = jnp.dot(cqn, wqa_ref[...], preferred_element_type=F32)
    qb = jnp.dot(cqn, wqb_ref[...], preferred_element_type=F32)
    qcos = cos * Q_SCALE
    qsin = sin * Q_SCALE
    for h in range(MLA_HEADS):
        lo = h * MLA_QPAD
        q_ref[:, lo:lo + LANE] = (qa[:, lo:lo + LANE] * Q_SCALE).astype(q_ref.dtype)
        q_ref[:, lo + LANE:lo + MLA_QPAD] = (
            qa[:, lo + LANE:lo + MLA_QPAD] * qcos + qb[:, h * LANE:(h + 1) * LANE] * qsin).astype(q_ref.dtype)
    c0 = Q_LORA
    ckvn = _rms(zc[:, c0:c0 + KV_LORA], kvnw_ref[...])
    ckv_ref[...] = ckvn
    k0 = c0 + KV_LORA
    krp = zc[:, k0:k0 + LANE] * cos + zc[:, k0 + LANE:k0 + 2 * LANE] * sin
    kr_ref[...] = krp[:, :MLA_ROPE]
    if with_kv:
        krp_ref[...] = krp.astype(BF16)
        kv_ref[...] = jnp.dot(ckvn.astype(BF16), wkv_ref[...], preferred_element_type=F32).astype(BF16)


def _mla_prep(x, ln_w, wc, qnw, wqa, wqb, kvnw, cos, sin, wkv, *, tm, with_kv):
    m, d = x.shape
    p_tiles = cos.shape[0] // tm
    row = lambda i: (i, 0)
    fixed = lambda i: (0, 0)
    in_specs = [
        pl.BlockSpec((tm, d), row),
        pl.BlockSpec((1, d), fixed),
        pl.BlockSpec(wc.shape, fixed),
        pl.BlockSpec((1, Q_LORA), fixed),
        pl.BlockSpec(wqa.shape, fixed),
        pl.BlockSpec(wqb.shape, fixed),
        pl.BlockSpec((1, KV_LORA), fixed),
        pl.BlockSpec((tm, LANE), lambda i: (i % p_tiles, 0)),
        pl.BlockSpec((tm, LANE), lambda i: (i % p_tiles, 0)),
    ]
    args = [x, ln_w, wc, qnw, wqa, wqb, kvnw, cos, sin]
    qw = MLA_HEADS * MLA_QPAD
    out_specs = [pl.BlockSpec((tm, qw), row), pl.BlockSpec((tm, KV_LORA), row), pl.BlockSpec((tm, MLA_ROPE), row)]
    out_shape = [jax.ShapeDtypeStruct((m, qw), BF16), jax.ShapeDtypeStruct((m, KV_LORA), F32),
                 jax.ShapeDtypeStruct((m, MLA_ROPE), F32)]
    if with_kv:
        in_specs.append(pl.BlockSpec(wkv.shape, fixed))
        args.append(wkv)
        out_specs += [pl.BlockSpec((tm, LANE), row), pl.BlockSpec((tm, wkv.shape[1]), row)]
        out_shape += [jax.ShapeDtypeStruct((m, LANE), BF16), jax.ShapeDtypeStruct((m, wkv.shape[1]), BF16)]
    return pl.pallas_call(
        functools.partial(_mla_prep_kernel, with_kv=with_kv),
        grid=(m // tm,),
        in_specs=in_specs,
        out_specs=out_specs,
        out_shape=out_shape,
        compiler_params=_params(("arbitrary",)),
        name="mla_prep",
    )(*args)


def _ret_chunk(q, k, v, s_prev, dmask, qdec, kdec, gc):
    sc = lax.dot_general(q, k, _NT, preferred_element_type=F32) * dmask
    intra = jnp.dot(sc.astype(BF16), v, preferred_element_type=F32)
    cross = jnp.dot(q, s_prev.astype(BF16), preferred_element_type=F32) * qdec
    kd = (k.astype(F32) * kdec).astype(BF16)
    s_new = gc * s_prev + lax.dot_general(kd, v, _TN, preferred_element_type=F32)
    return intra + cross, s_new


def _ret_act(o, rg, gnw):
    mu = jnp.mean(o, axis=-1, keepdims=True)
    d = o - mu
    var = jnp.mean(d * d, axis=-1, keepdims=True)
    return d * lax.rsqrt(var + EPS) * gnw * _silu(rg.astype(F32))


def _ret_prompt_kernel(q_ref, k_ref, v_ref, rg_ref, gnw_ref, dm_ref, qd_ref, kd_ref, gc_ref,
                       o_ref, sfin_ref, s_ref, *, n_chunks, nh):
    s_ref[...] = jnp.zeros_like(s_ref)

    def chunk(c, carry):
        rows = pl.ds(pl.multiple_of(c * RET_CHUNK, RET_CHUNK), RET_CHUNK)
        for h in range(nh):
            qk_cols = slice(h * RET_DK, (h + 1) * RET_DK)
            v_cols = slice(h * RET_DV, (h + 1) * RET_DV)
            o, s_new = _ret_chunk(q_ref[rows, qk_cols], k_ref[rows, qk_cols], v_ref[rows, v_cols], s_ref[h],
                                  dm_ref[h], qd_ref[h], kd_ref[h], gc_ref[h])
            s_ref[h] = s_new
            o_ref[rows, v_cols] = _ret_act(o, rg_ref[rows, v_cols], gnw_ref[:, v_cols]).astype(o_ref.dtype)
        return carry

    lax.fori_loop(0, n_chunks, chunk, 0)
    sfin_ref[0] = s_ref[...]


def _ret_prompt(qk, slab, gn_w, dmask, qdec, kdec, gcs, *, batch, seq):
    nh = HEADS_PER_STEP
    qw, vw = nh * RET_DK, nh * RET_DV
    k_blk = RET_QK // qw
    rg_blk = _S_RG // vw
    return pl.pallas_call(
        functools.partial(_ret_prompt_kernel, n_chunks=seq // RET_CHUNK, nh=nh),
        grid=(batch, RET_HEADS // nh),
        in_specs=[
            pl.BlockSpec((seq, qw), lambda b, hp: (b, hp)),
            pl.BlockSpec((seq, qw), lambda b, hp: (b, k_blk + hp)),
            pl.BlockSpec((seq, vw), lambda b, hp: (b, hp)),
            pl.BlockSpec((seq, vw), lambda b, hp: (b, rg_blk + hp)),
            pl.BlockSpec((1, vw), lambda b, hp: (0, hp)),
            pl.BlockSpec((nh, RET_CHUNK, RET_CHUNK), lambda b, hp: (hp, 0, 0)),
            pl.BlockSpec((nh, RET_CHUNK, RET_DV), lambda b, hp: (hp, 0, 0)),
            pl.BlockSpec((nh, RET_CHUNK, RET_DK), lambda b, hp: (hp, 0, 0)),
            pl.BlockSpec((nh, 1, RET_DV), lambda b, hp: (hp, 0, 0)),
        ],
        out_specs=[
            pl.BlockSpec((seq, vw), lambda b, hp: (b, hp)),
            pl.BlockSpec((1, nh, RET_DK, RET_DV), lambda b, hp: (b, hp, 0, 0)),
        ],
        out_shape=[
            jax.ShapeDtypeStruct((batch * seq, RET_VW), BF16),
            jax.ShapeDtypeStruct((batch, RET_HEADS, RET_DK, RET_DV), F32),
        ],
        scratch_shapes=[pltpu.VMEM((nh, RET_DK, RET_DV), F32)],
        compiler_params=_params(("arbitrary", "arbitrary")),
        name="ret_prompt",
    )(qk, qk, slab, slab, gn_w, dmask, qdec, kdec, gcs)


def _ret_sample_kernel(qk_ref, vg_ref, gnw_ref, st_ref, dm_ref, qd_ref, kd_ref, gc_ref,
                       o_ref, snew_ref, qp_ref, kp_ref, vp_ref, *, t, nb):
    qp_ref[...] = jnp.zeros_like(qp_ref)
    kp_ref[...] = jnp.zeros_like(kp_ref)
    vp_ref[...] = jnp.zeros_like(vp_ref)
    for bb in range(nb):
        for h in range(RET_HEADS):
            qp_ref[0:t, :] = qk_ref[bb, :, h * RET_DK:(h + 1) * RET_DK].astype(F32)
            kp_ref[0:t, :] = qk_ref[bb, :, RET_QK + h * RET_DK:RET_QK + (h + 1) * RET_DK].astype(F32)
            vp_ref[0:t, :] = vg_ref[bb, :, _S_RV + h * RET_DV:_S_RV + (h + 1) * RET_DV].astype(F32)
            o, s_new = _ret_chunk(qp_ref[...].astype(BF16), kp_ref[...].astype(BF16), vp_ref[...].astype(BF16),
                                  st_ref[bb, h], dm_ref[h], qd_ref[h], kd_ref[h], gc_ref[h])
            snew_ref[bb, h] = s_new
            act = _ret_act(o[0:t, :], vg_ref[bb, :, _S_RG + h * RET_DV:_S_RG + (h + 1) * RET_DV],
                           gnw_ref[:, h * RET_DV:(h + 1) * RET_DV])
            o_ref[bb, :, h * RET_DV:(h + 1) * RET_DV] = act.astype(o_ref.dtype)


def _ret_sample(qk, slab, gn_w, state, dmask, qdec, kdec, gcs, *, batch, t):
    h = RET_HEADS
    qk3 = qk.reshape(batch, t, qk.shape[1])
    slab3 = slab.reshape(batch, t, slab.shape[1])
    whole = lambda b: (0, 0, 0)
    nb = _row_tile(batch, RET_SAMPLE_SEQS)
    out, s_new = pl.pallas_call(
        functools.partial(_ret_sample_kernel, t=t, nb=nb),
        grid=(batch // nb,),
        in_specs=[
            pl.BlockSpec((nb, t, qk.shape[1]), lambda b: (b, 0, 0)),
            pl.BlockSpec((nb, t, _S_MG), lambda b: (b, 0, 0)),
            pl.BlockSpec((1, RET_VW), lambda b: (0, 0)),
            pl.BlockSpec((nb, h, RET_DK, RET_DV), lambda b: (b, 0, 0, 0)),
            pl.BlockSpec(dmask.shape, whole),
            pl.BlockSpec(qdec.shape, whole),
            pl.BlockSpec(kdec.shape, whole),
            pl.BlockSpec(gcs.shape, whole),
        ],
        out_specs=[
            pl.BlockSpec((nb, t, RET_VW), lambda b: (b, 0, 0)),
            pl.BlockSpec((nb, h, RET_DK, RET_DV), lambda b: (b, 0, 0, 0)),
        ],
        out_shape=[
            jax.ShapeDtypeStruct((batch, t, RET_VW), BF16),
            jax.ShapeDtypeStruct(state.shape, state.dtype),
        ],
        scratch_shapes=[pltpu.VMEM((RET_CHUNK, RET_DK), F32), pltpu.VMEM((RET_CHUNK, RET_DK), F32),
                        pltpu.VMEM((RET_CHUNK, RET_DV), F32)],
        compiler_params=_params(("arbitrary",)),
        name="ret_sample",
    )(qk3, slab3, gn_w, state, dmask, qdec, kdec, gcs)
    return out.reshape(batch * t, RET_VW), s_new


def _mla_prompt_kernel(q_ref, kv_ref, krp_ref, mg_ref, o_ref, m_ref, l_ref, acc_ref, *, tq, tk, nh):
    qi = pl.program_id(2)
    ratio = tq // tk

    def scores(kb, h):
        rows = pl.ds(pl.multiple_of(kb * tk, tk), tk)
        k = jnp.concatenate([kv_ref[rows, h * MLA_KVW:h * MLA_KVW + MLA_NOPE], krp_ref[rows, :]], axis=1)
        s = lax.dot_general(q_ref[:, h * MLA_QPAD:(h + 1) * MLA_QPAD], k, _NT, preferred_element_type=F32)
        return s, kv_ref[rows, h * MLA_KVW + MLA_NOPE:(h + 1) * MLA_KVW]

    def update(h, s, v, first=False):
        m_cur = _lane_rowmax(s)
        if first:
            m_new = m_cur
        else:
            m_old = m_ref[h]
            m_new = jnp.maximum(m_old, m_cur)
            alpha = jnp.exp2(m_old - m_new)
        p, l_part = _lane_exp(s, m_new)
        pv = jnp.dot(p, v, preferred_element_type=F32)
        m_ref[h] = m_new
        l_ref[h] = l_part if first else alpha * l_ref[h] + l_part
        acc_ref[h] = pv if first else alpha * acc_ref[h] + pv

    r_id = lax.broadcasted_iota(jnp.int32, (tq, tk), 0)
    c_id = lax.broadcasted_iota(jnp.int32, (tq, tk), 1)
    for d in range(ratio):
        for h in range(nh):
            s, v = scores(qi * ratio + d, h)
            update(h, jnp.where(c_id + d * tk <= r_id, s, -jnp.inf), v, first=d == 0)

    def body(kb, carry):
        for h in range(nh):
            s, v = scores(kb, h)
            update(h, s, v)
        return carry

    lax.fori_loop(0, qi * ratio, body, 0)
    for h in range(nh):
        cols = slice(h * MLA_V, (h + 1) * MLA_V)
        l = jnp.sum(l_ref[h], axis=-1, keepdims=True)
        o_ref[:, cols] = (acc_ref[h] / l * _silu(mg_ref[:, cols].astype(F32))).astype(o_ref.dtype)


def _mla_prompt(q, kv, krp, slab, *, batch, seq, tq, tk):
    nq = seq // tq
    nh = HEADS_PER_STEP
    mg_blk = _S_MG // (nh * MLA_V)
    return pl.pallas_call(
        functools.partial(_mla_prompt_kernel, tq=tq, tk=tk, nh=nh),
        grid=(batch, MLA_HEADS // nh, nq),
        in_specs=[
            pl.BlockSpec((tq, nh * MLA_QPAD), lambda b, hp, i: (b * nq + i, hp)),
            pl.BlockSpec((seq, nh * MLA_KVW), lambda b, hp, i: (b, hp)),
            pl.BlockSpec((seq, LANE), lambda b, hp, i: (b, 0)),
            pl.BlockSpec((tq, nh * MLA_V), lambda b, hp, i: (b * nq + i, mg_blk + hp)),
        ],
        out_specs=pl.BlockSpec((tq, nh * MLA_V), lambda b, hp, i: (b * nq + i, hp)),
        out_shape=jax.ShapeDtypeStruct((batch * seq, MLA_VW), BF16),
        scratch_shapes=[pltpu.VMEM((nh, tq, LANE), F32), pltpu.VMEM((nh, tq, LANE), F32),
                        pltpu.VMEM((nh, tq, MLA_V), F32)],
        compiler_params=_params(("arbitrary", "arbitrary", "arbitrary")),
        name="mla_prompt",
    )(q, kv, krp, slab)


def _headproj_kernel(x_ref, w_ref, *rest, gated):
    if gated:
        g_ref, o_ref = rest
    else:
        (o_ref,) = rest
    y = jnp.dot(x_ref[...], w_ref[0], preferred_element_type=F32)
    if gated:
        y = y * _silu(g_ref[...].astype(F32))
    o_ref[...] = y.astype(o_ref.dtype)


def _headproj(x, w, *, x_stride, gate=None, gate_blk0=0):
    m = x.shape[0]
    nh, kdim, n = w.shape
    in_specs = [pl.BlockSpec((m, kdim), lambda h: (0, h * x_stride)),
                pl.BlockSpec((1, kdim, n), lambda h: (h, 0, 0))]
    args = [x, w]
    if gate is not None:
        in_specs.append(pl.BlockSpec((m, n), lambda h: (0, gate_blk0 + h)))
        args.append(gate)
    return pl.pallas_call(
        functools.partial(_headproj_kernel, gated=gate is not None),
        grid=(nh,),
        in_specs=in_specs,
        out_specs=pl.BlockSpec((m, n), lambda h: (0, h)),
        out_shape=jax.ShapeDtypeStruct((m, nh * n), BF16),
        compiler_params=_params(("arbitrary",)),
        name="headproj",
    )(*args)


def _page_copies(pt_ref, cache_c, cache_rt, cbuf, rbuf, sem, idx, slot, *, g, n_steps, page):
    b = idx // n_steps
    first = (idx % n_steps) * g
    out = []
    for j in range(g):
        p = pt_ref[b, first + j]
        out.append(pltpu.make_async_copy(cache_c.at[p], cbuf.at[slot, pl.ds(j * page, page), :], sem.at[slot, 0]))
        out.append(pltpu.make_async_copy(cache_rt.at[p], rbuf.at[slot, j], sem.at[slot, 1]))
    return out


def _mla_sample_kernel(pt_ref, ql_ref, qr_ref, cn_ref, rn_ref, cache_c, cache_rt, o_ref,
                       cbuf, rbuf, sem, m_ref, l_ref, acc_ref, *, t, page, g, n_steps):
    i = pl.program_id(0)
    last = pl.num_programs(0) - 1
    step = i % n_steps
    slot = i % N_SLOTS
    copies = functools.partial(_page_copies, pt_ref, cache_c, cache_rt, cbuf, rbuf, sem,
                               g=g, n_steps=n_steps, page=page)

    @pl.when(i == 0)
    def _():
        for ahead in range(N_SLOTS - 1):
            for c in copies(jnp.minimum(ahead, last), ahead):
                c.start()

    ql = ql_ref[0]
    qr = qr_ref[0]
    rows = ql.shape[0]

    @pl.when(step == 0)
    def _():
        qlf = ql.astype(F32)
        qrf = qr.astype(F32)
        tok = lax.broadcasted_iota(jnp.int32, (rows, 1), 0) % t
        cn = cn_ref[0].astype(BF16).astype(F32)
        rn = rn_ref[0].astype(BF16).astype(F32)
        s_cols = []
        for j in range(t):
            sj = (jnp.sum(qlf * cn[j:j + 1, :], axis=-1, keepdims=True)
                  + jnp.sum(qrf * rn[j:j + 1, :], axis=-1, keepdims=True))
            s_cols.append(jnp.where(tok >= j, sj, -jnp.inf))
        m0 = s_cols[0]
        for j in range(1, t):
            m0 = jnp.maximum(m0, s_cols[j])
        l0 = jnp.zeros((rows, 1), F32)
        a0 = jnp.zeros((rows, KV_LORA), F32)
        for j in range(t):
            pj = jnp.exp2(s_cols[j] - m0)
            l0 = l0 + pj
            a0 = a0 + pj.astype(BF16).astype(F32) * cn[j:j + 1, :]
        lane = lax.broadcasted_iota(jnp.int32, (rows, LANE), 1)
        m_ref[...] = jnp.broadcast_to(m0, (rows, LANE))
        l_ref[...] = jnp.where(lane == 0, l0, 0.0)
        acc_ref[...] = a0

    for c in copies(i, slot):
        c.wait()
    ppk = g // KEY_SPLITS
    sub = ppk * page
    cbs, parts = [], []
    for k in range(KEY_SPLITS):
        cb = cbuf[slot, k * sub:(k + 1) * sub, :].astype(BF16)
        rb = jnp.concatenate([rbuf[slot, k * ppk + j] for j in range(ppk)], axis=1).astype(BF16)
        cbs.append(cb)
        parts.append(lax.dot_general(ql, cb, _NT, preferred_element_type=F32)
                     + jnp.dot(qr, rb, preferred_element_type=F32))
    s = jnp.concatenate(parts, axis=1)
    m_old = m_ref[...]
    m_new = jnp.maximum(m_old, _lane_rowmax(s))
    alpha = jnp.exp2(m_old - m_new)
    pb, l_part = _lane_exp(s, m_new)
    m_ref[...] = m_new
    l_ref[...] = alpha * l_ref[...] + l_part
    alpha_w = jnp.concatenate([alpha] * (KV_LORA // LANE), axis=1)
    pv = [jnp.dot(pb[:, k * sub:(k + 1) * sub], cbs[k], preferred_element_type=F32) for k in range(KEY_SPLITS)]
    while len(pv) > 1:
        pv = [pv[j] + pv[j + 1] for j in range(0, len(pv), 2)]
    acc_ref[...] = alpha_w * acc_ref[...] + pv[0]

    ahead = N_SLOTS - 1
    for c in copies(jnp.minimum(i + ahead, last), (i + ahead) % N_SLOTS):
        c.start()

    @pl.when(step == n_steps - 1)
    def _():
        o_ref[0] = acc_ref[...] / jnp.sum(l_ref[...], axis=-1, keepdims=True)

    @pl.when(i == last)
    def _():
        for extra in range(1, N_SLOTS):
            for c in copies(last, (last + extra) % N_SLOTS):
                c.wait()


def _mla_sample(page_table, ql, qr, c_new, r_new, cache_c, cache_rt, *, t):
    batch, n_pages = page_table.shape
    page = cache_c.shape[1]
    g = PAGES_PER_STEP
    n_steps = n_pages // g
    rows = ql.shape[1]
    per_b = lambda i, pt: (i // n_steps, 0, 0)
    grid_spec = pltpu.PrefetchScalarGridSpec(
        num_scalar_prefetch=1,
        grid=(batch * n_steps,),
        in_specs=[
            pl.BlockSpec((1, rows, KV_LORA), per_b),
            pl.BlockSpec((1, rows, MLA_ROPE), per_b),
            pl.BlockSpec((1, t, KV_LORA), per_b),
            pl.BlockSpec((1, t, MLA_ROPE), per_b),
            pl.BlockSpec(memory_space=pl.ANY),
            pl.BlockSpec(memory_space=pl.ANY),
        ],
        out_specs=pl.BlockSpec((1, rows, KV_LORA), per_b),
        scratch_shapes=[
            pltpu.VMEM((N_SLOTS, g * page, KV_LORA), F32),
            pltpu.VMEM((N_SLOTS, g, MLA_ROPE, page), F32),
            pltpu.SemaphoreType.DMA((N_SLOTS, 2)),
            pltpu.VMEM((rows, LANE), F32), pltpu.VMEM((rows, LANE), F32), pltpu.VMEM((rows, KV_LORA), F32),
        ],
    )
    return pl.pallas_call(
        functools.partial(_mla_sample_kernel, t=t, page=page, g=g, n_steps=n_steps),
        grid_spec=grid_spec,
        out_shape=jax.ShapeDtypeStruct((batch, rows, KV_LORA), F32),
        compiler_params=_params(("arbitrary",)),
        name="mla_sample",
    )(page_table, ql, qr, c_new, r_new, cache_c, cache_rt)


def _finish_kernel(x_ref, ar_ref, am_ref, ga_ref, gb_ref, ple_ref, wdr_ref, wdm_ref, wo_ref, wpg_ref, wpp_ref,
                   fw_ref, o_ref, *, final):
    h_ret = jnp.dot(ar_ref[...], wdr_ref[...], preferred_element_type=F32)
    h_mla = jnp.dot(am_ref[...], wdm_ref[...], preferred_element_type=F32)
    merged = (jax.nn.sigmoid(ga_ref[...].astype(F32)) * h_ret
              + jax.nn.sigmoid(gb_ref[...].astype(F32)) * h_mla)
    x2 = x_ref[...] + jnp.dot(merged.astype(BF16), wo_ref[...], preferred_element_type=F32)
    gate = jax.nn.sigmoid(jnp.dot(x2.astype(BF16), wpg_ref[...], preferred_element_type=F32))
    y = x2 + gate * jnp.dot(ple_ref[...].astype(BF16), wpp_ref[...], preferred_element_type=F32)
    if final:
        y = _rms(y, fw_ref[...])
    o_ref[...] = y


def _finish(x, act_ret, act_mla, slab, ple, wdr, wdm, wo, wpg, wpp, fw, *, tm, final):
    m, d = x.shape
    row = lambda i: (i, 0)
    fixed = lambda i: (0, 0)
    once = pl.Buffered(1)
    return pl.pallas_call(
        functools.partial(_finish_kernel, final=final),
        grid=(m // tm,),
        in_specs=[
            pl.BlockSpec((tm, d), row),
            pl.BlockSpec((tm, RET_VW), row),
            pl.BlockSpec((tm, MLA_VW), row),
            pl.BlockSpec((tm, d), lambda i: (i, _S_GA // D_MODEL)),
            pl.BlockSpec((tm, d), lambda i: (i, _S_GB // D_MODEL)),
            pl.BlockSpec((tm, PLE_DIM), row),
            pl.BlockSpec(wdr.shape, fixed, pipeline_mode=once),
            pl.BlockSpec(wdm.shape, fixed, pipeline_mode=once),
            pl.BlockSpec(wo.shape, fixed, pipeline_mode=once),
            pl.BlockSpec(wpg.shape, fixed, pipeline_mode=once),
            pl.BlockSpec(wpp.shape, fixed, pipeline_mode=once),
            pl.BlockSpec((1, d), fixed),
        ],
        out_specs=pl.BlockSpec((tm, d), row),
        out_shape=jax.ShapeDtypeStruct((m, d), F32),
        compiler_params=_params(("arbitrary",)),
        name="finish",
    )(x, act_ret, act_mla, slab, slab, ple, wdr, wdm, wo, wpg, wpp, fw)


def _rope_tables(pos, half):
    inv_freq = ROPE_BASE ** (-np.arange(half, dtype=np.float64) / half)
    ang = np.asarray(pos, np.float64)[:, None] * inv_freq[None, :]
    return np.cos(ang).astype(np.float32), np.sin(ang).astype(np.float32)


def _mla_rope_tables(pos):
    cos, sin = _rope_tables(pos, MLA_ROPE // 2)
    z = np.zeros((len(pos), LANE - MLA_ROPE), np.float32)
    return np.concatenate([cos, cos, z], axis=1), np.concatenate([sin, sin, z], axis=1)


def _decay_tables(c, rows):
    log_g = np.log1p(-np.exp2(-5.0 - np.arange(RET_HEADS, dtype=np.float64)))
    idx = np.arange(c, dtype=np.float64)
    diff = idx[:, None] - idx[None, :]
    dmask = np.where(diff >= 0, np.exp(np.maximum(diff, 0.0)[None] * log_g[:, None, None]), 0.0)
    qdec = np.exp((idx[None, :] + 1.0) * log_g[:, None])[:, :, None]
    kdec = np.exp((c - 1.0 - idx)[None, :] * log_g[:, None])[:, :, None]
    pad = rows - c
    dmask = np.pad(dmask, ((0, 0), (0, pad), (0, pad)))
    qdec = np.broadcast_to(np.pad(qdec, ((0, 0), (0, pad), (0, 0))), (RET_HEADS, rows, RET_DV))
    kdec = np.broadcast_to(np.pad(kdec, ((0, 0), (0, pad), (0, 0))), (RET_HEADS, rows, RET_DK))
    gcs = np.broadcast_to(np.exp(c * log_g)[:, None, None], (RET_HEADS, 1, RET_DV))
    return tuple(np.ascontiguousarray(a, np.float32) for a in (dmask, qdec, kdec, gcs))


def _rot_cols(w):
    half = w.shape[-1] // 2
    return jnp.concatenate([-w[..., half:], w[..., :half]], axis=-1)


def _layer_weights(w_in, w_uq, w_ukv, w_down_ret, w_down_mla, w_out, w_ple_gate, w_ple_proj):
    d = w_in.shape[0]
    src = [r for lo, hi in ((_OFF_RQ, _OFF_CQ), (_OFF_MG, w_in.shape[1])) for r in range(lo, hi, WT_TILE)]
    n_main = len(src) * WT_TILE
    src += list(range(_OFF_CQ, _OFF_MG, WT_TILE))
    w_main = _transpose_weight(w_in.T, src)
    c_cols = w_main[:, n_main:n_main + _OFF_MG - _OFF_CQ]
    kr = c_cols[:, _OFF_KR - _OFF_CQ:]
    zpad = jnp.zeros((d, LANE - MLA_ROPE), BF16)
    w_c = jnp.concatenate([c_cols, zpad, _rot_cols(kr), zpad], axis=1)
    uq = w_uq.reshape(Q_LORA, MLA_HEADS, MLA_QK)
    uq_rope = uq[:, :, MLA_NOPE:]
    zq = jnp.zeros((Q_LORA, MLA_HEADS, MLA_QPAD - MLA_QK), F32)
    w_qa = jnp.concatenate([uq, zq], axis=2).reshape(Q_LORA, MLA_HEADS * MLA_QPAD).astype(BF16)
    w_qb = jnp.concatenate([_rot_cols(uq_rope), zq], axis=2).reshape(Q_LORA, MLA_HEADS * LANE).astype(BF16)
    ukv = w_ukv.reshape(KV_LORA, MLA_HEADS, MLA_KVW)
    w_uk_t = jnp.transpose(ukv[:, :, :MLA_NOPE], (1, 2, 0)).astype(BF16)
    w_uv = jnp.transpose(ukv[:, :, MLA_NOPE:], (1, 0, 2)).astype(BF16)
    return dict(w_main=w_main, w_c=w_c, w_qa=w_qa, w_qb=w_qb, w_kv=w_ukv.astype(BF16),
                w_uk_t=w_uk_t, w_uv=w_uv, wdr=w_down_ret.astype(BF16), wdm=w_down_mla.astype(BF16),
                wo=w_out.astype(BF16), wpg=w_ple_gate.astype(BF16), wpp=w_ple_proj.astype(BF16))


def _row_tile(m, cap):
    t = min(m, cap)
    assert m % t == 0
    return t


def _project(x, ln_w, lw, cos_r, sin_r):
    tm = _row_tile(cos_r.shape[0], 1024)
    assert x.shape[0] % tm == 0
    qk = _inproj(x, ln_w, lw["w_main"], cos_r, sin_r, col0=0, n=2 * RET_QK, tm=tm, tn=512, n_q=RET_QK // 512,
                 k_scale=RET_DK ** -0.5)
    slab = _inproj(x, ln_w, lw["w_main"], col0=2 * RET_QK, n=_S_W, tm=tm, tn=1024)
    return qk, slab


def kernel(x_prompt, x_sample, cache_ckv, cache_krope, state_ret, page_table, p_prompt, p_sample, ln_w, w_in,
           q_norm_w, w_uq, kv_norm_w, w_ukv, ret_gn_w, w_down_ret, w_down_mla, w_out, w_ple_gate, w_ple_proj,
           final_norm_w):
    depth = w_in.shape[0]
    b, s, d = x_prompt.shape
    bd, t, _ = x_sample.shape
    n_pages = page_table.shape[1]
    page = cache_ckv.shape[2]
    past_len = n_pages * page
    assert s % RET_CHUNK == 0 and n_pages % PAGES_PER_STEP == 0

    pos_p = np.arange(s)
    pos_s = np.tile(past_len + np.arange(t), bd)
    cos_rp, sin_rp = _rope_tables(pos_p, RET_DK // 2)
    cos_rs, sin_rs = _rope_tables(pos_s, RET_DK // 2)
    cos_mp, sin_mp = _mla_rope_tables(pos_p)
    cos_ms, sin_ms = _mla_rope_tables(pos_s)
    dec_p = _decay_tables(RET_CHUNK, RET_CHUNK)
    dec_s = _decay_tables(t, RET_CHUNK)

    y_p = x_prompt.reshape(b * s, d)
    y_s = x_sample.reshape(bd * t, d)
    tq = _row_tile(s, 512)
    tk = tq
    outs = [[] for _ in range(6)]
    for i in range(depth):
        lw = _layer_weights(w_in[i], w_uq[i], w_ukv[i], w_down_ret[i], w_down_mla[i], w_out[i], w_ple_gate[i],
                            w_ple_proj[i])
        lnw = ln_w[i][None, :]
        qnw = q_norm_w[i][None, :]
        kvnw = kv_norm_w[i][None, :]
        gnw = ret_gn_w[i][None, :]
        fw = final_norm_w[None, :]
        final = i == depth - 1

        qk, slab = _project(y_p, lnw, lw, cos_rp, sin_rp)
        q, ckv, kr, krp, kv = _mla_prep(y_p, lnw, lw["w_c"], qnw, lw["w_qa"], lw["w_qb"], kvnw, cos_mp, sin_mp,
                                        lw["w_kv"], tm=_row_tile(s, 512), with_kv=True)
        act_ret, s_fin = _ret_prompt(qk, slab, gnw, *dec_p, batch=b, seq=s)
        act_mla = _mla_prompt(q, kv, krp, slab, batch=b, seq=s, tq=tq, tk=tk)
        y_p = _finish(y_p, act_ret, act_mla, slab, p_prompt[i].reshape(b * s, -1), lw["wdr"], lw["wdm"], lw["wo"],
                      lw["wpg"], lw["wpp"], fw, tm=_row_tile(b * s, 512), final=final)
        outs[0].append(ckv.reshape(b, s, KV_LORA))
        outs[1].append(kr.reshape(b, s, MLA_ROPE))
        outs[2].append(s_fin.astype(x_prompt.dtype))

        ms = bd * t
        qk, slab = _project(y_s, lnw, lw, cos_rs, sin_rs)
        q, ckv, kr = _mla_prep(y_s, lnw, lw["w_c"], qnw, lw["w_qa"], lw["w_qb"], kvnw, cos_ms, sin_ms, None,
                               tm=ms, with_kv=False)
        act_ret, s_new = _ret_sample(qk, slab, gnw, state_ret[i], *dec_s, batch=bd, t=t)
        q_lat = _headproj(q, lw["w_uk_t"], x_stride=MLA_QPAD // LANE)
        q_lat = q_lat.reshape(bd, t, MLA_HEADS, KV_LORA).transpose(0, 2, 1, 3).reshape(bd, MLA_HEADS * t, KV_LORA)
        q_rope = q.reshape(bd, t, MLA_HEADS, MLA_QPAD)[..., MLA_NOPE:MLA_QK]
        q_rope = q_rope.transpose(0, 2, 1, 3).reshape(bd, MLA_HEADS * t, MLA_ROPE)
        cache_rt = jnp.swapaxes(cache_krope[i], 1, 2)
        o_lat = _mla_sample(page_table, q_lat, q_rope, ckv.reshape(bd, t, KV_LORA), kr.reshape(bd, t, MLA_ROPE),
                            cache_ckv[i], cache_rt, t=t)
        o_lat = o_lat.reshape(bd, MLA_HEADS, t, KV_LORA).transpose(0, 2, 1, 3).reshape(ms, MLA_HEADS * KV_LORA)
        act_mla = _headproj(o_lat.astype(BF16), lw["w_uv"], x_stride=1, gate=slab, gate_blk0=_S_MG // MLA_V)
        y_s = _finish(y_s, act_ret, act_mla, slab, p_sample[i].reshape(ms, -1), lw["wdr"], lw["wdm"], lw["wo"],
                      lw["wpg"], lw["wpp"], fw, tm=_row_tile(ms, 512), final=final)
        outs[3].append(ckv.reshape(bd, t, KV_LORA))
        outs[4].append(kr.reshape(bd, t, MLA_ROPE))
        outs[5].append(s_new.astype(state_ret.dtype))

    y_prompt = y_p.reshape(b, s, d)
    y_sample = y_s.reshape(bd, t, d)
    ckv_p, kr_p, ret_p, ckv_s, kr_s, ret_s = [jnp.stack(o) for o in outs]
    return (y_prompt, y_sample, ckv_p, kr_p, ret_p, ckv_s, kr_s, ret_s)
```

```python
import functools
import math

import numpy as np
import jax
import jax.numpy as jnp
from jax import lax
from jax.experimental import pallas as pl
from jax.experimental.pallas import tpu as pltpu

F32 = jnp.float32
BF16 = jnp.bfloat16

D_MODEL = 1024
PLE_DIM = 256
RET_HEADS = 4
RET_DK = 256
RET_DV = 512
RET_CHUNK = 128
MLA_HEADS = 8
MLA_NOPE = 128
MLA_ROPE = 64
MLA_V = 128
Q_LORA = 384
KV_LORA = 256
ROPE_BASE = 10000.0
EPS = 1e-6

RET_QK = RET_HEADS * RET_DK
RET_VW = RET_HEADS * RET_DV
MLA_QK = MLA_NOPE + MLA_ROPE
MLA_VW = MLA_HEADS * MLA_V
MLA_QPAD = 256
MLA_KVW = MLA_NOPE + MLA_V
Q_SCALE = MLA_QK ** -0.5 * math.log2(math.e)
LANE = 128
PAGES_PER_STEP = 32
N_SLOTS = 3
KEY_GROUPS = 2
KEY_SPLITS = 2
HEADS_PER_STEP = 2
WT_TILE = 512
WT_ROWS = 64
RET_SAMPLE_SEQS = 4
VMEM_LIMIT = 48 * 1024 * 1024

_OFF_RQ = 0
_OFF_RK = _OFF_RQ + RET_QK
_OFF_RV = _OFF_RK + RET_QK
_OFF_RG = _OFF_RV + RET_VW
_OFF_CQ = _OFF_RG + RET_VW
_OFF_CKV = _OFF_CQ + Q_LORA
_OFF_KR = _OFF_CKV + KV_LORA
_OFF_MG = _OFF_KR + MLA_ROPE
_OFF_GA = _OFF_MG + MLA_VW
_OFF_GB = _OFF_GA + D_MODEL
_S_RV = 0
_S_RG = _S_RV + RET_VW
_S_MG = _S_RG + RET_VW
_S_GA = _S_MG + MLA_VW
_S_GB = _S_GA + D_MODEL
_S_W = _S_GB + D_MODEL

_NT = (((1,), (1,)), ((), ()))
_TN = (((0,), (0,)), ((), ()))


def _params(sem):
    return pltpu.CompilerParams(dimension_semantics=sem, vmem_limit_bytes=VMEM_LIMIT)


def _rms(x, w):
    return x * lax.rsqrt(jnp.mean(x * x, axis=-1, keepdims=True) + EPS) * w


def _silu(x):
    return x * jax.nn.sigmoid(x)


def _lane_blocks(s):
    return [s[:, c * LANE:(c + 1) * LANE] for c in range(s.shape[1] // LANE)]


def _lane_rowmax(s):
    blocks = _lane_blocks(s)
    mx = blocks[0]
    for blk in blocks[1:]:
        mx = jnp.maximum(mx, blk)
    return jnp.broadcast_to(jnp.max(mx, axis=-1, keepdims=True), mx.shape)


def _lane_exp(s, m):
    ps = [jnp.exp2(blk - m) for blk in _lane_blocks(s)]
    part = ps[0]
    for p in ps[1:]:
        part = part + p
    return jnp.concatenate([p.astype(BF16) for p in ps], axis=1), part


def _inproj_kernel(x_ref, lnw_ref, w_ref, *rest, rope, n_q, k_scale, tn):
    if rope:
        cos_ref, sin_ref, o_ref, xn_ref = rest
    else:
        o_ref, xn_ref = rest
    j = pl.program_id(1)

    @pl.when(j == 0)
    def _():
        xn_ref[...] = _rms(x_ref[...], lnw_ref[...]).astype(BF16)

    acc = jnp.dot(xn_ref[...], w_ref[...], preferred_element_type=F32)
    if not rope:
        o_ref[...] = acc.astype(o_ref.dtype)
        return
    scale = jnp.where(j >= n_q, k_scale, 1.0).astype(F32)
    cos = cos_ref[...] * scale
    sin = sin_ref[...] * scale
    for h in range(tn // RET_DK):
        lo = h * RET_DK
        a1 = acc[:, lo:lo + LANE]
        a2 = acc[:, lo + LANE:lo + RET_DK]
        o_ref[:, lo:lo + LANE] = (a1 * cos - a2 * sin).astype(o_ref.dtype)
        o_ref[:, lo + LANE:lo + RET_DK] = (a2 * cos + a1 * sin).astype(o_ref.dtype)


def _inproj(x, ln_w, w, cos=None, sin=None, *, col0, n, tm, tn, n_q=0, k_scale=1.0):
    m, d = x.shape
    rope = cos is not None
    blk0 = col0 // tn
    in_specs = [
        pl.BlockSpec((tm, d), lambda i, j: (i, 0)),
        pl.BlockSpec((1, d), lambda i, j: (0, 0)),
        pl.BlockSpec((d, tn), lambda i, j: (0, blk0 + j)),
    ]
    args = [x, ln_w, w]
    if rope:
        p_tiles = cos.shape[0] // tm
        in_specs += [pl.BlockSpec((tm, LANE), lambda i, j: (i % p_tiles, 0))] * 2
        args += [cos, sin]
    return pl.pallas_call(
        functools.partial(_inproj_kernel, rope=rope, n_q=n_q, k_scale=k_scale, tn=tn),
        grid=(m // tm, n // tn),
        in_specs=in_specs,
        out_specs=pl.BlockSpec((tm, tn), lambda i, j: (i, j)),
        out_shape=jax.ShapeDtypeStruct((m, n), BF16),
        scratch_shapes=[pltpu.VMEM((tm, d), BF16)],
        compiler_params=_params(("arbitrary", "arbitrary")),
        name="inproj_rope" if rope else "inproj",
    )(*args)


def _wt_kernel(tbl_ref, *refs):
    del tbl_ref
    o_ref = refs[-1]
    rows = jnp.concatenate([r[...] for r in refs[:-1]], axis=0)
    o_ref[...] = rows.T.astype(BF16)


def _transpose_weight(w_t, src_rows):
    d = w_t.shape[1]
    parts = WT_TILE // WT_ROWS
    assert all(r % WT_ROWS == 0 for r in src_rows)
    tbl = np.asarray([r // WT_ROWS for r in src_rows], np.int32)
    grid_spec = pltpu.PrefetchScalarGridSpec(
        num_scalar_prefetch=1,
        grid=(len(src_rows),),
        in_specs=[pl.BlockSpec((WT_ROWS, d), lambda j, tbl, r=r: (tbl[j] + r, 0)) for r in range(parts)],
        out_specs=pl.BlockSpec((d, WT_TILE), lambda j, tbl: (0, j)),
    )
    return pl.pallas_call(
        _wt_kernel,
        grid_spec=grid_spec,
        out_shape=jax.ShapeDtypeStruct((d, len(src_rows) * WT_TILE), BF16),
        compiler_params=_params(("arbitrary",)),
        name="wt",
    )(tbl, *([w_t] * parts))


def _mla_prep_kernel(x_ref, lnw_ref, wc_ref, qnw_ref, wqa_ref, wqb_ref, kvnw_ref, cos_ref, sin_ref, *rest,
                     with_kv):
    if with_kv:
        wkv_ref, q_ref, ckv_ref, kr_ref, krp_ref, kv_ref = rest
    else:
        q_ref, ckv_ref, kr_ref = rest
    xn = _rms(x_ref[...], lnw_ref[...]).astype(BF16)
    zc = jnp.dot(xn, wc_ref[...], preferred_element_type=F32)
    cos = cos_ref[...]
    sin = sin_ref[...]
    cqn = _rms(zc[:, :Q_LORA], qnw_ref[...]).astype(BF16)
    qa = jnp.dot(cqn, wqa_ref[...], preferred_element_type=F32)
    qb = jnp.dot(cqn, wqb_ref[...], preferred_element_type=F32)
    qcos = cos * Q_SCALE
    qsin = sin * Q_SCALE
    for h in range(MLA_HEADS):
        lo = h * MLA_QPAD
        q_ref[:, lo:lo + LANE] = (qa[:, lo:lo + LANE] * Q_SCALE).astype(q_ref.dtype)
        q_ref[:, lo + LANE:lo + MLA_QPAD] = (
            qa[:, lo + LANE:lo + MLA_QPAD] * qcos + qb[:, h * LANE:(h + 1) * LANE] * qsin).astype(q_ref.dtype)
    c0 = Q_LORA
    ckvn = _rms(zc[:, c0:c0 + KV_LORA], kvnw_ref[...])
    ckv_ref[...] = ckvn
    k0 = c0 + KV_LORA
    krp = zc[:, k0:k0 + LANE] * cos + zc[:, k0 + LANE:k0 + 2 * LANE] * sin
    kr_ref[...] = krp[:, :MLA_ROPE]
    if with_kv:
        krp_ref[...] = krp.astype(BF16)
        kv_ref[...] = jnp.dot(ckvn.astype(BF16), wkv_ref[...], preferred_element_type=F32).astype(BF16)


def _mla_prep(x, ln_w, wc, qnw, wqa, wqb, kvnw, cos, sin, wkv, *, tm, with_kv):
    m, d = x.shape
    p_tiles = cos.shape[0] // tm
    row = lambda i: (i, 0)
    fixed = lambda i: (0, 0)
    in_specs = [
        pl.BlockSpec((tm, d), row),
        pl.BlockSpec((1, d), fixed),
        pl.BlockSpec(wc.shape, fixed),
        pl.BlockSpec((1, Q_LORA), fixed),
        pl.BlockSpec(wqa.shape, fixed),
        pl.BlockSpec(wqb.shape, fixed),
        pl.BlockSpec((1, KV_LORA), fixed),
        pl.BlockSpec((tm, LANE), lambda i: (i % p_tiles, 0)),
        pl.BlockSpec((tm, LANE), lambda i: (i % p_tiles, 0)),
    ]
    args = [x, ln_w, wc, qnw, wqa, wqb, kvnw, cos, sin]
    qw = MLA_HEADS * MLA_QPAD
    out_specs = [pl.BlockSpec((tm, qw), row), pl.BlockSpec((tm, KV_LORA), row), pl.BlockSpec((tm, MLA_ROPE), row)]
    out_shape = [jax.ShapeDtypeStruct((m, qw), BF16), jax.ShapeDtypeStruct((m, KV_LORA), F32),
                 jax.ShapeDtypeStruct((m, MLA_ROPE), F32)]
    if with_kv:
        in_specs.append(pl.BlockSpec(wkv.shape, fixed))
        args.append(wkv)
        out_specs += [pl.BlockSpec((tm, LANE), row), pl.BlockSpec((tm, wkv.shape[1]), row)]
        out_shape += [jax.ShapeDtypeStruct((m, LANE), BF16), jax.ShapeDtypeStruct((m, wkv.shape[1]), BF16)]
    return pl.pallas_call(
        functools.partial(_mla_prep_kernel, with_kv=with_kv),
        grid=(m // tm,),
        in_specs=in_specs,
        out_specs=out_specs,
        out_shape=out_shape,
        compiler_params=_params(("arbitrary",)),
        name="mla_prep",
    )(*args)


def _ret_chunk(q, k, v, s_prev, dmask, qdec, kdec, gc):
    sc = lax.dot_general(q, k, _NT, preferred_element_type=F32) * dmask
    intra = jnp.dot(sc.astype(BF16), v, preferred_element_type=F32)
    cross = jnp.dot(q, s_prev.astype(BF16), preferred_element_type=F32) * qdec
    kd = (k.astype(F32) * kdec).astype(BF16)
    s_new = gc * s_prev + lax.dot_general(kd, v, _TN, preferred_element_type=F32)
    return intra + cross, s_new


def _ret_act(o, rg, gnw):
    mu = jnp.mean(o, axis=-1, keepdims=True)
    d = o - mu
    var = jnp.mean(d * d, axis=-1, keepdims=True)
    return d * lax.rsqrt(var + EPS) * gnw * _silu(rg.astype(F32))


def _ret_prompt_kernel(q_ref, k_ref, v_ref, rg_ref, gnw_ref, dm_ref, qd_ref, kd_ref, gc_ref,
                       o_ref, sfin_ref, s_ref, *, n_chunks, nh):
    s_ref[...] = jnp.zeros_like(s_ref)

    def chunk(c, carry):
        rows = pl.ds(pl.multiple_of(c * RET_CHUNK, RET_CHUNK), RET_CHUNK)
        for h in range(nh):
            qk_cols = slice(h * RET_DK, (h + 1) * RET_DK)
            v_cols = slice(h * RET_DV, (h + 1) * RET_DV)
            o, s_new = _ret_chunk(q_ref[rows, qk_cols], k_ref[rows, qk_cols], v_ref[rows, v_cols], s_ref[h],
                                  dm_ref[h], qd_ref[h], kd_ref[h], gc_ref[h])
            s_ref[h] = s_new
            o_ref[rows, v_cols] = _ret_act(o, rg_ref[rows, v_cols], gnw_ref[:, v_cols]).astype(o_ref.dtype)
        return carry

    lax.fori_loop(0, n_chunks, chunk, 0, unroll=4)
    sfin_ref[0] = s_ref[...]


def _ret_prompt(qk, slab, gn_w, dmask, qdec, kdec, gcs, *, batch, seq):
    nh = HEADS_PER_STEP
    qw, vw = nh * RET_DK, nh * RET_DV
    k_blk = RET_QK // qw
    rg_blk = _S_RG // vw
    return pl.pallas_call(
        functools.partial(_ret_prompt_kernel, n_chunks=seq // RET_CHUNK, nh=nh),
        grid=(batch, RET_HEADS // nh),
        in_specs=[
            pl.BlockSpec((seq, qw), lambda b, hp: (b, hp)),
            pl.BlockSpec((seq, qw), lambda b, hp: (b, k_blk + hp)),
            pl.BlockSpec((seq, vw), lambda b, hp: (b, hp)),
            pl.BlockSpec((seq, vw), lambda b, hp: (b, rg_blk + hp)),
            pl.BlockSpec((1, vw), lambda b, hp: (0, hp)),
            pl.BlockSpec((nh, RET_CHUNK, RET_CHUNK), lambda b, hp: (hp, 0, 0)),
            pl.BlockSpec((nh, RET_CHUNK, RET_DV), lambda b, hp: (hp, 0, 0)),
            pl.BlockSpec((nh, RET_CHUNK, RET_DK), lambda b, hp: (hp, 0, 0)),
            pl.BlockSpec((nh, 1, RET_DV), lambda b, hp: (hp, 0, 0)),
        ],
        out_specs=[
            pl.BlockSpec((seq, vw), lambda b, hp: (b, hp)),
            pl.BlockSpec((1, nh, RET_DK, RET_DV), lambda b, hp: (b, hp, 0, 0)),
        ],
        out_shape=[
            jax.ShapeDtypeStruct((batch * seq, RET_VW), BF16),
            jax.ShapeDtypeStruct((batch, RET_HEADS, RET_DK, RET_DV), F32),
        ],
        scratch_shapes=[pltpu.VMEM((nh, RET_DK, RET_DV), F32)],
        compiler_params=_params(("arbitrary", "arbitrary")),
        name="ret_prompt",
    )(qk, qk, slab, slab, gn_w, dmask, qdec, kdec, gcs)


def _ret_sample_kernel(qk_ref, vg_ref, gnw_ref, st_ref, dm_ref, qd_ref, kd_ref, gc_ref,
                       o_ref, snew_ref, qp_ref, kp_ref, vp_ref, act_ref, *, t, nb):
    qp_ref[...] = jnp.zeros_like(qp_ref)
    kp_ref[...] = jnp.zeros_like(kp_ref)
    vp_ref[...] = jnp.zeros_like(vp_ref)
    qk = qk_ref[...].astype(F32)
    vg = vg_ref[...].astype(F32)
    for bb in range(nb):
        rows = slice(bb * t, (bb + 1) * t)
        for h in range(RET_HEADS):
            qp_ref[0:t, :] = qk[rows, h * RET_DK:(h + 1) * RET_DK]
            kp_ref[0:t, :] = qk[rows, RET_QK + h * RET_DK:RET_QK + (h + 1) * RET_DK]
            vp_ref[0:t, :] = vg[rows, _S_RV + h * RET_DV:_S_RV + (h + 1) * RET_DV]
            o, s_new = _ret_chunk(qp_ref[...].astype(BF16), kp_ref[...].astype(BF16), vp_ref[...].astype(BF16),
                                  st_ref[bb, h], dm_ref[h], qd_ref[h], kd_ref[h], gc_ref[h])
            snew_ref[bb, h] = s_new
            act_ref[rows, h * RET_DV:(h + 1) * RET_DV] = _ret_act(
                o[0:t, :], vg[rows, _S_RG + h * RET_DV:_S_RG + (h + 1) * RET_DV],
                gnw_ref[:, h * RET_DV:(h + 1) * RET_DV])
    o_ref[...] = act_ref[...].astype(o_ref.dtype)


def _ret_sample(qk, slab, gn_w, state, dmask, qdec, kdec, gcs, *, batch, t):
    h = RET_HEADS
    whole = lambda b: (0, 0, 0)
    nb = _row_tile(batch, RET_SAMPLE_SEQS)
    return pl.pallas_call(
        functools.partial(_ret_sample_kernel, t=t, nb=nb),
        grid=(batch // nb,),
        in_specs=[
            pl.BlockSpec((nb * t, qk.shape[1]), lambda b: (b, 0)),
            pl.BlockSpec((nb * t, _S_MG), lambda b: (b, 0)),
            pl.BlockSpec((1, RET_VW), lambda b: (0, 0)),
            pl.BlockSpec((nb, h, RET_DK, RET_DV), lambda b: (b, 0, 0, 0)),
            pl.BlockSpec(dmask.shape, whole),
            pl.BlockSpec(qdec.shape, whole),
            pl.BlockSpec(kdec.shape, whole),
            pl.BlockSpec(gcs.shape, whole),
        ],
        out_specs=[
            pl.BlockSpec((nb * t, RET_VW), lambda b: (b, 0)),
            pl.BlockSpec((nb, h, RET_DK, RET_DV), lambda b: (b, 0, 0, 0)),
        ],
        out_shape=[
            jax.ShapeDtypeStruct((batch * t, RET_VW), BF16),
            jax.ShapeDtypeStruct(state.shape, state.dtype),
        ],
        scratch_shapes=[pltpu.VMEM((RET_CHUNK, RET_DK), F32), pltpu.VMEM((RET_CHUNK, RET_DK), F32),
                        pltpu.VMEM((RET_CHUNK, RET_DV), F32), pltpu.VMEM((nb * t, RET_VW), F32)],
        compiler_params=_params(("arbitrary",)),
        name="ret_sample",
    )(qk, slab, gn_w, state, dmask, qdec, kdec, gcs)


def _mla_prompt_kernel(q_ref, kv_ref, krp_ref, mg_ref, o_ref, m_ref, l_ref, acc_ref, *, tq, tk, nh):
    qi = pl.program_id(2)
    ratio = tq // tk

    def scores(kb, h):
        rows = pl.ds(pl.multiple_of(kb * tk, tk), tk)
        k = jnp.concatenate([kv_ref[rows, h * MLA_KVW:h * MLA_KVW + MLA_NOPE], krp_ref[rows, :]], axis=1)
        s = lax.dot_general(q_ref[:, h * MLA_QPAD:(h + 1) * MLA_QPAD], k, _NT, preferred_element_type=F32)
        return s, kv_ref[rows, h * MLA_KVW + MLA_NOPE:(h + 1) * MLA_KVW]

    def update(h, s, v, first=False):
        m_cur = _lane_rowmax(s)
        if first:
            m_new = m_cur
        else:
            m_old = m_ref[h]
            m_new = jnp.maximum(m_old, m_cur)
            alpha = jnp.exp2(m_old - m_new)
        p, l_part = _lane_exp(s, m_new)
        pv = jnp.dot(p, v, preferred_element_type=F32)
        m_ref[h] = m_new
        l_ref[h] = l_part if first else alpha * l_ref[h] + l_part
        acc_ref[h] = pv if first else alpha * acc_ref[h] + pv

    r_id = lax.broadcasted_iota(jnp.int32, (tq, tk), 0)
    c_id = lax.broadcasted_iota(jnp.int32, (tq, tk), 1)
    for d in range(ratio):
        for h in range(nh):
            s, v = scores(qi * ratio + d, h)
            update(h, jnp.where(c_id + d * tk <= r_id, s, -jnp.inf), v, first=d == 0)

    def body(kb, carry):
        for h in range(nh):
            s, v = scores(kb, h)
            update(h, s, v)
        return carry

    lax.fori_loop(0, qi * ratio, body, 0)
    for h in range(nh):
        cols = slice(h * MLA_V, (h + 1) * MLA_V)
        l = jnp.sum(l_ref[h], axis=-1, keepdims=True)
        o_ref[:, cols] = (acc_ref[h] / l * _silu(mg_ref[:, cols].astype(F32))).astype(o_ref.dtype)


def _mla_prompt(q, kv, krp, slab, *, batch, seq, tq, tk):
    nq = seq // tq
    nh = MLA_HEADS
    mg_blk = _S_MG // (nh * MLA_V)
    return pl.pallas_call(
        functools.partial(_mla_prompt_kernel, tq=tq, tk=tk, nh=nh),
        grid=(batch, MLA_HEADS // nh, nq),
        in_specs=[
            pl.BlockSpec((tq, nh * MLA_QPAD), lambda b, hp, i: (b * nq + i, hp)),
            pl.BlockSpec((seq, nh * MLA_KVW), lambda b, hp, i: (b, hp)),
            pl.BlockSpec((seq, LANE), lambda b, hp, i: (b, 0)),
            pl.BlockSpec((tq, nh * MLA_V), lambda b, hp, i: (b * nq + i, mg_blk + hp)),
        ],
        out_specs=pl.BlockSpec((tq, nh * MLA_V), lambda b, hp, i: (b * nq + i, hp)),
        out_shape=jax.ShapeDtypeStruct((batch * seq, MLA_VW), BF16),
        scratch_shapes=[pltpu.VMEM((nh, tq, LANE), F32), pltpu.VMEM((nh, tq, LANE), F32),
                        pltpu.VMEM((nh, tq, MLA_V), F32)],
        compiler_params=_params(("arbitrary", "arbitrary", "arbitrary")),
        name="mla_prompt",
    )(q, kv, krp, slab)


def _headproj_kernel(x_ref, w_ref, *rest, gated):
    if gated:
        g_ref, o_ref = rest
    else:
        (o_ref,) = rest
    y = jnp.dot(x_ref[...], w_ref[0], preferred_element_type=F32)
    if gated:
        y = y * _silu(g_ref[...].astype(F32))
    o_ref[...] = y.astype(o_ref.dtype)


def _headproj(x, w, *, x_stride, gate=None, gate_blk0=0):
    m = x.shape[0]
    nh, kdim, n = w.shape
    in_specs = [pl.BlockSpec((m, kdim), lambda h: (0, h * x_stride)),
                pl.BlockSpec((1, kdim, n), lambda h: (h, 0, 0))]
    args = [x, w]
    if gate is not None:
        in_specs.append(pl.BlockSpec((m, n), lambda h: (0, gate_blk0 + h)))
        args.append(gate)
    return pl.pallas_call(
        functools.partial(_headproj_kernel, gated=gate is not None),
        grid=(nh,),
        in_specs=in_specs,
        out_specs=pl.BlockSpec((m, n), lambda h: (0, h)),
        out_shape=jax.ShapeDtypeStruct((m, nh * n), BF16),
        compiler_params=_params(("arbitrary",)),
        name="headproj",
    )(*args)


def _page_copies(pt_ref, cache_c, cache_rt, cbuf, rbuf, sem, idx, slot, *, g, n_steps, page):
    b = idx // n_steps
    first = (idx % n_steps) * g
    out = []
    for j in range(g):
        p = pt_ref[b, first + j]
        out.append(pltpu.make_async_copy(cache_c.at[p], cbuf.at[slot, pl.ds(j * page, page), :], sem.at[slot, 0]))
        out.append(pltpu.make_async_copy(cache_rt.at[p], rbuf.at[slot, j], sem.at[slot, 1]))
    return out


def _mla_sample_kernel(pt_ref, ql_ref, qr_ref, cn_ref, rn_ref, cache_c, cache_rt, o_ref,
                       cbuf, rbuf, sem, m_ref, l_ref, acc_ref, *, t, page, g, n_steps):
    i = pl.program_id(0)
    last = pl.num_programs(0) - 1
    step = i % n_steps
    slot = i % N_SLOTS
    copies = functools.partial(_page_copies, pt_ref, cache_c, cache_rt, cbuf, rbuf, sem,
                               g=g, n_steps=n_steps, page=page)

    @pl.when(i == 0)
    def _():
        for ahead in range(N_SLOTS - 1):
            for c in copies(jnp.minimum(ahead, last), ahead):
                c.start()

    ql = ql_ref[0]
    qr = qr_ref[0]
    rows = ql.shape[0]

    @pl.when(step == 0)
    def _():
        qlf = ql.astype(F32)
        qrf = qr.astype(F32)
        tok = lax.broadcasted_iota(jnp.int32, (rows, 1), 0) // (rows // t)
        cn = cn_ref[0].astype(BF16).astype(F32)
        rn = rn_ref[0].astype(BF16).astype(F32)
        s_cols = []
        for j in range(t):
            sj = (jnp.sum(qlf * cn[j:j + 1, :], axis=-1, keepdims=True)
                  + jnp.sum(qrf * rn[j:j + 1, :], axis=-1, keepdims=True))
            s_cols.append(jnp.where(tok >= j, sj, -jnp.inf))
        m0 = s_cols[0]
        for j in range(1, t):
            m0 = jnp.maximum(m0, s_cols[j])
        l0 = jnp.zeros((rows, 1), F32)
        a0 = jnp.zeros((rows, KV_LORA), F32)
        for j in range(t):
            pj = jnp.exp2(s_cols[j] - m0)
            l0 = l0 + pj
            a0 = a0 + pj.astype(BF16).astype(F32) * cn[j:j + 1, :]
        lane = lax.broadcasted_iota(jnp.int32, (rows, LANE), 1)
        m_ref[...] = jnp.broadcast_to(m0, (rows, LANE))
        l_ref[...] = jnp.where(lane == 0, l0, 0.0)
        acc_ref[...] = a0

    for c in copies(i, slot):
        c.wait()
    ppk = g // (KEY_GROUPS * KEY_SPLITS)
    sub = ppk * page
    m_run, l_run, acc_run = m_ref[...], l_ref[...], acc_ref[...]
    for grp in range(KEY_GROUPS):
        cbs, parts = [], []
        for k in range(grp * KEY_SPLITS, (grp + 1) * KEY_SPLITS):
            cb = cbuf[slot, k * sub:(k + 1) * sub, :].astype(BF16)
            rb = jnp.concatenate([rbuf[slot, k * ppk + j] for j in range(ppk)], axis=1).astype(BF16)
            cbs.append(cb)
            parts.append(lax.dot_general(ql, cb, _NT, preferred_element_type=F32)
                         + jnp.dot(qr, rb, preferred_element_type=F32))
        s = jnp.concatenate(parts, axis=1)
        m_new = jnp.maximum(m_run, _lane_rowmax(s))
        alpha = jnp.exp2(m_run - m_new)
        pb, l_part = _lane_exp(s, m_new)
        pv = [jnp.dot(pb[:, k * sub:(k + 1) * sub], cbs[k], preferred_element_type=F32) for k in range(KEY_SPLITS)]
        while len(pv) > 1:
            pv = [pv[j] + pv[j + 1] for j in range(0, len(pv), 2)]
        m_run = m_new
        l_run = alpha * l_run + l_part
        acc_run = jnp.concatenate([alpha] * (KV_LORA // LANE), axis=1) * acc_run + pv[0]
    m_ref[...] = m_run
    l_ref[...] = l_run
    acc_ref[...] = acc_run

    ahead = N_SLOTS - 1
    for c in copies(jnp.minimum(i + ahead, last), (i + ahead) % N_SLOTS):
        c.start()

    @pl.when(step == n_steps - 1)
    def _():
        o_ref[0] = (acc_ref[...] / jnp.sum(l_ref[...], axis=-1, keepdims=True)).astype(o_ref.dtype)

    @pl.when(i == last)
    def _():
        for extra in range(1, N_SLOTS):
            for c in copies(last, (last + extra) % N_SLOTS):
                c.wait()


def _mla_sample(page_table, ql, qr, c_new, r_new, cache_c, cache_rt, *, t):
    batch, n_pages = page_table.shape
    page = cache_c.shape[1]
    g = PAGES_PER_STEP
    n_steps = n_pages // g
    rows = ql.shape[1]
    per_b = lambda i, pt: (i // n_steps, 0, 0)
    grid_spec = pltpu.PrefetchScalarGridSpec(
        num_scalar_prefetch=1,
        grid=(batch * n_steps,),
        in_specs=[
            pl.BlockSpec((1, rows, KV_LORA), per_b),
            pl.BlockSpec((1, rows, MLA_ROPE), per_b),
            pl.BlockSpec((1, t, KV_LORA), per_b),
            pl.BlockSpec((1, t, MLA_ROPE), per_b),
            pl.BlockSpec(memory_space=pl.ANY),
            pl.BlockSpec(memory_space=pl.ANY),
        ],
        out_specs=pl.BlockSpec((1, rows, KV_LORA), per_b),
        scratch_shapes=[
            pltpu.VMEM((N_SLOTS, g * page, KV_LORA), F32),
            pltpu.VMEM((N_SLOTS, g, MLA_ROPE, page), F32),
            pltpu.SemaphoreType.DMA((N_SLOTS, 2)),
            pltpu.VMEM((rows, LANE), F32), pltpu.VMEM((rows, LANE), F32), pltpu.VMEM((rows, KV_LORA), F32),
        ],
    )
    return pl.pallas_call(
        functools.partial(_mla_sample_kernel, t=t, page=page, g=g, n_steps=n_steps),
        grid_spec=grid_spec,
        out_shape=jax.ShapeDtypeStruct((batch, rows, KV_LORA), BF16),
        compiler_params=_params(("arbitrary",)),
        name="mla_sample",
    )(page_table, ql, qr, c_new, r_new, cache_c, cache_rt)


def _finish_kernel(x_ref, ar_ref, am_ref, ga_ref, gb_ref, ple_ref, wdr_ref, wdm_ref, wo_ref, wpg_ref, wpp_ref,
                   fw_ref, o_ref, *, final):
    h_ret = jnp.dot(ar_ref[...], wdr_ref[...], preferred_element_type=F32)
    h_mla = jnp.dot(am_ref[...], wdm_ref[...], preferred_element_type=F32)
    merged = (jax.nn.sigmoid(ga_ref[...].astype(F32)) * h_ret
              + jax.nn.sigmoid(gb_ref[...].astype(F32)) * h_mla)
    x2 = x_ref[...] + jnp.dot(merged.astype(BF16), wo_ref[...], preferred_element_type=F32)
    gate = jax.nn.sigmoid(jnp.dot(x2.astype(BF16), wpg_ref[...], preferred_element_type=F32))
    y = x2 + gate * jnp.dot(ple_ref[...].astype(BF16), wpp_ref[...], preferred_element_type=F32)
    if final:
        y = _rms(y, fw_ref[...])
    o_ref[...] = y


def _finish(x, act_ret, act_mla, slab, ple, wdr, wdm, wo, wpg, wpp, fw, *, tm, final):
    m, d = x.shape
    row = lambda i: (i, 0)
    fixed = lambda i: (0, 0)
    once = pl.Buffered(1)
    return pl.pallas_call(
        functools.partial(_finish_kernel, final=final),
        grid=(m // tm,),
        in_specs=[
            pl.BlockSpec((tm, d), row),
            pl.BlockSpec((tm, RET_VW), row),
            pl.BlockSpec((tm, MLA_VW), row),
            pl.BlockSpec((tm, d), lambda i: (i, _S_GA // D_MODEL)),
            pl.BlockSpec((tm, d), lambda i: (i, _S_GB // D_MODEL)),
            pl.BlockSpec((tm, PLE_DIM), row),
            pl.BlockSpec(wdr.shape, fixed, pipeline_mode=once),
            pl.BlockSpec(wdm.shape, fixed, pipeline_mode=once),
            pl.BlockSpec(wo.shape, fixed, pipeline_mode=once),
            pl.BlockSpec(wpg.shape, fixed, pipeline_mode=once),
            pl.BlockSpec(wpp.shape, fixed, pipeline_mode=once),
            pl.BlockSpec((1, d), fixed),
        ],
        out_specs=pl.BlockSpec((tm, d), row),
        out_shape=jax.ShapeDtypeStruct((m, d), F32),
        compiler_params=_params(("arbitrary",)),
        name="finish",
    )(x, act_ret, act_mla, slab, slab, ple, wdr, wdm, wo, wpg, wpp, fw)


def _rope_tables(pos, half):
    inv_freq = ROPE_BASE ** (-np.arange(half, dtype=np.float64) / half)
    ang = np.asarray(pos, np.float64)[:, None] * inv_freq[None, :]
    return np.cos(ang).astype(np.float32), np.sin(ang).astype(np.float32)


def _mla_rope_tables(pos):
    cos, sin = _rope_tables(pos, MLA_ROPE // 2)
    z = np.zeros((len(pos), LANE - MLA_ROPE), np.float32)
    return np.concatenate([cos, cos, z], axis=1), np.concatenate([sin, sin, z], axis=1)


def _decay_tables(c, rows):
    log_g = np.log1p(-np.exp2(-5.0 - np.arange(RET_HEADS, dtype=np.float64)))
    idx = np.arange(c, dtype=np.float64)
    diff = idx[:, None] - idx[None, :]
    dmask = np.where(diff >= 0, np.exp(np.maximum(diff, 0.0)[None] * log_g[:, None, None]), 0.0)
    qdec = np.exp((idx[None, :] + 1.0) * log_g[:, None])[:, :, None]
    kdec = np.exp((c - 1.0 - idx)[None, :] * log_g[:, None])[:, :, None]
    pad = rows - c
    dmask = np.pad(dmask, ((0, 0), (0, pad), (0, pad)))
    qdec = np.broadcast_to(np.pad(qdec, ((0, 0), (0, pad), (0, 0))), (RET_HEADS, rows, RET_DV))
    kdec = np.broadcast_to(np.pad(kdec, ((0, 0), (0, pad), (0, 0))), (RET_HEADS, rows, RET_DK))
    gcs = np.broadcast_to(np.exp(c * log_g)[:, None, None], (RET_HEADS, 1, RET_DV))
    return tuple(np.ascontiguousarray(a, np.float32) for a in (dmask, qdec, kdec, gcs))


def _rot_cols(w):
    half = w.shape[-1] // 2
    return jnp.concatenate([-w[..., half:], w[..., :half]], axis=-1)


def _layer_weights(w_in, w_uq, w_ukv, w_down_ret, w_down_mla, w_out, w_ple_gate, w_ple_proj):
    d = w_in.shape[0]
    src = [r for lo, hi in ((_OFF_RQ, _OFF_CQ), (_OFF_MG, w_in.shape[1])) for r in range(lo, hi, WT_TILE)]
    n_main = len(src) * WT_TILE
    src += list(range(_OFF_CQ, _OFF_MG, WT_TILE))
    w_main = _transpose_weight(w_in.T, src)
    c_cols = w_main[:, n_main:n_main + _OFF_MG - _OFF_CQ]
    kr = c_cols[:, _OFF_KR - _OFF_CQ:]
    zpad = jnp.zeros((d, LANE - MLA_ROPE), BF16)
    w_c = jnp.concatenate([c_cols, zpad, _rot_cols(kr), zpad], axis=1)
    uq = w_uq.reshape(Q_LORA, MLA_HEADS, MLA_QK)
    uq_rope = uq[:, :, MLA_NOPE:]
    zq = jnp.zeros((Q_LORA, MLA_HEADS, MLA_QPAD - MLA_QK), F32)
    w_qa = jnp.concatenate([uq, zq], axis=2).reshape(Q_LORA, MLA_HEADS * MLA_QPAD).astype(BF16)
    w_qb = jnp.concatenate([_rot_cols(uq_rope), zq], axis=2).reshape(Q_LORA, MLA_HEADS * LANE).astype(BF16)
    ukv = w_ukv.reshape(KV_LORA, MLA_HEADS, MLA_KVW)
    w_uk_t = jnp.transpose(ukv[:, :, :MLA_NOPE], (1, 2, 0)).astype(BF16)
    w_uv = jnp.transpose(ukv[:, :, MLA_NOPE:], (1, 0, 2)).astype(BF16)
    return dict(w_main=w_main, w_c=w_c, w_qa=w_qa, w_qb=w_qb, w_kv=w_ukv.astype(BF16),
                w_uk_t=w_uk_t, w_uv=w_uv, wdr=w_down_ret.astype(BF16), wdm=w_down_mla.astype(BF16),
                wo=w_out.astype(BF16), wpg=w_ple_gate.astype(BF16), wpp=w_ple_proj.astype(BF16))


def _row_tile(m, cap):
    t = min(m, cap)
    assert m % t == 0
    return t


def _project(x, ln_w, lw, cos_r, sin_r):
    tm = _row_tile(cos_r.shape[0], 1024)
    assert x.shape[0] % tm == 0
    qk = _inproj(x, ln_w, lw["w_main"], cos_r, sin_r, col0=0, n=2 * RET_QK, tm=tm, tn=1024, n_q=RET_QK // 1024,
                 k_scale=RET_DK ** -0.5)
    slab = _inproj(x, ln_w, lw["w_main"], col0=2 * RET_QK, n=_S_W, tm=tm, tn=1024)
    return qk, slab


def kernel(x_prompt, x_sample, cache_ckv, cache_krope, state_ret, page_table, p_prompt, p_sample, ln_w, w_in,
           q_norm_w, w_uq, kv_norm_w, w_ukv, ret_gn_w, w_down_ret, w_down_mla, w_out, w_ple_gate, w_ple_proj,
           final_norm_w):
    depth = w_in.shape[0]
    b, s, d = x_prompt.shape
    bd, t, _ = x_sample.shape
    n_pages = page_table.shape[1]
    page = cache_ckv.shape[2]
    past_len = n_pages * page
    assert s % RET_CHUNK == 0 and n_pages % PAGES_PER_STEP == 0

    pos_p = np.arange(s)
    pos_s = np.tile(past_len + np.arange(t), bd)
    cos_rp, sin_rp = _rope_tables(pos_p, RET_DK // 2)
    cos_rs, sin_rs = _rope_tables(pos_s, RET_DK // 2)
    cos_mp, sin_mp = _mla_rope_tables(pos_p)
    cos_ms, sin_ms = _mla_rope_tables(pos_s)
    dec_p = _decay_tables(RET_CHUNK, RET_CHUNK)
    dec_s = _decay_tables(t, RET_CHUNK)

    y_p = x_prompt.reshape(b * s, d)
    y_s = x_sample.reshape(bd * t, d)
    tq = _row_tile(s, 512)
    tk = tq
    outs = [[] for _ in range(6)]
    for i in range(depth):
        lw = _layer_weights(w_in[i], w_uq[i], w_ukv[i], w_down_ret[i], w_down_mla[i], w_out[i], w_ple_gate[i],
                            w_ple_proj[i])
        lnw = ln_w[i][None, :]
        qnw = q_norm_w[i][None, :]
        kvnw = kv_norm_w[i][None, :]
        gnw = ret_gn_w[i][None, :]
        fw = final_norm_w[None, :]
        final = i == depth - 1

        qk, slab = _project(y_p, lnw, lw, cos_rp, sin_rp)
        q, ckv, kr, krp, kv = _mla_prep(y_p, lnw, lw["w_c"], qnw, lw["w_qa"], lw["w_qb"], kvnw, cos_mp, sin_mp,
                                        lw["w_kv"], tm=_row_tile(s, 512), with_kv=True)
        act_ret, s_fin = _ret_prompt(qk, slab, gnw, *dec_p, batch=b, seq=s)
        act_mla = _mla_prompt(q, kv, krp, slab, batch=b, seq=s, tq=tq, tk=tk)
        y_p = _finish(y_p, act_ret, act_mla, slab, p_prompt[i].reshape(b * s, -1), lw["wdr"], lw["wdm"], lw["wo"],
                      lw["wpg"], lw["wpp"], fw, tm=_row_tile(b * s, 512), final=final)
        outs[0].append(ckv.reshape(b, s, KV_LORA))
        outs[1].append(kr.reshape(b, s, MLA_ROPE))
        outs[2].append(s_fin.astype(x_prompt.dtype))

        ms = bd * t
        qk, slab = _project(y_s, lnw, lw, cos_rs, sin_rs)
        q, ckv, kr = _mla_prep(y_s, lnw, lw["w_c"], qnw, lw["w_qa"], lw["w_qb"], kvnw, cos_ms, sin_ms, None,
                               tm=ms, with_kv=False)
        act_ret, s_new = _ret_sample(qk, slab, gnw, state_ret[i], *dec_s, batch=bd, t=t)
        q_lat = _headproj(q, lw["w_uk_t"], x_stride=MLA_QPAD // LANE)
        q_lat = q_lat.reshape(bd, t * MLA_HEADS, KV_LORA)
        q_rope = q.reshape(bd, t * MLA_HEADS, MLA_QPAD)[..., MLA_NOPE:MLA_QK]
        cache_rt = jnp.swapaxes(cache_krope[i], 1, 2)
        o_lat = _mla_sample(page_table, q_lat, q_rope, ckv.reshape(bd, t, KV_LORA), kr.reshape(bd, t, MLA_ROPE),
                            cache_ckv[i], cache_rt, t=t)
        o_lat = o_lat.reshape(ms, MLA_HEADS * KV_LORA)
        act_mla = _headproj(o_lat, lw["w_uv"], x_stride=1, gate=slab, gate_blk0=_S_MG // MLA_V)
        y_s = _finish(y_s, act_ret, act_mla, slab, p_sample[i].reshape(ms, -1), lw["wdr"], lw["wdm"], lw["wo"],
                      lw["wpg"], lw["wpp"], fw, tm=_row_tile(ms, 512), final=final)
        outs[3].append(ckv.reshape(bd, t, KV_LORA))
        outs[4].append(kr.reshape(bd, t, MLA_ROPE))
        outs[5].append(s_new.astype(state_ret.dtype))

    y_prompt = y_p.reshape(b, s, d)
    y_sample = y_s.reshape(bd, t, d)
    ckv_p, kr_p, ret_p, ckv_s, kr_s, ret_s = [jnp.stack(o) for o in outs]
    return (y_prompt, y_sample, ckv_p, kr_p, ret_p, ckv_s, kr_s, ret_s)
```

```python
import functools
import math

import numpy as np
import jax
import jax.numpy as jnp
from jax import lax
from jax.experimental import pallas as pl
from jax.experimental.pallas import tpu as pltpu

F32 = jnp.float32
BF16 = jnp.bfloat16

D_MODEL = 1024
PLE_DIM = 256
RET_HEADS = 4
RET_DK = 256
RET_DV = 512
RET_CHUNK = 128
MLA_HEADS = 8
MLA_NOPE = 128
MLA_ROPE = 64
MLA_V = 128
Q_LORA = 384
KV_LORA = 256
ROPE_BASE = 10000.0
EPS = 1e-6

RET_QK = RET_HEADS * RET_DK
RET_VW = RET_HEADS * RET_DV
MLA_QK = MLA_NOPE + MLA_ROPE
MLA_VW = MLA_HEADS * MLA_V
MLA_QPAD = 256
MLA_KVW = MLA_NOPE + MLA_V
Q_SCALE = MLA_QK ** -0.5 * math.log2(math.e)
LANE = 128
PAGES_PER_STEP = 32
N_SLOTS = 3
KEY_GROUPS = 2
KEY_SPLITS = 2
HEADS_PER_STEP = 2
SLAB_TN = 1792
WT_TILE = 512
WT_ROWS = 64
RET_SAMPLE_SEQS = 4
VMEM_LIMIT = 48 * 1024 * 1024

_OFF_RQ = 0
_OFF_RK = _OFF_RQ + RET_QK
_OFF_RV = _OFF_RK + RET_QK
_OFF_RG = _OFF_RV + RET_VW
_OFF_CQ = _OFF_RG + RET_VW
_OFF_CKV = _OFF_CQ + Q_LORA
_OFF_KR = _OFF_CKV + KV_LORA
_OFF_MG = _OFF_KR + MLA_ROPE
_OFF_GA = _OFF_MG + MLA_VW
_OFF_GB = _OFF_GA + D_MODEL
_S_RV = 0
_S_RG = _S_RV + RET_VW
_S_MG = _S_RG + RET_VW
_S_GA = _S_MG + MLA_VW
_S_GB = _S_GA + D_MODEL
_S_W = _S_GB + D_MODEL

_NT = (((1,), (1,)), ((), ()))
_TN = (((0,), (0,)), ((), ()))


def _params(sem):
    return pltpu.CompilerParams(dimension_semantics=sem, vmem_limit_bytes=VMEM_LIMIT)


def _rms(x, w):
    return x * lax.rsqrt(jnp.mean(x * x, axis=-1, keepdims=True) + EPS) * w


def _silu(x):
    return x * jax.nn.sigmoid(x)


def _lane_blocks(s):
    return [s[:, c * LANE:(c + 1) * LANE] for c in range(s.shape[1] // LANE)]


def _lane_rowmax(s):
    blocks = _lane_blocks(s)
    mx = blocks[0]
    for blk in blocks[1:]:
        mx = jnp.maximum(mx, blk)
    return jnp.broadcast_to(jnp.max(mx, axis=-1, keepdims=True), mx.shape)


def _lane_exp(s, m):
    ps = [jnp.exp2(blk - m) for blk in _lane_blocks(s)]
    part = ps[0]
    for p in ps[1:]:
        part = part + p
    return jnp.concatenate([p.astype(BF16) for p in ps], axis=1), part


def _matmul_kernel(x_ref, w_ref, o_ref):
    o_ref[...] = jnp.dot(x_ref[...], w_ref[...], preferred_element_type=F32).astype(o_ref.dtype)


def _inproj_rope_kernel(x_ref, lnw_ref, w_ref, cos_ref, sin_ref, o_ref, xn_ref, *, n_q, k_scale, tn):
    j = pl.program_id(1)

    @pl.when(j == 0)
    def _():
        xn_ref[...] = _rms(x_ref[...], lnw_ref[...]).astype(BF16)

    acc = jnp.dot(xn_ref[...], w_ref[...], preferred_element_type=F32)
    scale = jnp.where(j >= n_q, k_scale, 1.0).astype(F32)
    cos = cos_ref[...] * scale
    sin = sin_ref[...] * scale
    for h in range(tn // RET_DK):
        lo = h * RET_DK
        a1 = acc[:, lo:lo + LANE]
        a2 = acc[:, lo + LANE:lo + RET_DK]
        o_ref[:, lo:lo + LANE] = (a1 * cos - a2 * sin).astype(o_ref.dtype)
        o_ref[:, lo + LANE:lo + RET_DK] = (a2 * cos + a1 * sin).astype(o_ref.dtype)


def _inproj_rope(x, ln_w, w, cos, sin, *, col0, n, tm, tn, n_q, k_scale):
    m, d = x.shape
    blk0 = col0 // tn
    p_tiles = cos.shape[0] // tm
    return pl.pallas_call(
        functools.partial(_inproj_rope_kernel, n_q=n_q, k_scale=k_scale, tn=tn),
        grid=(m // tm, n // tn),
        in_specs=[
            pl.BlockSpec((tm, d), lambda i, j: (i, 0)),
            pl.BlockSpec((1, d), lambda i, j: (0, 0)),
            pl.BlockSpec((d, tn), lambda i, j: (0, blk0 + j)),
            pl.BlockSpec((tm, LANE), lambda i, j: (i % p_tiles, 0)),
            pl.BlockSpec((tm, LANE), lambda i, j: (i % p_tiles, 0)),
        ],
        out_specs=[pl.BlockSpec((tm, tn), lambda i, j: (i, j)), pl.BlockSpec((tm, d), lambda i, j: (i, 0))],
        out_shape=[jax.ShapeDtypeStruct((m, n), BF16), jax.ShapeDtypeStruct((m, d), BF16)],
        compiler_params=_params(("arbitrary", "arbitrary")),
        name="inproj_rope",
    )(x, ln_w, w, cos, sin)


def _inproj(xn, w, *, col0, n, tm, tn):
    m, d = xn.shape
    blk0 = col0 // tn
    return pl.pallas_call(
        _matmul_kernel,
        grid=(m // tm, n // tn),
        in_specs=[pl.BlockSpec((tm, d), lambda i, j: (i, 0)), pl.BlockSpec((d, tn), lambda i, j: (0, blk0 + j))],
        out_specs=pl.BlockSpec((tm, tn), lambda i, j: (i, j)),
        out_shape=jax.ShapeDtypeStruct((m, n), BF16),
        compiler_params=_params(("arbitrary", "arbitrary")),
        name="inproj",
    )(xn, w)


def _wt_kernel(tbl_ref, *refs):
    del tbl_ref
    o_ref = refs[-1]
    rows = jnp.concatenate([r[...] for r in refs[:-1]], axis=0)
    o_ref[...] = rows.T.astype(BF16)


def _transpose_weight(w_t, src_rows):
    d = w_t.shape[1]
    parts = WT_TILE // WT_ROWS
    assert all(r % WT_ROWS == 0 for r in src_rows)
    tbl = np.asarray([r // WT_ROWS for r in src_rows], np.int32)
    grid_spec = pltpu.PrefetchScalarGridSpec(
        num_scalar_prefetch=1,
        grid=(len(src_rows),),
        in_specs=[pl.BlockSpec((WT_ROWS, d), lambda j, tbl, r=r: (tbl[j] + r, 0)) for r in range(parts)],
        out_specs=pl.BlockSpec((d, WT_TILE), lambda j, tbl: (0, j)),
    )
    return pl.pallas_call(
        _wt_kernel,
        grid_spec=grid_spec,
        out_shape=jax.ShapeDtypeStruct((d, len(src_rows) * WT_TILE), BF16),
        compiler_params=_params(("arbitrary",)),
        name="wt",
    )(tbl, *([w_t] * parts))


def _mla_prep_kernel(xn_ref, wc_ref, qnw_ref, wqa_ref, wqb_ref, kvnw_ref, cos_ref, sin_ref, *rest,
                     with_kv):
    if with_kv:
        wkv_ref, q_ref, ckv_ref, kr_ref, krp_ref, kv_ref = rest
    else:
        q_ref, ckv_ref, kr_ref = rest
    zc = jnp.dot(xn_ref[...], wc_ref[...], preferred_element_type=F32)
    cos = cos_ref[...]
    sin = sin_ref[...]
    cqn = _rms(zc[:, :Q_LORA], qnw_ref[...]).astype(BF16)
    qa = jnp.dot(cqn, wqa_ref[...], preferred_element_type=F32)
    qb = jnp.dot(cqn, wqb_ref[...], preferred_element_type=F32)
    qcos = cos * Q_SCALE
    qsin = sin * Q_SCALE
    for h in range(MLA_HEADS):
        lo = h * MLA_QPAD
        q_ref[:, lo:lo + LANE] = (qa[:, lo:lo + LANE] * Q_SCALE).astype(q_ref.dtype)
        q_ref[:, lo + LANE:lo + MLA_QPAD] = (
            qa[:, lo + LANE:lo + MLA_QPAD] * qcos + qb[:, h * LANE:(h + 1) * LANE] * qsin).astype(q_ref.dtype)
    c0 = Q_LORA
    ckvn = _rms(zc[:, c0:c0 + KV_LORA], kvnw_ref[...])
    ckv_ref[...] = ckvn
    k0 = c0 + KV_LORA
    krp = zc[:, k0:k0 + LANE] * cos + zc[:, k0 + LANE:k0 + 2 * LANE] * sin
    kr_ref[...] = krp[:, :MLA_ROPE]
    if with_kv:
        krp_ref[...] = krp.astype(BF16)
        kv_ref[...] = jnp.dot(ckvn.astype(BF16), wkv_ref[...], preferred_element_type=F32).astype(BF16)


def _mla_prep(xn, wc, qnw, wqa, wqb, kvnw, cos, sin, wkv, *, tm, with_kv):
    m, d = xn.shape
    p_tiles = cos.shape[0] // tm
    row = lambda i: (i, 0)
    fixed = lambda i: (0, 0)
    in_specs = [
        pl.BlockSpec((tm, d), row),
        pl.BlockSpec(wc.shape, fixed),
        pl.BlockSpec((1, Q_LORA), fixed),
        pl.BlockSpec(wqa.shape, fixed),
        pl.BlockSpec(wqb.shape, fixed),
        pl.BlockSpec((1, KV_LORA), fixed),
        pl.BlockSpec((tm, LANE), lambda i: (i % p_tiles, 0)),
        pl.BlockSpec((tm, LANE), lambda i: (i % p_tiles, 0)),
    ]
    args = [xn, wc, qnw, wqa, wqb, kvnw, cos, sin]
    qw = MLA_HEADS * MLA_QPAD
    out_specs = [pl.BlockSpec((tm, qw), row), pl.BlockSpec((tm, KV_LORA), row), pl.BlockSpec((tm, MLA_ROPE), row)]
    out_shape = [jax.ShapeDtypeStruct((m, qw), BF16), jax.ShapeDtypeStruct((m, KV_LORA), F32),
                 jax.ShapeDtypeStruct((m, MLA_ROPE), F32)]
    if with_kv:
        in_specs.append(pl.BlockSpec(wkv.shape, fixed))
        args.append(wkv)
        out_specs += [pl.BlockSpec((tm, LANE), row), pl.BlockSpec((tm, wkv.shape[1]), row)]
        out_shape += [jax.ShapeDtypeStruct((m, LANE), BF16), jax.ShapeDtypeStruct((m, wkv.shape[1]), BF16)]
    return pl.pallas_call(
        functools.partial(_mla_prep_kernel, with_kv=with_kv),
        grid=(m // tm,),
        in_specs=in_specs,
        out_specs=out_specs,
        out_shape=out_shape,
        compiler_params=_params(("arbitrary",)),
        name="mla_prep",
    )(*args)


def _ret_chunk(q, k, v, s_prev, dmask, qdec, kdec, gc):
    sc = lax.dot_general(q, k, _NT, preferred_element_type=F32) * dmask
    intra = jnp.dot(sc.astype(BF16), v, preferred_element_type=F32)
    cross = jnp.dot(q, s_prev.astype(BF16), preferred_element_type=F32) * qdec
    kd = (k.astype(F32) * kdec).astype(BF16)
    s_new = gc * s_prev + lax.dot_general(kd, v, _TN, preferred_element_type=F32)
    return intra + cross, s_new


def _ret_act(o, rg, gnw):
    mu = jnp.mean(o, axis=-1, keepdims=True)
    d = o - mu
    var = jnp.mean(d * d, axis=-1, keepdims=True)
    return d * lax.rsqrt(var + EPS) * gnw * _silu(rg.astype(F32))


def _ret_prompt_kernel(q_ref, k_ref, v_ref, rg_ref, gnw_ref, dm_ref, qd_ref, kd_ref, gc_ref,
                       o_ref, sfin_ref, s_ref, *, n_chunks, nh):
    s_ref[...] = jnp.zeros_like(s_ref)

    def chunk(c, carry):
        rows = pl.ds(pl.multiple_of(c * RET_CHUNK, RET_CHUNK), RET_CHUNK)
        for h in range(nh):
            qk_cols = slice(h * RET_DK, (h + 1) * RET_DK)
            v_cols = slice(h * RET_DV, (h + 1) * RET_DV)
            o, s_new = _ret_chunk(q_ref[rows, qk_cols], k_ref[rows, qk_cols], v_ref[rows, v_cols], s_ref[h],
                                  dm_ref[h], qd_ref[h], kd_ref[h], gc_ref[h])
            s_ref[h] = s_new
            o_ref[rows, v_cols] = _ret_act(o, rg_ref[rows, v_cols], gnw_ref[:, v_cols]).astype(o_ref.dtype)
        return carry

    lax.fori_loop(0, n_chunks, chunk, 0, unroll=4)
    sfin_ref[0] = s_ref[...]


def _ret_prompt(qk, slab, gn_w, dmask, qdec, kdec, gcs, *, batch, seq):
    nh = HEADS_PER_STEP
    qw, vw = nh * RET_DK, nh * RET_DV
    k_blk = RET_QK // qw
    rg_blk = _S_RG // vw
    return pl.pallas_call(
        functools.partial(_ret_prompt_kernel, n_chunks=seq // RET_CHUNK, nh=nh),
        grid=(batch, RET_HEADS // nh),
        in_specs=[
            pl.BlockSpec((seq, qw), lambda b, hp: (b, hp)),
            pl.BlockSpec((seq, qw), lambda b, hp: (b, k_blk + hp)),
            pl.BlockSpec((seq, vw), lambda b, hp: (b, hp)),
            pl.BlockSpec((seq, vw), lambda b, hp: (b, rg_blk + hp)),
            pl.BlockSpec((1, vw), lambda b, hp: (0, hp)),
            pl.BlockSpec((nh, RET_CHUNK, RET_CHUNK), lambda b, hp: (hp, 0, 0)),
            pl.BlockSpec((nh, RET_CHUNK, RET_DV), lambda b, hp: (hp, 0, 0)),
            pl.BlockSpec((nh, RET_CHUNK, RET_DK), lambda b, hp: (hp, 0, 0)),
            pl.BlockSpec((nh, 1, RET_DV), lambda b, hp: (hp, 0, 0)),
        ],
        out_specs=[
            pl.BlockSpec((seq, vw), lambda b, hp: (b, hp)),
            pl.BlockSpec((1, nh, RET_DK, RET_DV), lambda b, hp: (b, hp, 0, 0)),
        ],
        out_shape=[
            jax.ShapeDtypeStruct((batch * seq, RET_VW), BF16),
            jax.ShapeDtypeStruct((batch, RET_HEADS, RET_DK, RET_DV), F32),
        ],
        scratch_shapes=[pltpu.VMEM((nh, RET_DK, RET_DV), F32)],
        compiler_params=_params(("arbitrary", "arbitrary")),
        name="ret_prompt",
    )(qk, qk, slab, slab, gn_w, dmask, qdec, kdec, gcs)


def _ret_sample_kernel(qk_ref, vg_ref, gnw_ref, st_ref, dm_ref, qd_ref, kd_ref, gc_ref,
                       o_ref, snew_ref, qp_ref, kp_ref, vp_ref, act_ref, *, t, nb):
    qp_ref[...] = jnp.zeros_like(qp_ref)
    kp_ref[...] = jnp.zeros_like(kp_ref)
    vp_ref[...] = jnp.zeros_like(vp_ref)
    qk = qk_ref[...].astype(F32)
    vg = vg_ref[...].astype(F32)
    for bb in range(nb):
        rows = slice(bb * t, (bb + 1) * t)
        for h in range(RET_HEADS):
            qp_ref[0:t, :] = qk[rows, h * RET_DK:(h + 1) * RET_DK]
            kp_ref[0:t, :] = qk[rows, RET_QK + h * RET_DK:RET_QK + (h + 1) * RET_DK]
            vp_ref[0:t, :] = vg[rows, _S_RV + h * RET_DV:_S_RV + (h + 1) * RET_DV]
            o, s_new = _ret_chunk(qp_ref[...].astype(BF16), kp_ref[...].astype(BF16), vp_ref[...].astype(BF16),
                                  st_ref[bb, h], dm_ref[h], qd_ref[h], kd_ref[h], gc_ref[h])
            snew_ref[bb, h] = s_new
            act_ref[rows, h * RET_DV:(h + 1) * RET_DV] = _ret_act(
                o[0:t, :], vg[rows, _S_RG + h * RET_DV:_S_RG + (h + 1) * RET_DV],
                gnw_ref[:, h * RET_DV:(h + 1) * RET_DV])
    o_ref[...] = act_ref[...].astype(o_ref.dtype)


def _ret_sample(qk, slab, gn_w, state, dmask, qdec, kdec, gcs, *, batch, t):
    h = RET_HEADS
    whole = lambda b: (0, 0, 0)
    nb = _row_tile(batch, RET_SAMPLE_SEQS)
    return pl.pallas_call(
        functools.partial(_ret_sample_kernel, t=t, nb=nb),
        grid=(batch // nb,),
        in_specs=[
            pl.BlockSpec((nb * t, qk.shape[1]), lambda b: (b, 0)),
            pl.BlockSpec((nb * t, _S_MG), lambda b: (b, 0)),
            pl.BlockSpec((1, RET_VW), lambda b: (0, 0)),
            pl.BlockSpec((nb, h, RET_DK, RET_DV), lambda b: (b, 0, 0, 0)),
            pl.BlockSpec(dmask.shape, whole),
            pl.BlockSpec(qdec.shape, whole),
            pl.BlockSpec(kdec.shape, whole),
            pl.BlockSpec(gcs.shape, whole),
        ],
        out_specs=[
            pl.BlockSpec((nb * t, RET_VW), lambda b: (b, 0)),
            pl.BlockSpec((nb, h, RET_DK, RET_DV), lambda b: (b, 0, 0, 0)),
        ],
        out_shape=[
            jax.ShapeDtypeStruct((batch * t, RET_VW), BF16),
            jax.ShapeDtypeStruct(state.shape, state.dtype),
        ],
        scratch_shapes=[pltpu.VMEM((RET_CHUNK, RET_DK), F32), pltpu.VMEM((RET_CHUNK, RET_DK), F32),
                        pltpu.VMEM((RET_CHUNK, RET_DV), F32), pltpu.VMEM((nb * t, RET_VW), F32)],
        compiler_params=_params(("arbitrary",)),
        name="ret_sample",
    )(qk, slab, gn_w, state, dmask, qdec, kdec, gcs)


def _mla_prompt_kernel(q_ref, kv_ref, krp_ref, mg_ref, o_ref, m_ref, l_ref, acc_ref, *, tq, tk, nh):
    qi = pl.program_id(2)
    ratio = tq // tk

    def scores(kb, h):
        rows = pl.ds(pl.multiple_of(kb * tk, tk), tk)
        k = jnp.concatenate([kv_ref[rows, h * MLA_KVW:h * MLA_KVW + MLA_NOPE], krp_ref[rows, :]], axis=1)
        s = lax.dot_general(q_ref[:, h * MLA_QPAD:(h + 1) * MLA_QPAD], k, _NT, preferred_element_type=F32)
        return s, kv_ref[rows, h * MLA_KVW + MLA_NOPE:(h + 1) * MLA_KVW]

    def update(h, s, v, first=False):
        m_cur = _lane_rowmax(s)
        if first:
            m_new = m_cur
        else:
            m_old = m_ref[h]
            m_new = jnp.maximum(m_old, m_cur)
            alpha = jnp.exp2(m_old - m_new)
        p, l_part = _lane_exp(s, m_new)
        pv = jnp.dot(p, v, preferred_element_type=F32)
        m_ref[h] = m_new
        l_ref[h] = l_part if first else alpha * l_ref[h] + l_part
        acc_ref[h] = pv if first else alpha * acc_ref[h] + pv

    r_id = lax.broadcasted_iota(jnp.int32, (tq, tk), 0)
    c_id = lax.broadcasted_iota(jnp.int32, (tq, tk), 1)
    for d in range(ratio):
        for h in range(nh):
            s, v = scores(qi * ratio + d, h)
            update(h, jnp.where(c_id + d * tk <= r_id, s, -jnp.inf), v, first=d == 0)

    def body(kb, carry):
        for h in range(nh):
            s, v = scores(kb, h)
            update(h, s, v)
        return carry

    lax.fori_loop(0, qi * ratio, body, 0)
    for h in range(nh):
        cols = slice(h * MLA_V, (h + 1) * MLA_V)
        l = jnp.sum(l_ref[h], axis=-1, keepdims=True)
        o_ref[:, cols] = (acc_ref[h] / l * _silu(mg_ref[:, cols].astype(F32))).astype(o_ref.dtype)


def _mla_prompt(q, kv, krp, slab, *, batch, seq, tq, tk):
    nq = seq // tq
    nh = MLA_HEADS
    mg_blk = _S_MG // (nh * MLA_V)
    return pl.pallas_call(
        functools.partial(_mla_prompt_kernel, tq=tq, tk=tk, nh=nh),
        grid=(batch, MLA_HEADS // nh, nq),
        in_specs=[
            pl.BlockSpec((tq, nh * MLA_QPAD), lambda b, hp, i: (b * nq + i, hp)),
            pl.BlockSpec((seq, nh * MLA_KVW), lambda b, hp, i: (b, hp)),
            pl.BlockSpec((seq, LANE), lambda b, hp, i: (b, 0)),
            pl.BlockSpec((tq, nh * MLA_V), lambda b, hp, i: (b * nq + i, mg_blk + hp)),
        ],
        out_specs=pl.BlockSpec((tq, nh * MLA_V), lambda b, hp, i: (b * nq + i, hp)),
        out_shape=jax.ShapeDtypeStruct((batch * seq, MLA_VW), BF16),
        scratch_shapes=[pltpu.VMEM((nh, tq, LANE), F32), pltpu.VMEM((nh, tq, LANE), F32),
                        pltpu.VMEM((nh, tq, MLA_V), F32)],
        compiler_params=_params(("arbitrary", "arbitrary", "arbitrary")),
        name="mla_prompt",
    )(q, kv, krp, slab)


def _headproj_kernel(x_ref, w_ref, *rest, gated):
    if gated:
        g_ref, o_ref = rest
    else:
        (o_ref,) = rest
    y = jnp.dot(x_ref[...], w_ref[0], preferred_element_type=F32)
    if gated:
        y = y * _silu(g_ref[...].astype(F32))
    o_ref[...] = y.astype(o_ref.dtype)


def _headproj(x, w, *, x_stride, gate=None, gate_blk0=0):
    m = x.shape[0]
    nh, kdim, n = w.shape
    in_specs = [pl.BlockSpec((m, kdim), lambda h: (0, h * x_stride)),
                pl.BlockSpec((1, kdim, n), lambda h: (h, 0, 0))]
    args = [x, w]
    if gate is not None:
        in_specs.append(pl.BlockSpec((m, n), lambda h: (0, gate_blk0 + h)))
        args.append(gate)
    return pl.pallas_call(
        functools.partial(_headproj_kernel, gated=gate is not None),
        grid=(nh,),
        in_specs=in_specs,
        out_specs=pl.BlockSpec((m, n), lambda h: (0, h)),
        out_shape=jax.ShapeDtypeStruct((m, nh * n), BF16),
        compiler_params=_params(("arbitrary",)),
        name="headproj",
    )(*args)


def _page_copies(pt_ref, cache_c, cache_rt, cbuf, rbuf, sem, idx, slot, *, g, n_steps, page):
    b = idx // n_steps
    first = (idx % n_steps) * g
    out = []
    for j in range(g):
        p = pt_ref[b, first + j]
        out.append(pltpu.make_async_copy(cache_c.at[p], cbuf.at[slot, pl.ds(j * page, page), :], sem.at[slot, 0]))
        out.append(pltpu.make_async_copy(cache_rt.at[p], rbuf.at[slot, j], sem.at[slot, 1]))
    return out


def _mla_sample_kernel(pt_ref, ql_ref, qr_ref, cn_ref, rn_ref, cache_c, cache_rt, o_ref,
                       cbuf, rbuf, sem, m_ref, l_ref, acc_ref, *, t, page, g, n_steps):
    i = pl.program_id(0)
    last = pl.num_programs(0) - 1
    step = i % n_steps
    slot = i % N_SLOTS
    copies = functools.partial(_page_copies, pt_ref, cache_c, cache_rt, cbuf, rbuf, sem,
                               g=g, n_steps=n_steps, page=page)

    @pl.when(i == 0)
    def _():
        for ahead in range(N_SLOTS - 1):
            for c in copies(jnp.minimum(ahead, last), ahead):
                c.start()

    ql = ql_ref[0]
    qr = qr_ref[0]
    rows = ql.shape[0]

    @pl.when(step == 0)
    def _():
        qlf = ql.astype(F32)
        qrf = qr.astype(F32)
        tok = lax.broadcasted_iota(jnp.int32, (rows, 1), 0) // (rows // t)
        cn = cn_ref[0].astype(BF16).astype(F32)
        rn = rn_ref[0].astype(BF16).astype(F32)
        s_cols = []
        for j in range(t):
            sj = (jnp.sum(qlf * cn[j:j + 1, :], axis=-1, keepdims=True)
                  + jnp.sum(qrf * rn[j:j + 1, :], axis=-1, keepdims=True))
            s_cols.append(jnp.where(tok >= j, sj, -jnp.inf))
        m0 = s_cols[0]
        for j in range(1, t):
            m0 = jnp.maximum(m0, s_cols[j])
        l0 = jnp.zeros((rows, 1), F32)
        a0 = jnp.zeros((rows, KV_LORA), F32)
        for j in range(t):
            pj = jnp.exp2(s_cols[j] - m0)
            l0 = l0 + pj
            a0 = a0 + pj.astype(BF16).astype(F32) * cn[j:j + 1, :]
        lane = lax.broadcasted_iota(jnp.int32, (rows, LANE), 1)
        m_ref[...] = jnp.broadcast_to(m0, (rows, LANE))
        l_ref[...] = jnp.where(lane == 0, l0, 0.0)
        acc_ref[...] = a0

    for c in copies(i, slot):
        c.wait()
    ppk = g // (KEY_GROUPS * KEY_SPLITS)
    sub = ppk * page
    m_run, l_run, acc_run = m_ref[...], l_ref[...], acc_ref[...]
    for grp in range(KEY_GROUPS):
        cbs, parts = [], []
        for k in range(grp * KEY_SPLITS, (grp + 1) * KEY_SPLITS):
            cb = cbuf[slot, k * sub:(k + 1) * sub, :].astype(BF16)
            rb = jnp.concatenate([rbuf[slot, k * ppk + j] for j in range(ppk)], axis=1).astype(BF16)
            cbs.append(cb)
            parts.append(lax.dot_general(ql, cb, _NT, preferred_element_type=F32)
                         + jnp.dot(qr, rb, preferred_element_type=F32))
        s = jnp.concatenate(parts, axis=1)
        m_new = jnp.maximum(m_run, _lane_rowmax(s))
        alpha = jnp.exp2(m_run - m_new)
        pb, l_part = _lane_exp(s, m_new)
        pv = [jnp.dot(pb[:, k * sub:(k + 1) * sub], cbs[k], preferred_element_type=F32) for k in range(KEY_SPLITS)]
        while len(pv) > 1:
            pv = [pv[j] + pv[j + 1] for j in range(0, len(pv), 2)]
        m_run = m_new
        l_run = alpha * l_run + l_part
        acc_run = jnp.concatenate([alpha] * (KV_LORA // LANE), axis=1) * acc_run + pv[0]
    m_ref[...] = m_run
    l_ref[...] = l_run
    acc_ref[...] = acc_run

    ahead = N_SLOTS - 1
    for c in copies(jnp.minimum(i + ahead, last), (i + ahead) % N_SLOTS):
        c.start()

    @pl.when(step == n_steps - 1)
    def _():
        o_ref[0] = (acc_ref[...] / jnp.sum(l_ref[...], axis=-1, keepdims=True)).astype(o_ref.dtype)

    @pl.when(i == last)
    def _():
        for extra in range(1, N_SLOTS):
            for c in copies(last, (last + extra) % N_SLOTS):
                c.wait()


def _mla_sample(page_table, ql, qr, c_new, r_new, cache_c, cache_rt, *, t):
    batch, n_pages = page_table.shape
    page = cache_c.shape[1]
    g = PAGES_PER_STEP
    n_steps = n_pages // g
    rows = ql.shape[1]
    per_b = lambda i, pt: (i // n_steps, 0, 0)
    grid_spec = pltpu.PrefetchScalarGridSpec(
        num_scalar_prefetch=1,
        grid=(batch * n_steps,),
        in_specs=[
            pl.BlockSpec((1, rows, KV_LORA), per_b),
            pl.BlockSpec((1, rows, MLA_ROPE), per_b),
            pl.BlockSpec((1, t, KV_LORA), per_b),
            pl.BlockSpec((1, t, MLA_ROPE), per_b),
            pl.BlockSpec(memory_space=pl.ANY),
            pl.BlockSpec(memory_space=pl.ANY),
        ],
        out_specs=pl.BlockSpec((1, rows, KV_LORA), per_b),
        scratch_shapes=[
            pltpu.VMEM((N_SLOTS, g * page, KV_LORA), F32),
            pltpu.VMEM((N_SLOTS, g, MLA_ROPE, page), F32),
            pltpu.SemaphoreType.DMA((N_SLOTS, 2)),
            pltpu.VMEM((rows, LANE), F32), pltpu.VMEM((rows, LANE), F32), pltpu.VMEM((rows, KV_LORA), F32),
        ],
    )
    return pl.pallas_call(
        functools.partial(_mla_sample_kernel, t=t, page=page, g=g, n_steps=n_steps),
        grid_spec=grid_spec,
        out_shape=jax.ShapeDtypeStruct((batch, rows, KV_LORA), BF16),
        compiler_params=_params(("arbitrary",)),
        name="mla_sample",
    )(page_table, ql, qr, c_new, r_new, cache_c, cache_rt)


def _finish_kernel(x_ref, ar_ref, am_ref, ga_ref, gb_ref, ple_ref, wdr_ref, wdm_ref, wo_ref, wpg_ref, wpp_ref,
                   fw_ref, o_ref, *, final):
    h_ret = jnp.dot(ar_ref[...], wdr_ref[...], preferred_element_type=F32)
    h_mla = jnp.dot(am_ref[...], wdm_ref[...], preferred_element_type=F32)
    merged = (jax.nn.sigmoid(ga_ref[...].astype(F32)) * h_ret
              + jax.nn.sigmoid(gb_ref[...].astype(F32)) * h_mla)
    x2 = x_ref[...] + jnp.dot(merged.astype(BF16), wo_ref[...], preferred_element_type=F32)
    gate = jax.nn.sigmoid(jnp.dot(x2.astype(BF16), wpg_ref[...], preferred_element_type=F32))
    y = x2 + gate * jnp.dot(ple_ref[...].astype(BF16), wpp_ref[...], preferred_element_type=F32)
    if final:
        y = _rms(y, fw_ref[...])
    o_ref[...] = y


def _finish(x, act_ret, act_mla, slab, ple, wdr, wdm, wo, wpg, wpp, fw, *, tm, final):
    m, d = x.shape
    row = lambda i: (i, 0)
    fixed = lambda i: (0, 0)
    once = pl.Buffered(1)
    return pl.pallas_call(
        functools.partial(_finish_kernel, final=final),
        grid=(m // tm,),
        in_specs=[
            pl.BlockSpec((tm, d), row),
            pl.BlockSpec((tm, RET_VW), row),
            pl.BlockSpec((tm, MLA_VW), row),
            pl.BlockSpec((tm, d), lambda i: (i, _S_GA // D_MODEL)),
            pl.BlockSpec((tm, d), lambda i: (i, _S_GB // D_MODEL)),
            pl.BlockSpec((tm, PLE_DIM), row),
            pl.BlockSpec(wdr.shape, fixed, pipeline_mode=once),
            pl.BlockSpec(wdm.shape, fixed, pipeline_mode=once),
            pl.BlockSpec(wo.shape, fixed, pipeline_mode=once),
            pl.BlockSpec(wpg.shape, fixed, pipeline_mode=once),
            pl.BlockSpec(wpp.shape, fixed, pipeline_mode=once),
            pl.BlockSpec((1, d), fixed),
        ],
        out_specs=pl.BlockSpec((tm, d), row),
        out_shape=jax.ShapeDtypeStruct((m, d), F32),
        compiler_params=_params(("arbitrary",)),
        name="finish",
    )(x, act_ret, act_mla, slab, slab, ple, wdr, wdm, wo, wpg, wpp, fw)


def _rope_tables(pos, half):
    inv_freq = ROPE_BASE ** (-np.arange(half, dtype=np.float64) / half)
    ang = np.asarray(pos, np.float64)[:, None] * inv_freq[None, :]
    return np.cos(ang).astype(np.float32), np.sin(ang).astype(np.float32)


def _mla_rope_tables(pos):
    cos, sin = _rope_tables(pos, MLA_ROPE // 2)
    z = np.zeros((len(pos), LANE - MLA_ROPE), np.float32)
    return np.concatenate([cos, cos, z], axis=1), np.concatenate([sin, sin, z], axis=1)


def _decay_tables(c, rows):
    log_g = np.log1p(-np.exp2(-5.0 - np.arange(RET_HEADS, dtype=np.float64)))
    idx = np.arange(c, dtype=np.float64)
    diff = idx[:, None] - idx[None, :]
    dmask = np.where(diff >= 0, np.exp(np.maximum(diff, 0.0)[None] * log_g[:, None, None]), 0.0)
    qdec = np.exp((idx[None, :] + 1.0) * log_g[:, None])[:, :, None]
    kdec = np.exp((c - 1.0 - idx)[None, :] * log_g[:, None])[:, :, None]
    pad = rows - c
    dmask = np.pad(dmask, ((0, 0), (0, pad), (0, pad)))
    qdec = np.broadcast_to(np.pad(qdec, ((0, 0), (0, pad), (0, 0))), (RET_HEADS, rows, RET_DV))
    kdec = np.broadcast_to(np.pad(kdec, ((0, 0), (0, pad), (0, 0))), (RET_HEADS, rows, RET_DK))
    gcs = np.broadcast_to(np.exp(c * log_g)[:, None, None], (RET_HEADS, 1, RET_DV))
    return tuple(np.ascontiguousarray(a, np.float32) for a in (dmask, qdec, kdec, gcs))


def _rot_cols(w):
    half = w.shape[-1] // 2
    return jnp.concatenate([-w[..., half:], w[..., :half]], axis=-1)


def _layer_weights(w_in, w_uq, w_ukv, w_down_ret, w_down_mla, w_out, w_ple_gate, w_ple_proj):
    d = w_in.shape[0]
    src = [r for lo, hi in ((_OFF_RV, _OFF_CQ), (_OFF_MG, w_in.shape[1]), (_OFF_RQ, _OFF_RV))
           for r in range(lo, hi, WT_TILE)]
    n_main = len(src) * WT_TILE
    src += list(range(_OFF_CQ, _OFF_MG, WT_TILE))
    w_main = _transpose_weight(w_in.T, src)
    c_cols = w_main[:, n_main:n_main + _OFF_MG - _OFF_CQ]
    kr = c_cols[:, _OFF_KR - _OFF_CQ:]
    zpad = jnp.zeros((d, LANE - MLA_ROPE), BF16)
    w_c = jnp.concatenate([c_cols, zpad, _rot_cols(kr), zpad], axis=1)
    uq = w_uq.reshape(Q_LORA, MLA_HEADS, MLA_QK)
    uq_rope = uq[:, :, MLA_NOPE:]
    zq = jnp.zeros((Q_LORA, MLA_HEADS, MLA_QPAD - MLA_QK), F32)
    w_qa = jnp.concatenate([uq, zq], axis=2).reshape(Q_LORA, MLA_HEADS * MLA_QPAD).astype(BF16)
    w_qb = jnp.concatenate([_rot_cols(uq_rope), zq], axis=2).reshape(Q_LORA, MLA_HEADS * LANE).astype(BF16)
    ukv = w_ukv.reshape(KV_LORA, MLA_HEADS, MLA_KVW)
    w_uk_t = jnp.transpose(ukv[:, :, :MLA_NOPE], (1, 2, 0)).astype(BF16)
    w_uv = jnp.transpose(ukv[:, :, MLA_NOPE:], (1, 0, 2)).astype(BF16)
    return dict(w_main=w_main, w_c=w_c, w_qa=w_qa, w_qb=w_qb, w_kv=w_ukv.astype(BF16),
                w_uk_t=w_uk_t, w_uv=w_uv, wdr=w_down_ret.astype(BF16), wdm=w_down_mla.astype(BF16),
                wo=w_out.astype(BF16), wpg=w_ple_gate.astype(BF16), wpp=w_ple_proj.astype(BF16))


def _row_tile(m, cap):
    t = min(m, cap)
    assert m % t == 0
    return t


def _project(x, ln_w, lw, cos_r, sin_r):
    tm = _row_tile(cos_r.shape[0], 1024)
    assert x.shape[0] % tm == 0
    qk, xn = _inproj_rope(x, ln_w, lw["w_main"], cos_r, sin_r, col0=_S_W, n=2 * RET_QK, tm=tm, tn=1024,
                          n_q=RET_QK // 1024, k_scale=RET_DK ** -0.5)
    slab = _inproj(xn, lw["w_main"], col0=0, n=_S_W, tm=tm, tn=SLAB_TN)
    return qk, slab, xn


def kernel(x_prompt, x_sample, cache_ckv, cache_krope, state_ret, page_table, p_prompt, p_sample, ln_w, w_in,
           q_norm_w, w_uq, kv_norm_w, w_ukv, ret_gn_w, w_down_ret, w_down_mla, w_out, w_ple_gate, w_ple_proj,
           final_norm_w):
    depth = w_in.shape[0]
    b, s, d = x_prompt.shape
    bd, t, _ = x_sample.shape
    n_pages = page_table.shape[1]
    page = cache_ckv.shape[2]
    past_len = n_pages * page
    assert s % RET_CHUNK == 0 and n_pages % PAGES_PER_STEP == 0

    pos_p = np.arange(s)
    pos_s = np.tile(past_len + np.arange(t), bd)
    cos_rp, sin_rp = _rope_tables(pos_p, RET_DK // 2)
    cos_rs, sin_rs = _rope_tables(pos_s, RET_DK // 2)
    cos_mp, sin_mp = _mla_rope_tables(pos_p)
    cos_ms, sin_ms = _mla_rope_tables(pos_s)
    dec_p = _decay_tables(RET_CHUNK, RET_CHUNK)
    dec_s = _decay_tables(t, RET_CHUNK)

    y_p = x_prompt.reshape(b * s, d)
    y_s = x_sample.reshape(bd * t, d)
    tq = _row_tile(s, 512)
    tk = tq
    outs = [[] for _ in range(6)]
    for i in range(depth):
        lw = _layer_weights(w_in[i], w_uq[i], w_ukv[i], w_down_ret[i], w_down_mla[i], w_out[i], w_ple_gate[i],
                            w_ple_proj[i])
        lnw = ln_w[i][None, :]
        qnw = q_norm_w[i][None, :]
        kvnw = kv_norm_w[i][None, :]
        gnw = ret_gn_w[i][None, :]
        fw = final_norm_w[None, :]
        final = i == depth - 1

        qk, slab, xn = _project(y_p, lnw, lw, cos_rp, sin_rp)
        q, ckv, kr, krp, kv = _mla_prep(xn, lw["w_c"], qnw, lw["w_qa"], lw["w_qb"], kvnw, cos_mp, sin_mp,
                                        lw["w_kv"], tm=_row_tile(s, 512), with_kv=True)
        act_ret, s_fin = _ret_prompt(qk, slab, gnw, *dec_p, batch=b, seq=s)
        act_mla = _mla_prompt(q, kv, krp, slab, batch=b, seq=s, tq=tq, tk=tk)
        y_p = _finish(y_p, act_ret, act_mla, slab, p_prompt[i].reshape(b * s, -1), lw["wdr"], lw["wdm"], lw["wo"],
                      lw["wpg"], lw["wpp"], fw, tm=_row_tile(b * s, 512), final=final)
        outs[0].append(ckv.reshape(b, s, KV_LORA))
        outs[1].append(kr.reshape(b, s, MLA_ROPE))
        outs[2].append(s_fin.astype(x_prompt.dtype))

        ms = bd * t
        qk, slab, xn = _project(y_s, lnw, lw, cos_rs, sin_rs)
        q, ckv, kr = _mla_prep(xn, lw["w_c"], qnw, lw["w_qa"], lw["w_qb"], kvnw, cos_ms, sin_ms, None,
                               tm=ms, with_kv=False)
        act_ret, s_new = _ret_sample(qk, slab, gnw, state_ret[i], *dec_s, batch=bd, t=t)
        q_lat = _headproj(q, lw["w_uk_t"], x_stride=MLA_QPAD // LANE)
        q_lat = q_lat.reshape(bd, t * MLA_HEADS, KV_LORA)
        q_rope = q.reshape(bd, t * MLA_HEADS, MLA_QPAD)[..., MLA_NOPE:MLA_QK]
        cache_rt = jnp.swapaxes(cache_krope[i], 1, 2)
        o_lat = _mla_sample(page_table, q_lat, q_rope, ckv.reshape(bd, t, KV_LORA), kr.reshape(bd, t, MLA_ROPE),
                            cache_ckv[i], cache_rt, t=t)
        o_lat = o_lat.reshape(ms, MLA_HEADS * KV_LORA)
        act_mla = _headproj(o_lat, lw["w_uv"], x_stride=1, gate=slab, gate_blk0=_S_MG // MLA_V)
        y_s = _finish(y_s, act_ret, act_mla, slab, p_sample[i].reshape(ms, -1), lw["wdr"], lw["wdm"], lw["wo"],
                      lw["wpg"], lw["wpp"], fw, tm=_row_tile(ms, 512), final=final)
        outs[3].append(ckv.reshape(bd, t, KV_LORA))
        outs[4].append(kr.reshape(bd, t, MLA_ROPE))
        outs[5].append(s_new.astype(state_ret.dtype))

    y_prompt = y_p.reshape(b, s, d)
    y_sample = y_s.reshape(bd, t, d)
    ckv_p, kr_p, ret_p, ckv_s, kr_s, ret_s = [jnp.stack(o) for o in outs]
    return (y_prompt, y_sample, ckv_p, kr_p, ret_p, ckv_s, kr_s, ret_s)
```

```python
import functools
import math

import numpy as np
import jax
import jax.numpy as jnp
from jax import lax
from jax.experimental import pallas as pl
from jax.experimental.pallas import tpu as pltpu

F32 = jnp.float32
BF16 = jnp.bfloat16

D_MODEL = 1024
PLE_DIM = 256
RET_HEADS = 4
RET_DK = 256
RET_DV = 512
RET_CHUNK = 128
MLA_HEADS = 8
MLA_NOPE = 128
MLA_ROPE = 64
MLA_V = 128
Q_LORA = 384
KV_LORA = 256
ROPE_BASE = 10000.0
EPS = 1e-6

RET_QK = RET_HEADS * RET_DK
RET_VW = RET_HEADS * RET_DV
MLA_QK = MLA_NOPE + MLA_ROPE
MLA_VW = MLA_HEADS * MLA_V
MLA_QPAD = 256
MLA_KVW = MLA_NOPE + MLA_V
Q_SCALE = MLA_QK ** -0.5 * math.log2(math.e)
LANE = 128
PAGES_PER_STEP = 32
N_SLOTS = 3
KEY_GROUPS = 2
KEY_SPLITS = 2
HEADS_PER_STEP = 2
SLAB_TN = 1792
WT_TILE = 512
WT_ROWS = 64
RET_SAMPLE_SEQS = 4
VMEM_LIMIT = 48 * 1024 * 1024

_OFF_RQ = 0
_OFF_RK = _OFF_RQ + RET_QK
_OFF_RV = _OFF_RK + RET_QK
_OFF_RG = _OFF_RV + RET_VW
_OFF_CQ = _OFF_RG + RET_VW
_OFF_CKV = _OFF_CQ + Q_LORA
_OFF_KR = _OFF_CKV + KV_LORA
_OFF_MG = _OFF_KR + MLA_ROPE
_OFF_GA = _OFF_MG + MLA_VW
_OFF_GB = _OFF_GA + D_MODEL
_S_RV = 0
_S_RG = _S_RV + RET_VW
_S_MG = _S_RG + RET_VW
_S_GA = _S_MG + MLA_VW
_S_GB = _S_GA + D_MODEL
_S_W = _S_GB + D_MODEL

_NT = (((1,), (1,)), ((), ()))
_TN = (((0,), (0,)), ((), ()))


def _params(sem):
    return pltpu.CompilerParams(dimension_semantics=sem, vmem_limit_bytes=VMEM_LIMIT)


def _rms(x, w):
    return x * lax.rsqrt(jnp.mean(x * x, axis=-1, keepdims=True) + EPS) * w


def _silu(x):
    return x * jax.nn.sigmoid(x)


def _lane_blocks(s):
    return [s[:, c * LANE:(c + 1) * LANE] for c in range(s.shape[1] // LANE)]


def _lane_rowmax(s):
    blocks = _lane_blocks(s)
    mx = blocks[0]
    for blk in blocks[1:]:
        mx = jnp.maximum(mx, blk)
    return jnp.broadcast_to(jnp.max(mx, axis=-1, keepdims=True), mx.shape)


def _lane_exp(s, m):
    ps = [jnp.exp2(blk - m) for blk in _lane_blocks(s)]
    part = ps[0]
    for p in ps[1:]:
        part = part + p
    return jnp.concatenate([p.astype(BF16) for p in ps], axis=1), part


def _matmul_kernel(x_ref, w_ref, o_ref):
    o_ref[...] = jnp.dot(x_ref[...], w_ref[...], preferred_element_type=F32).astype(o_ref.dtype)


def _inproj_rope_kernel(x_ref, lnw_ref, w_ref, cos_ref, sin_ref, o_ref, xn_ref, *, n_q, k_scale, tn):
    j = pl.program_id(1)

    @pl.when(j == 0)
    def _():
        xn_ref[...] = _rms(x_ref[...], lnw_ref[...]).astype(BF16)

    acc = jnp.dot(xn_ref[...], w_ref[...], preferred_element_type=F32)
    scale = jnp.where(j >= n_q, k_scale, 1.0).astype(F32)
    cos = cos_ref[...] * scale
    sin = sin_ref[...] * scale
    for h in range(tn // RET_DK):
        lo = h * RET_DK
        a1 = acc[:, lo:lo + LANE]
        a2 = acc[:, lo + LANE:lo + RET_DK]
        o_ref[:, lo:lo + LANE] = (a1 * cos - a2 * sin).astype(o_ref.dtype)
        o_ref[:, lo + LANE:lo + RET_DK] = (a2 * cos + a1 * sin).astype(o_ref.dtype)


def _inproj_rope(x, ln_w, w, cos, sin, *, col0, n, tm, tn, n_q, k_scale):
    m, d = x.shape
    blk0 = col0 // tn
    p_tiles = cos.shape[0] // tm
    return pl.pallas_call(
        functools.partial(_inproj_rope_kernel, n_q=n_q, k_scale=k_scale, tn=tn),
        grid=(m // tm, n // tn),
        in_specs=[
            pl.BlockSpec((tm, d), lambda i, j: (i, 0)),
            pl.BlockSpec((1, d), lambda i, j: (0, 0)),
            pl.BlockSpec((d, tn), lambda i, j: (0, blk0 + j)),
            pl.BlockSpec((tm, LANE), lambda i, j: (i % p_tiles, 0)),
            pl.BlockSpec((tm, LANE), lambda i, j: (i % p_tiles, 0)),
        ],
        out_specs=[pl.BlockSpec((tm, tn), lambda i, j: (i, j)), pl.BlockSpec((tm, d), lambda i, j: (i, 0))],
        out_shape=[jax.ShapeDtypeStruct((m, n), BF16), jax.ShapeDtypeStruct((m, d), BF16)],
        compiler_params=_params(("arbitrary", "arbitrary")),
        name="inproj_rope",
    )(x, ln_w, w, cos, sin)


def _inproj(xn, w, *, col0, n, tm, tn):
    m, d = xn.shape
    blk0 = col0 // tn
    return pl.pallas_call(
        _matmul_kernel,
        grid=(m // tm, n // tn),
        in_specs=[pl.BlockSpec((tm, d), lambda i, j: (i, 0)), pl.BlockSpec((d, tn), lambda i, j: (0, blk0 + j))],
        out_specs=pl.BlockSpec((tm, tn), lambda i, j: (i, j)),
        out_shape=jax.ShapeDtypeStruct((m, n), BF16),
        compiler_params=_params(("arbitrary", "arbitrary")),
        name="inproj",
    )(xn, w)


def _wt_kernel(tbl_ref, *refs):
    del tbl_ref
    o_ref = refs[-1]
    rows = jnp.concatenate([r[...] for r in refs[:-1]], axis=0)
    o_ref[...] = rows.T.astype(BF16)


def _transpose_weight(w_t, src_rows):
    d = w_t.shape[1]
    parts = WT_TILE // WT_ROWS
    assert all(r % WT_ROWS == 0 for r in src_rows)
    tbl = np.asarray([r // WT_ROWS for r in src_rows], np.int32)
    grid_spec = pltpu.PrefetchScalarGridSpec(
        num_scalar_prefetch=1,
        grid=(len(src_rows),),
        in_specs=[pl.BlockSpec((WT_ROWS, d), lambda j, tbl, r=r: (tbl[j] + r, 0)) for r in range(parts)],
        out_specs=pl.BlockSpec((d, WT_TILE), lambda j, tbl: (0, j)),
    )
    return pl.pallas_call(
        _wt_kernel,
        grid_spec=grid_spec,
        out_shape=jax.ShapeDtypeStruct((d, len(src_rows) * WT_TILE), BF16),
        compiler_params=_params(("arbitrary",)),
        name="wt",
    )(tbl, *([w_t] * parts))


def _mla_prep_kernel(xn_ref, wc_ref, qnw_ref, wqa_ref, wqb_ref, kvnw_ref, cos_ref, sin_ref, *rest,
                     with_kv):
    if with_kv:
        wkv_ref, q_ref, ckv_ref, kr_ref, krp_ref, kv_ref = rest
    else:
        q_ref, ckv_ref, kr_ref = rest
    zc = jnp.dot(xn_ref[...], wc_ref[...], preferred_element_type=F32)
    cos = cos_ref[...]
    sin = sin_ref[...]
    cqn = _rms(zc[:, :Q_LORA], qnw_ref[...]).astype(BF16)
    qa = jnp.dot(cqn, wqa_ref[...], preferred_element_type=F32)
    qb = jnp.dot(cqn, wqb_ref[...], preferred_element_type=F32)
    qcos = cos * Q_SCALE
    qsin = sin * Q_SCALE
    for h in range(MLA_HEADS):
        lo = h * MLA_QPAD
        q_ref[:, lo:lo + LANE] = (qa[:, lo:lo + LANE] * Q_SCALE).astype(q_ref.dtype)
        q_ref[:, lo + LANE:lo + MLA_QPAD] = (
            qa[:, lo + LANE:lo + MLA_QPAD] * qcos + qb[:, h * LANE:(h + 1) * LANE] * qsin).astype(q_ref.dtype)
    c0 = Q_LORA
    ckvn = _rms(zc[:, c0:c0 + KV_LORA], kvnw_ref[...])
    ckv_ref[...] = ckvn
    k0 = c0 + KV_LORA
    krp = zc[:, k0:k0 + LANE] * cos + zc[:, k0 + LANE:k0 + 2 * LANE] * sin
    kr_ref[...] = krp[:, :MLA_ROPE]
    if with_kv:
        krp_ref[...] = krp.astype(BF16)
        kv_ref[...] = jnp.dot(ckvn.astype(BF16), wkv_ref[...], preferred_element_type=F32).astype(BF16)


def _mla_prep(xn, wc, qnw, wqa, wqb, kvnw, cos, sin, wkv, *, tm, with_kv):
    m, d = xn.shape
    p_tiles = cos.shape[0] // tm
    row = lambda i: (i, 0)
    fixed = lambda i: (0, 0)
    in_specs = [
        pl.BlockSpec((tm, d), row),
        pl.BlockSpec(wc.shape, fixed),
        pl.BlockSpec((1, Q_LORA), fixed),
        pl.BlockSpec(wqa.shape, fixed),
        pl.BlockSpec(wqb.shape, fixed),
        pl.BlockSpec((1, KV_LORA), fixed),
        pl.BlockSpec((tm, LANE), lambda i: (i % p_tiles, 0)),
        pl.BlockSpec((tm, LANE), lambda i: (i % p_tiles, 0)),
    ]
    args = [xn, wc, qnw, wqa, wqb, kvnw, cos, sin]
    qw = MLA_HEADS * MLA_QPAD
    out_specs = [pl.BlockSpec((tm, qw), row), pl.BlockSpec((tm, KV_LORA), row), pl.BlockSpec((tm, MLA_ROPE), row)]
    out_shape = [jax.ShapeDtypeStruct((m, qw), BF16), jax.ShapeDtypeStruct((m, KV_LORA), F32),
                 jax.ShapeDtypeStruct((m, MLA_ROPE), F32)]
    if with_kv:
        in_specs.append(pl.BlockSpec(wkv.shape, fixed))
        args.append(wkv)
        out_specs += [pl.BlockSpec((tm, LANE), row), pl.BlockSpec((tm, wkv.shape[1]), row)]
        out_shape += [jax.ShapeDtypeStruct((m, LANE), BF16), jax.ShapeDtypeStruct((m, wkv.shape[1]), BF16)]
    return pl.pallas_call(
        functools.partial(_mla_prep_kernel, with_kv=with_kv),
        grid=(m // tm,),
        in_specs=in_specs,
        out_specs=out_specs,
        out_shape=out_shape,
        compiler_params=_params(("arbitrary",)),
        name="mla_prep",
    )(*args)


def _ret_chunk(q, k, v, s_prev, dmask, qdec, kdec, gc):
    sc = lax.dot_general(q, k, _NT, preferred_element_type=F32) * dmask
    intra = jnp.dot(sc.astype(BF16), v, preferred_element_type=F32)
    cross = jnp.dot(q, s_prev.astype(BF16), preferred_element_type=F32) * qdec
    kd = (k.astype(F32) * kdec).astype(BF16)
    s_new = gc * s_prev + lax.dot_general(kd, v, _TN, preferred_element_type=F32)
    return intra + cross, s_new


def _ret_act(o, rg, gnw):
    mu = jnp.mean(o, axis=-1, keepdims=True)
    d = o - mu
    var = jnp.mean(d * d, axis=-1, keepdims=True)
    return d * lax.rsqrt(var + EPS) * gnw * _silu(rg.astype(F32))


def _ret_prompt_kernel(q_ref, k_ref, v_ref, rg_ref, gnw_ref, dm_ref, qd_ref, kd_ref, gc_ref,
                       o_ref, sfin_ref, s_ref, *, n_chunks, nh):
    s_ref[...] = jnp.zeros_like(s_ref)

    def chunk(c, carry):
        rows = pl.ds(pl.multiple_of(c * RET_CHUNK, RET_CHUNK), RET_CHUNK)
        for h in range(nh):
            qk_cols = slice(h * RET_DK, (h + 1) * RET_DK)
            v_cols = slice(h * RET_DV, (h + 1) * RET_DV)
            o, s_new = _ret_chunk(q_ref[rows, qk_cols], k_ref[rows, qk_cols], v_ref[rows, v_cols], s_ref[h],
                                  dm_ref[h], qd_ref[h], kd_ref[h], gc_ref[h])
            s_ref[h] = s_new
            o_ref[rows, v_cols] = _ret_act(o, rg_ref[rows, v_cols], gnw_ref[:, v_cols]).astype(o_ref.dtype)
        return carry

    lax.fori_loop(0, n_chunks, chunk, 0, unroll=4)
    sfin_ref[0] = s_ref[...]


def _ret_prompt(qk, slab, gn_w, dmask, qdec, kdec, gcs, *, batch, seq):
    nh = HEADS_PER_STEP
    qw, vw = nh * RET_DK, nh * RET_DV
    k_blk = RET_QK // qw
    rg_blk = _S_RG // vw
    return pl.pallas_call(
        functools.partial(_ret_prompt_kernel, n_chunks=seq // RET_CHUNK, nh=nh),
        grid=(batch, RET_HEADS // nh),
        in_specs=[
            pl.BlockSpec((seq, qw), lambda b, hp: (b, hp)),
            pl.BlockSpec((seq, qw), lambda b, hp: (b, k_blk + hp)),
            pl.BlockSpec((seq, vw), lambda b, hp: (b, hp)),
            pl.BlockSpec((seq, vw), lambda b, hp: (b, rg_blk + hp)),
            pl.BlockSpec((1, vw), lambda b, hp: (0, hp)),
            pl.BlockSpec((nh, RET_CHUNK, RET_CHUNK), lambda b, hp: (hp, 0, 0)),
            pl.BlockSpec((nh, RET_CHUNK, RET_DV), lambda b, hp: (hp, 0, 0)),
            pl.BlockSpec((nh, RET_CHUNK, RET_DK), lambda b, hp: (hp, 0, 0)),
            pl.BlockSpec((nh, 1, RET_DV), lambda b, hp: (hp, 0, 0)),
        ],
        out_specs=[
            pl.BlockSpec((seq, vw), lambda b, hp: (b, hp)),
            pl.BlockSpec((1, nh, RET_DK, RET_DV), lambda b, hp: (b, hp, 0, 0)),
        ],
        out_shape=[
            jax.ShapeDtypeStruct((batch * seq, RET_VW), BF16),
            jax.ShapeDtypeStruct((batch, RET_HEADS, RET_DK, RET_DV), F32),
        ],
        scratch_shapes=[pltpu.VMEM((nh, RET_DK, RET_DV), F32)],
        compiler_params=_params(("arbitrary", "arbitrary")),
        name="ret_prompt",
    )(qk, qk, slab, slab, gn_w, dmask, qdec, kdec, gcs)


def _ret_sample_kernel(qk_ref, vg_ref, gnw_ref, st_ref, dm_ref, qd_ref, kd_ref, gc_ref,
                       o_ref, snew_ref, qp_ref, kp_ref, vp_ref, act_ref, *, t, nb):
    qp_ref[...] = jnp.zeros_like(qp_ref)
    kp_ref[...] = jnp.zeros_like(kp_ref)
    vp_ref[...] = jnp.zeros_like(vp_ref)
    qk = qk_ref[...].astype(F32)
    vg = vg_ref[...].astype(F32)
    for bb in range(nb):
        rows = slice(bb * t, (bb + 1) * t)
        for h in range(RET_HEADS):
            qp_ref[0:t, :] = qk[rows, h * RET_DK:(h + 1) * RET_DK]
            kp_ref[0:t, :] = qk[rows, RET_QK + h * RET_DK:RET_QK + (h + 1) * RET_DK]
            vp_ref[0:t, :] = vg[rows, _S_RV + h * RET_DV:_S_RV + (h + 1) * RET_DV]
            o, s_new = _ret_chunk(qp_ref[...].astype(BF16), kp_ref[...].astype(BF16), vp_ref[...].astype(BF16),
                                  st_ref[bb, h], dm_ref[h], qd_ref[h], kd_ref[h], gc_ref[h])
            snew_ref[bb, h] = s_new
            act_ref[rows, h * RET_DV:(h + 1) * RET_DV] = _ret_act(
                o[0:t, :], vg[rows, _S_RG + h * RET_DV:_S_RG + (h + 1) * RET_DV],
                gnw_ref[:, h * RET_DV:(h + 1) * RET_DV])
    o_ref[...] = act_ref[...].astype(o_ref.dtype)


def _ret_sample(qk, slab, gn_w, state, dmask, qdec, kdec, gcs, *, batch, t):
    h = RET_HEADS
    whole = lambda b: (0, 0, 0)
    nb = _row_tile(batch, RET_SAMPLE_SEQS)
    return pl.pallas_call(
        functools.partial(_ret_sample_kernel, t=t, nb=nb),
        grid=(batch // nb,),
        in_specs=[
            pl.BlockSpec((nb * t, qk.shape[1]), lambda b: (b, 0)),
            pl.BlockSpec((nb * t, _S_MG), lambda b: (b, 0)),
            pl.BlockSpec((1, RET_VW), lambda b: (0, 0)),
            pl.BlockSpec((nb, h, RET_DK, RET_DV), lambda b: (b, 0, 0, 0)),
            pl.BlockSpec(dmask.shape, whole),
            pl.BlockSpec(qdec.shape, whole),
            pl.BlockSpec(kdec.shape, whole),
            pl.BlockSpec(gcs.shape, whole),
        ],
        out_specs=[
            pl.BlockSpec((nb * t, RET_VW), lambda b: (b, 0)),
            pl.BlockSpec((nb, h, RET_DK, RET_DV), lambda b: (b, 0, 0, 0)),
        ],
        out_shape=[
            jax.ShapeDtypeStruct((batch * t, RET_VW), BF16),
            jax.ShapeDtypeStruct(state.shape, state.dtype),
        ],
        scratch_shapes=[pltpu.VMEM((RET_CHUNK, RET_DK), F32), pltpu.VMEM((RET_CHUNK, RET_DK), F32),
                        pltpu.VMEM((RET_CHUNK, RET_DV), F32), pltpu.VMEM((nb * t, RET_VW), F32)],
        compiler_params=_params(("arbitrary",)),
        name="ret_sample",
    )(qk, slab, gn_w, state, dmask, qdec, kdec, gcs)


def _mla_prompt_kernel(q_ref, kv_ref, krp_ref, mg_ref, o_ref, m_ref, l_ref, acc_ref, *, tq, tk, nh):
    qi = pl.program_id(2)
    ratio = tq // tk

    def scores(kb, h):
        rows = pl.ds(pl.multiple_of(kb * tk, tk), tk)
        k = jnp.concatenate([kv_ref[rows, h * MLA_KVW:h * MLA_KVW + MLA_NOPE], krp_ref[rows, :]], axis=1)
        s = lax.dot_general(q_ref[:, h * MLA_QPAD:(h + 1) * MLA_QPAD], k, _NT, preferred_element_type=F32)
        return s, kv_ref[rows, h * MLA_KVW + MLA_NOPE:(h + 1) * MLA_KVW]

    def update(h, s, v, first=False):
        m_cur = _lane_rowmax(s)
        if first:
            m_new = m_cur
        else:
            m_old = m_ref[h]
            m_new = jnp.maximum(m_old, m_cur)
            alpha = jnp.exp2(m_old - m_new)
        p, l_part = _lane_exp(s, m_new)
        pv = jnp.dot(p, v, preferred_element_type=F32)
        m_ref[h] = m_new
        l_ref[h] = l_part if first else alpha * l_ref[h] + l_part
        acc_ref[h] = pv if first else alpha * acc_ref[h] + pv

    r_id = lax.broadcasted_iota(jnp.int32, (tq, tk), 0)
    c_id = lax.broadcasted_iota(jnp.int32, (tq, tk), 1)
    for d in range(ratio):
        for h in range(nh):
            s, v = scores(qi * ratio + d, h)
            update(h, jnp.where(c_id + d * tk <= r_id, s, -jnp.inf), v, first=d == 0)

    def body(kb, carry):
        for h in range(nh):
            s, v = scores(kb, h)
            update(h, s, v)
        return carry

    lax.fori_loop(0, qi * ratio, body, 0)
    for h in range(nh):
        cols = slice(h * MLA_V, (h + 1) * MLA_V)
        l = jnp.sum(l_ref[h], axis=-1, keepdims=True)
        o_ref[:, cols] = (acc_ref[h] / l * _silu(mg_ref[:, cols].astype(F32))).astype(o_ref.dtype)


def _mla_prompt(q, kv, krp, slab, *, batch, seq, tq, tk):
    nq = seq // tq
    nh = MLA_HEADS
    mg_blk = _S_MG // (nh * MLA_V)
    return pl.pallas_call(
        functools.partial(_mla_prompt_kernel, tq=tq, tk=tk, nh=nh),
        grid=(batch, MLA_HEADS // nh, nq),
        in_specs=[
            pl.BlockSpec((tq, nh * MLA_QPAD), lambda b, hp, i: (b * nq + i, hp)),
            pl.BlockSpec((seq, nh * MLA_KVW), lambda b, hp, i: (b, hp)),
            pl.BlockSpec((seq, LANE), lambda b, hp, i: (b, 0)),
            pl.BlockSpec((tq, nh * MLA_V), lambda b, hp, i: (b * nq + i, mg_blk + hp)),
        ],
        out_specs=pl.BlockSpec((tq, nh * MLA_V), lambda b, hp, i: (b * nq + i, hp)),
        out_shape=jax.ShapeDtypeStruct((batch * seq, MLA_VW), BF16),
        scratch_shapes=[pltpu.VMEM((nh, tq, LANE), F32), pltpu.VMEM((nh, tq, LANE), F32),
                        pltpu.VMEM((nh, tq, MLA_V), F32)],
        compiler_params=_params(("arbitrary", "arbitrary", "arbitrary")),
        name="mla_prompt",
    )(q, kv, krp, slab)


def _headproj_kernel(x_ref, w_ref, *rest, gated):
    if gated:
        g_ref, o_ref = rest
    else:
        (o_ref,) = rest
    y = jnp.dot(x_ref[...], w_ref[0], preferred_element_type=F32)
    if gated:
        y = y * _silu(g_ref[...].astype(F32))
    o_ref[...] = y.astype(o_ref.dtype)


def _headproj(x, w, *, x_stride, gate=None, gate_blk0=0):
    m = x.shape[0]
    nh, kdim, n = w.shape
    in_specs = [pl.BlockSpec((m, kdim), lambda h: (0, h * x_stride)),
                pl.BlockSpec((1, kdim, n), lambda h: (h, 0, 0))]
    args = [x, w]
    if gate is not None:
        in_specs.append(pl.BlockSpec((m, n), lambda h: (0, gate_blk0 + h)))
        args.append(gate)
    return pl.pallas_call(
        functools.partial(_headproj_kernel, gated=gate is not None),
        grid=(nh,),
        in_specs=in_specs,
        out_specs=pl.BlockSpec((m, n), lambda h: (0, h)),
        out_shape=jax.ShapeDtypeStruct((m, nh * n), BF16),
        compiler_params=_params(("arbitrary",)),
        name="headproj",
    )(*args)


def _page_copies(pt_ref, cache_c, cache_rt, cbuf, rbuf, sem, idx, slot, *, g, n_steps, page):
    b = idx // n_steps
    first = (idx % n_steps) * g
    out = []
    for j in range(g):
        p = pt_ref[b, first + j]
        out.append(pltpu.make_async_copy(cache_c.at[p], cbuf.at[slot, pl.ds(j * page, page), :], sem.at[slot, 0]))
        out.append(pltpu.make_async_copy(cache_rt.at[p], rbuf.at[slot, j], sem.at[slot, 1]))
    return out


def _mla_sample_kernel(pt_ref, ql_ref, qr_ref, cn_ref, rn_ref, cache_c, cache_rt, o_ref,
                       cbuf, rbuf, sem, m_ref, l_ref, acc_ref, *, t, page, g, n_steps):
    i = pl.program_id(0)
    last = pl.num_programs(0) - 1
    step = i % n_steps
    slot = i % N_SLOTS
    copies = functools.partial(_page_copies, pt_ref, cache_c, cache_rt, cbuf, rbuf, sem,
                               g=g, n_steps=n_steps, page=page)

    @pl.when(i == 0)
    def _():
        for ahead in range(N_SLOTS - 1):
            for c in copies(jnp.minimum(ahead, last), ahead):
                c.start()

    ql = ql_ref[0]
    qr = qr_ref[0]
    rows = ql.shape[0]

    @pl.when(step == 0)
    def _():
        qlf = ql.astype(F32)
        qrf = qr.astype(F32)
        tok = lax.broadcasted_iota(jnp.int32, (rows, 1), 0) // (rows // t)
        cn = cn_ref[0].astype(BF16).astype(F32)
        rn = rn_ref[0].astype(BF16).astype(F32)
        s_cols = []
        for j in range(t):
            sj = (jnp.sum(qlf * cn[j:j + 1, :], axis=-1, keepdims=True)
                  + jnp.sum(qrf * rn[j:j + 1, :], axis=-1, keepdims=True))
            s_cols.append(jnp.where(tok >= j, sj, -jnp.inf))
        m0 = s_cols[0]
        for j in range(1, t):
            m0 = jnp.maximum(m0, s_cols[j])
        l0 = jnp.zeros((rows, 1), F32)
        a0 = jnp.zeros((rows, KV_LORA), F32)
        for j in range(t):
            pj = jnp.exp2(s_cols[j] - m0)
            l0 = l0 + pj
            a0 = a0 + pj.astype(BF16).astype(F32) * cn[j:j + 1, :]
        lane = lax.broadcasted_iota(jnp.int32, (rows, LANE), 1)
        m_ref[...] = jnp.broadcast_to(m0, (rows, LANE))
        l_ref[...] = jnp.where(lane == 0, l0, 0.0)
        acc_ref[...] = a0

    for c in copies(i, slot):
        c.wait()
    ppk = g // (KEY_GROUPS * KEY_SPLITS)
    sub = ppk * page
    m_run, l_run, acc_run = m_ref[...], l_ref[...], acc_ref[...]
    for grp in range(KEY_GROUPS):
        cbs, parts = [], []
        for k in range(grp * KEY_SPLITS, (grp + 1) * KEY_SPLITS):
            c32 = cbuf[slot, k * sub:(k + 1) * sub, :]
            rb = jnp.concatenate([rbuf[slot, k * ppk + j] for j in range(ppk)], axis=1).astype(BF16)
            cbs.append(c32.astype(BF16))
            ct = c32.T.astype(BF16)
            parts.append(jnp.dot(ql, ct, preferred_element_type=F32)
                         + jnp.dot(qr, rb, preferred_element_type=F32))
        s = jnp.concatenate(parts, axis=1)
        m_new = jnp.maximum(m_run, _lane_rowmax(s))
        alpha = jnp.exp2(m_run - m_new)
        pb, l_part = _lane_exp(s, m_new)
        pv = [jnp.dot(pb[:, k * sub:(k + 1) * sub], cbs[k], preferred_element_type=F32) for k in range(KEY_SPLITS)]
        while len(pv) > 1:
            pv = [pv[j] + pv[j + 1] for j in range(0, len(pv), 2)]
        m_run = m_new
        l_run = alpha * l_run + l_part
        acc_run = jnp.concatenate([alpha] * (KV_LORA // LANE), axis=1) * acc_run + pv[0]
    m_ref[...] = m_run
    l_ref[...] = l_run
    acc_ref[...] = acc_run

    ahead = N_SLOTS - 1
    for c in copies(jnp.minimum(i + ahead, last), (i + ahead) % N_SLOTS):
        c.start()

    @pl.when(step == n_steps - 1)
    def _():
        o_ref[0] = (acc_ref[...] / jnp.sum(l_ref[...], axis=-1, keepdims=True)).astype(o_ref.dtype)

    @pl.when(i == last)
    def _():
        for extra in range(1, N_SLOTS):
            for c in copies(last, (last + extra) % N_SLOTS):
                c.wait()


def _mla_sample(page_table, ql, qr, c_new, r_new, cache_c, cache_rt, *, t):
    batch, n_pages = page_table.shape
    page = cache_c.shape[1]
    g = PAGES_PER_STEP
    n_steps = n_pages // g
    rows = ql.shape[1]
    per_b = lambda i, pt: (i // n_steps, 0, 0)
    grid_spec = pltpu.PrefetchScalarGridSpec(
        num_scalar_prefetch=1,
        grid=(batch * n_steps,),
        in_specs=[
            pl.BlockSpec((1, rows, KV_LORA), per_b),
            pl.BlockSpec((1, rows, MLA_ROPE), per_b),
            pl.BlockSpec((1, t, KV_LORA), per_b),
            pl.BlockSpec((1, t, MLA_ROPE), per_b),
            pl.BlockSpec(memory_space=pl.ANY),
            pl.BlockSpec(memory_space=pl.ANY),
        ],
        out_specs=pl.BlockSpec((1, rows, KV_LORA), per_b),
        scratch_shapes=[
            pltpu.VMEM((N_SLOTS, g * page, KV_LORA), F32),
            pltpu.VMEM((N_SLOTS, g, MLA_ROPE, page), F32),
            pltpu.SemaphoreType.DMA((N_SLOTS, 2)),
            pltpu.VMEM((rows, LANE), F32), pltpu.VMEM((rows, LANE), F32), pltpu.VMEM((rows, KV_LORA), F32),
        ],
    )
    return pl.pallas_call(
        functools.partial(_mla_sample_kernel, t=t, page=page, g=g, n_steps=n_steps),
        grid_spec=grid_spec,
        out_shape=jax.ShapeDtypeStruct((batch, rows, KV_LORA), BF16),
        compiler_params=_params(("arbitrary",)),
        name="mla_sample",
    )(page_table, ql, qr, c_new, r_new, cache_c, cache_rt)


def _finish_kernel(x_ref, ar_ref, am_ref, ga_ref, gb_ref, ple_ref, wdr_ref, wdm_ref, wo_ref, wpg_ref, wpp_ref,
                   fw_ref, o_ref, *, final):
    h_ret = jnp.dot(ar_ref[...], wdr_ref[...], preferred_element_type=F32)
    h_mla = jnp.dot(am_ref[...], wdm_ref[...], preferred_element_type=F32)
    merged = (jax.nn.sigmoid(ga_ref[...].astype(F32)) * h_ret
              + jax.nn.sigmoid(gb_ref[...].astype(F32)) * h_mla)
    x2 = x_ref[...] + jnp.dot(merged.astype(BF16), wo_ref[...], preferred_element_type=F32)
    gate = jax.nn.sigmoid(jnp.dot(x2.astype(BF16), wpg_ref[...], preferred_element_type=F32))
    y = x2 + gate * jnp.dot(ple_ref[...].astype(BF16), wpp_ref[...], preferred_element_type=F32)
    if final:
        y = _rms(y, fw_ref[...])
    o_ref[...] = y


def _finish(x, act_ret, act_mla, slab, ple, wdr, wdm, wo, wpg, wpp, fw, *, tm, final):
    m, d = x.shape
    row = lambda i: (i, 0)
    fixed = lambda i: (0, 0)
    once = pl.Buffered(1)
    return pl.pallas_call(
        functools.partial(_finish_kernel, final=final),
        grid=(m // tm,),
        in_specs=[
            pl.BlockSpec((tm, d), row),
            pl.BlockSpec((tm, RET_VW), row),
            pl.BlockSpec((tm, MLA_VW), row),
            pl.BlockSpec((tm, d), lambda i: (i, _S_GA // D_MODEL)),
            pl.BlockSpec((tm, d), lambda i: (i, _S_GB // D_MODEL)),
            pl.BlockSpec((tm, PLE_DIM), row),
            pl.BlockSpec(wdr.shape, fixed, pipeline_mode=once),
            pl.BlockSpec(wdm.shape, fixed, pipeline_mode=once),
            pl.BlockSpec(wo.shape, fixed, pipeline_mode=once),
            pl.BlockSpec(wpg.shape, fixed, pipeline_mode=once),
            pl.BlockSpec(wpp.shape, fixed, pipeline_mode=once),
            pl.BlockSpec((1, d), fixed),
        ],
        out_specs=pl.BlockSpec((tm, d), row),
        out_shape=jax.ShapeDtypeStruct((m, d), F32),
        compiler_params=_params(("arbitrary",)),
        name="finish",
    )(x, act_ret, act_mla, slab, slab, ple, wdr, wdm, wo, wpg, wpp, fw)


def _rope_tables(pos, half):
    inv_freq = ROPE_BASE ** (-np.arange(half, dtype=np.float64) / half)
    ang = np.asarray(pos, np.float64)[:, None] * inv_freq[None, :]
    return np.cos(ang).astype(np.float32), np.sin(ang).astype(np.float32)


def _mla_rope_tables(pos):
    cos, sin = _rope_tables(pos, MLA_ROPE // 2)
    z = np.zeros((len(pos), LANE - MLA_ROPE), np.float32)
    return np.concatenate([cos, cos, z], axis=1), np.concatenate([sin, sin, z], axis=1)


def _decay_tables(c, rows):
    log_g = np.log1p(-np.exp2(-5.0 - np.arange(RET_HEADS, dtype=np.float64)))
    idx = np.arange(c, dtype=np.float64)
    diff = idx[:, None] - idx[None, :]
    dmask = np.where(diff >= 0, np.exp(np.maximum(diff, 0.0)[None] * log_g[:, None, None]), 0.0)
    qdec = np.exp((idx[None, :] + 1.0) * log_g[:, None])[:, :, None]
    kdec = np.exp((c - 1.0 - idx)[None, :] * log_g[:, None])[:, :, None]
    pad = rows - c
    dmask = np.pad(dmask, ((0, 0), (0, pad), (0, pad)))
    qdec = np.broadcast_to(np.pad(qdec, ((0, 0), (0, pad), (0, 0))), (RET_HEADS, rows, RET_DV))
    kdec = np.broadcast_to(np.pad(kdec, ((0, 0), (0, pad), (0, 0))), (RET_HEADS, rows, RET_DK))
    gcs = np.broadcast_to(np.exp(c * log_g)[:, None, None], (RET_HEADS, 1, RET_DV))
    return tuple(np.ascontiguousarray(a, np.float32) for a in (dmask, qdec, kdec, gcs))


def _rot_cols(w):
    half = w.shape[-1] // 2
    return jnp.concatenate([-w[..., half:], w[..., :half]], axis=-1)


def _layer_weights(w_in, w_uq, w_ukv, w_down_ret, w_down_mla, w_out, w_ple_gate, w_ple_proj):
    d = w_in.shape[0]
    src = [r for lo, hi in ((_OFF_RV, _OFF_CQ), (_OFF_MG, w_in.shape[1]), (_OFF_RQ, _OFF_RV))
           for r in range(lo, hi, WT_TILE)]
    n_main = len(src) * WT_TILE
    src += list(range(_OFF_CQ, _OFF_MG, WT_TILE))
    w_main = _transpose_weight(w_in.T, src)
    c_cols = w_main[:, n_main:n_main + _OFF_MG - _OFF_CQ]
    kr = c_cols[:, _OFF_KR - _OFF_CQ:]
    zpad = jnp.zeros((d, LANE - MLA_ROPE), BF16)
    w_c = jnp.concatenate([c_cols, zpad, _rot_cols(kr), zpad], axis=1)
    uq = w_uq.reshape(Q_LORA, MLA_HEADS, MLA_QK)
    uq_rope = uq[:, :, MLA_NOPE:]
    zq = jnp.zeros((Q_LORA, MLA_HEADS, MLA_QPAD - MLA_QK), F32)
    w_qa = jnp.concatenate([uq, zq], axis=2).reshape(Q_LORA, MLA_HEADS * MLA_QPAD).astype(BF16)
    w_qb = jnp.concatenate([_rot_cols(uq_rope), zq], axis=2).reshape(Q_LORA, MLA_HEADS * LANE).astype(BF16)
    ukv = w_ukv.reshape(KV_LORA, MLA_HEADS, MLA_KVW)
    w_uk_t = jnp.transpose(ukv[:, :, :MLA_NOPE], (1, 2, 0)).astype(BF16)
    w_uv = jnp.transpose(ukv[:, :, MLA_NOPE:], (1, 0, 2)).astype(BF16)
    return dict(w_main=w_main, w_c=w_c, w_qa=w_qa, w_qb=w_qb, w_kv=w_ukv.astype(BF16),
                w_uk_t=w_uk_t, w_uv=w_uv, wdr=w_down_ret.astype(BF16), wdm=w_down_mla.astype(BF16),
                wo=w_out.astype(BF16), wpg=w_ple_gate.astype(BF16), wpp=w_ple_proj.astype(BF16))


def _row_tile(m, cap):
    t = min(m, cap)
    assert m % t == 0
    return t


def _project(x, ln_w, lw, cos_r, sin_r):
    tm = _row_tile(cos_r.shape[0], 1024)
    assert x.shape[0] % tm == 0
    qk, xn = _inproj_rope(x, ln_w, lw["w_main"], cos_r, sin_r, col0=_S_W, n=2 * RET_QK, tm=tm, tn=1024,
                          n_q=RET_QK // 1024, k_scale=RET_DK ** -0.5)
    slab = _inproj(xn, lw["w_main"], col0=0, n=_S_W, tm=tm, tn=SLAB_TN)
    return qk, slab, xn


def kernel(x_prompt, x_sample, cache_ckv, cache_krope, state_ret, page_table, p_prompt, p_sample, ln_w, w_in,
           q_norm_w, w_uq, kv_norm_w, w_ukv, ret_gn_w, w_down_ret, w_down_mla, w_out, w_ple_gate, w_ple_proj,
           final_norm_w):
    depth = w_in.shape[0]
    b, s, d = x_prompt.shape
    bd, t, _ = x_sample.shape
    n_pages = page_table.shape[1]
    page = cache_ckv.shape[2]
    past_len = n_pages * page
    assert s % RET_CHUNK == 0 and n_pages % PAGES_PER_STEP == 0

    pos_p = np.arange(s)
    pos_s = np.tile(past_len + np.arange(t), bd)
    cos_rp, sin_rp = _rope_tables(pos_p, RET_DK // 2)
    cos_rs, sin_rs = _rope_tables(pos_s, RET_DK // 2)
    cos_mp, sin_mp = _mla_rope_tables(pos_p)
    cos_ms, sin_ms = _mla_rope_tables(pos_s)
    dec_p = _decay_tables(RET_CHUNK, RET_CHUNK)
    dec_s = _decay_tables(t, RET_CHUNK)

    y_p = x_prompt.reshape(b * s, d)
    y_s = x_sample.reshape(bd * t, d)
    tq = _row_tile(s, 512)
    tk = tq
    outs = [[] for _ in range(6)]
    for i in range(depth):
        lw = _layer_weights(w_in[i], w_uq[i], w_ukv[i], w_down_ret[i], w_down_mla[i], w_out[i], w_ple_gate[i],
                            w_ple_proj[i])
        lnw = ln_w[i][None, :]
        qnw = q_norm_w[i][None, :]
        kvnw = kv_norm_w[i][None, :]
        gnw = ret_gn_w[i][None, :]
        fw = final_norm_w[None, :]
        final = i == depth - 1

        qk, slab, xn = _project(y_p, lnw, lw, cos_rp, sin_rp)
        q, ckv, kr, krp, kv = _mla_prep(xn, lw["w_c"], qnw, lw["w_qa"], lw["w_qb"], kvnw, cos_mp, sin_mp,
                                        lw["w_kv"], tm=_row_tile(s, 512), with_kv=True)
        act_ret, s_fin = _ret_prompt(qk, slab, gnw, *dec_p, batch=b, seq=s)
        act_mla = _mla_prompt(q, kv, krp, slab, batch=b, seq=s, tq=tq, tk=tk)
        y_p = _finish(y_p, act_ret, act_mla, slab, p_prompt[i].reshape(b * s, -1), lw["wdr"], lw["wdm"], lw["wo"],
                      lw["wpg"], lw["wpp"], fw, tm=_row_tile(b * s, 512), final=final)
        outs[0].append(ckv.reshape(b, s, KV_LORA))
        outs[1].append(kr.reshape(b, s, MLA_ROPE))
        outs[2].append(s_fin.astype(x_prompt.dtype))

        ms = bd * t
        qk, slab, xn = _project(y_s, lnw, lw, cos_rs, sin_rs)
        q, ckv, kr = _mla_prep(xn, lw["w_c"], qnw, lw["w_qa"], lw["w_qb"], kvnw, cos_ms, sin_ms, None,
                               tm=ms, with_kv=False)
        act_ret, s_new = _ret_sample(qk, slab, gnw, state_ret[i], *dec_s, batch=bd, t=t)
        q_lat = _headproj(q, lw["w_uk_t"], x_stride=MLA_QPAD // LANE)
        q_lat = q_lat.reshape(bd, t * MLA_HEADS, KV_LORA)
        q_rope = q.reshape(bd, t * MLA_HEADS, MLA_QPAD)[..., MLA_NOPE:MLA_QK]
        cache_rt = jnp.swapaxes(cache_krope[i], 1, 2)
        o_lat = _mla_sample(page_table, q_lat, q_rope, ckv.reshape(bd, t, KV_LORA), kr.reshape(bd, t, MLA_ROPE),
                            cache_ckv[i], cache_rt, t=t)
        o_lat = o_lat.reshape(ms, MLA_HEADS * KV_LORA)
        act_mla = _headproj(o_lat, lw["w_uv"], x_stride=1, gate=slab, gate_blk0=_S_MG // MLA_V)
        y_s = _finish(y_s, act_ret, act_mla, slab, p_sample[i].reshape(ms, -1), lw["wdr"], lw["wdm"], lw["wo"],
                      lw["wpg"], lw["wpp"], fw, tm=_row_tile(ms, 512), final=final)
        outs[3].append(ckv.reshape(bd, t, KV_LORA))
        outs[4].append(kr.reshape(bd, t, MLA_ROPE))
        outs[5].append(s_new.astype(state_ret.dtype))

    y_prompt = y_p.reshape(b, s, d)
    y_sample = y_s.reshape(bd, t, d)
    ckv_p, kr_p, ret_p, ckv_s, kr_s, ret_s = [jnp.stack(o) for o in outs]
    return (y_prompt, y_sample, ckv_p, kr_p, ret_p, ckv_s, kr_s, ret_s)
```

```python
import functools
import math

import numpy as np
import jax
import jax.numpy as jnp
from jax import lax
from jax.experimental import pallas as pl
from jax.experimental.pallas import tpu as pltpu

F32 = jnp.float32
BF16 = jnp.bfloat16

D_MODEL = 1024
PLE_DIM = 256
RET_HEADS = 4
RET_DK = 256
RET_DV = 512
RET_CHUNK = 128
MLA_HEADS = 8
MLA_NOPE = 128
MLA_ROPE = 64
MLA_V = 128
Q_LORA = 384
KV_LORA = 256
ROPE_BASE = 10000.0
EPS = 1e-6

RET_QK = RET_HEADS * RET_DK
RET_VW = RET_HEADS * RET_DV
MLA_QK = MLA_NOPE + MLA_ROPE
MLA_VW = MLA_HEADS * MLA_V
MLA_QPAD = 256
MLA_KVW = MLA_NOPE + MLA_V
Q_SCALE = MLA_QK ** -0.5 * math.log2(math.e)
LANE = 128
PAGES_PER_STEP = 32
N_SLOTS = 3
KEY_GROUPS = 2
KEY_SPLITS = 2
HEADS_PER_STEP = 2
SLAB_TN = 1792
WT_TILE = 512
WT_ROWS = 64
RET_SAMPLE_SEQS = 4
VMEM_LIMIT = 48 * 1024 * 1024

_OFF_RQ = 0
_OFF_RK = _OFF_RQ + RET_QK
_OFF_RV = _OFF_RK + RET_QK
_OFF_RG = _OFF_RV + RET_VW
_OFF_CQ = _OFF_RG + RET_VW
_OFF_CKV = _OFF_CQ + Q_LORA
_OFF_KR = _OFF_CKV + KV_LORA
_OFF_MG = _OFF_KR + MLA_ROPE
_OFF_GA = _OFF_MG + MLA_VW
_OFF_GB = _OFF_GA + D_MODEL
_S_RV = 0
_S_RG = _S_RV + RET_VW
_S_MG = _S_RG + RET_VW
_S_GA = _S_MG + MLA_VW
_S_GB = _S_GA + D_MODEL
_S_W = _S_GB + D_MODEL

_NT = (((1,), (1,)), ((), ()))
_TN = (((0,), (0,)), ((), ()))


def _params(sem):
    return pltpu.CompilerParams(dimension_semantics=sem, vmem_limit_bytes=VMEM_LIMIT)


def _rms(x, w):
    return x * lax.rsqrt(jnp.mean(x * x, axis=-1, keepdims=True) + EPS) * w


def _silu(x):
    return x * jax.nn.sigmoid(x)


def _lane_blocks(s):
    return [s[:, c * LANE:(c + 1) * LANE] for c in range(s.shape[1] // LANE)]


def _lane_rowmax(s):
    blocks = _lane_blocks(s)
    mx = blocks[0]
    for blk in blocks[1:]:
        mx = jnp.maximum(mx, blk)
    return jnp.broadcast_to(jnp.max(mx, axis=-1, keepdims=True), mx.shape)


def _lane_exp(s, m):
    ps = [jnp.exp2(blk - m) for blk in _lane_blocks(s)]
    part = ps[0]
    for p in ps[1:]:
        part = part + p
    return jnp.concatenate([p.astype(BF16) for p in ps], axis=1), part


def _matmul_kernel(x_ref, w_ref, o_ref):
    o_ref[...] = jnp.dot(x_ref[...], w_ref[...], preferred_element_type=F32).astype(o_ref.dtype)


def _inproj_rope_kernel(x_ref, lnw_ref, w_ref, cos_ref, sin_ref, o_ref, xn_ref, *, n_q, k_scale, tn):
    j = pl.program_id(1)

    @pl.when(j == 0)
    def _():
        xn_ref[...] = _rms(x_ref[...], lnw_ref[...]).astype(BF16)

    acc = jnp.dot(xn_ref[...], w_ref[...], preferred_element_type=F32)
    scale = jnp.where(j >= n_q, k_scale, 1.0).astype(F32)
    cos = cos_ref[...] * scale
    sin = sin_ref[...] * scale
    for h in range(tn // RET_DK):
        lo = h * RET_DK
        a1 = acc[:, lo:lo + LANE]
        a2 = acc[:, lo + LANE:lo + RET_DK]
        o_ref[:, lo:lo + LANE] = (a1 * cos - a2 * sin).astype(o_ref.dtype)
        o_ref[:, lo + LANE:lo + RET_DK] = (a2 * cos + a1 * sin).astype(o_ref.dtype)


def _inproj_rope(x, ln_w, w, cos, sin, *, col0, n, tm, tn, n_q, k_scale):
    m, d = x.shape
    blk0 = col0 // tn
    p_tiles = cos.shape[0] // tm
    return pl.pallas_call(
        functools.partial(_inproj_rope_kernel, n_q=n_q, k_scale=k_scale, tn=tn),
        grid=(m // tm, n // tn),
        in_specs=[
            pl.BlockSpec((tm, d), lambda i, j: (i, 0)),
            pl.BlockSpec((1, d), lambda i, j: (0, 0)),
            pl.BlockSpec((d, tn), lambda i, j: (0, blk0 + j)),
            pl.BlockSpec((tm, LANE), lambda i, j: (i % p_tiles, 0)),
            pl.BlockSpec((tm, LANE), lambda i, j: (i % p_tiles, 0)),
        ],
        out_specs=[pl.BlockSpec((tm, tn), lambda i, j: (i, j)), pl.BlockSpec((tm, d), lambda i, j: (i, 0))],
        out_shape=[jax.ShapeDtypeStruct((m, n), BF16), jax.ShapeDtypeStruct((m, d), BF16)],
        compiler_params=_params(("arbitrary", "arbitrary")),
        name="inproj_rope",
    )(x, ln_w, w, cos, sin)


def _inproj(xn, w, *, col0, n, tm, tn):
    m, d = xn.shape
    blk0 = col0 // tn
    return pl.pallas_call(
        _matmul_kernel,
        grid=(m // tm, n // tn),
        in_specs=[pl.BlockSpec((tm, d), lambda i, j: (i, 0)), pl.BlockSpec((d, tn), lambda i, j: (0, blk0 + j))],
        out_specs=pl.BlockSpec((tm, tn), lambda i, j: (i, j)),
        out_shape=jax.ShapeDtypeStruct((m, n), BF16),
        compiler_params=_params(("arbitrary", "arbitrary")),
        name="inproj",
    )(xn, w)


def _wt_kernel(tbl_ref, *refs):
    del tbl_ref
    o_ref = refs[-1]
    rows = jnp.concatenate([r[...] for r in refs[:-1]], axis=0)
    o_ref[...] = rows.T.astype(BF16)


def _transpose_weight(w_t, src_rows):
    d = w_t.shape[1]
    parts = WT_TILE // WT_ROWS
    assert all(r % WT_ROWS == 0 for r in src_rows)
    tbl = np.asarray([r // WT_ROWS for r in src_rows], np.int32)
    grid_spec = pltpu.PrefetchScalarGridSpec(
        num_scalar_prefetch=1,
        grid=(len(src_rows),),
        in_specs=[pl.BlockSpec((WT_ROWS, d), lambda j, tbl, r=r: (tbl[j] + r, 0)) for r in range(parts)],
        out_specs=pl.BlockSpec((d, WT_TILE), lambda j, tbl: (0, j)),
    )
    return pl.pallas_call(
        _wt_kernel,
        grid_spec=grid_spec,
        out_shape=jax.ShapeDtypeStruct((d, len(src_rows) * WT_TILE), BF16),
        compiler_params=_params(("arbitrary",)),
        name="wt",
    )(tbl, *([w_t] * parts))


def _mla_prep_kernel(xn_ref, wc_ref, qnw_ref, wqa_ref, wqb_ref, kvnw_ref, cos_ref, sin_ref, *rest,
                     with_kv):
    if with_kv:
        wkv_ref, q_ref, ckv_ref, kr_ref, krp_ref, kv_ref = rest
    else:
        q_ref, ckv_ref, kr_ref = rest
    zc = jnp.dot(xn_ref[...], wc_ref[...], preferred_element_type=F32)
    cos = cos_ref[...]
    sin = sin_ref[...]
    cqn = _rms(zc[:, :Q_LORA], qnw_ref[...]).astype(BF16)
    qa = jnp.dot(cqn, wqa_ref[...], preferred_element_type=F32)
    qb = jnp.dot(cqn, wqb_ref[...], preferred_element_type=F32)
    qcos = cos * Q_SCALE
    qsin = sin * Q_SCALE
    for h in range(MLA_HEADS):
        lo = h * MLA_QPAD
        q_ref[:, lo:lo + LANE] = (qa[:, lo:lo + LANE] * Q_SCALE).astype(q_ref.dtype)
        q_ref[:, lo + LANE:lo + MLA_QPAD] = (
            qa[:, lo + LANE:lo + MLA_QPAD] * qcos + qb[:, h * LANE:(h + 1) * LANE] * qsin).astype(q_ref.dtype)
    c0 = Q_LORA
    ckvn = _rms(zc[:, c0:c0 + KV_LORA], kvnw_ref[...])
    ckv_ref[...] = ckvn
    k0 = c0 + KV_LORA
    krp = zc[:, k0:k0 + LANE] * cos + zc[:, k0 + LANE:k0 + 2 * LANE] * sin
    kr_ref[...] = krp[:, :MLA_ROPE]
    if with_kv:
        krp_ref[...] = krp.astype(BF16)
        kv_ref[...] = jnp.dot(ckvn.astype(BF16), wkv_ref[...], preferred_element_type=F32).astype(BF16)


def _mla_prep(xn, wc, qnw, wqa, wqb, kvnw, cos, sin, wkv, *, tm, with_kv):
    m, d = xn.shape
    p_tiles = cos.shape[0] // tm
    row = lambda i: (i, 0)
    fixed = lambda i: (0, 0)
    in_specs = [
        pl.BlockSpec((tm, d), row),
        pl.BlockSpec(wc.shape, fixed),
        pl.BlockSpec((1, Q_LORA), fixed),
        pl.BlockSpec(wqa.shape, fixed),
        pl.BlockSpec(wqb.shape, fixed),
        pl.BlockSpec((1, KV_LORA), fixed),
        pl.BlockSpec((tm, LANE), lambda i: (i % p_tiles, 0)),
        pl.BlockSpec((tm, LANE), lambda i: (i % p_tiles, 0)),
    ]
    args = [xn, wc, qnw, wqa, wqb, kvnw, cos, sin]
    qw = MLA_HEADS * MLA_QPAD
    out_specs = [pl.BlockSpec((tm, qw), row), pl.BlockSpec((tm, KV_LORA), row), pl.BlockSpec((tm, MLA_ROPE), row)]
    out_shape = [jax.ShapeDtypeStruct((m, qw), BF16), jax.ShapeDtypeStruct((m, KV_LORA), F32),
                 jax.ShapeDtypeStruct((m, MLA_ROPE), F32)]
    if with_kv:
        in_specs.append(pl.BlockSpec(wkv.shape, fixed))
        args.append(wkv)
        out_specs += [pl.BlockSpec((tm, LANE), row), pl.BlockSpec((tm, wkv.shape[1]), row)]
        out_shape += [jax.ShapeDtypeStruct((m, LANE), BF16), jax.ShapeDtypeStruct((m, wkv.shape[1]), BF16)]
    return pl.pallas_call(
        functools.partial(_mla_prep_kernel, with_kv=with_kv),
        grid=(m // tm,),
        in_specs=in_specs,
        out_specs=out_specs,
        out_shape=out_shape,
        compiler_params=_params(("arbitrary",)),
        name="mla_prep",
    )(*args)


def _ret_chunk(q, k, v, s_prev, dmask, qdec, kdec, gc):
    sc = lax.dot_general(q, k, _NT, preferred_element_type=F32) * dmask
    intra = jnp.dot(sc.astype(BF16), v, preferred_element_type=F32)
    cross = jnp.dot(q, s_prev.astype(BF16), preferred_element_type=F32) * qdec
    kd = (k.astype(F32) * kdec).astype(BF16)
    s_new = gc * s_prev + lax.dot_general(kd, v, _TN, preferred_element_type=F32)
    return intra + cross, s_new


def _ret_act(o, rg, gnw):
    mu = jnp.mean(o, axis=-1, keepdims=True)
    d = o - mu
    var = jnp.mean(d * d, axis=-1, keepdims=True)
    return d * lax.rsqrt(var + EPS) * gnw * _silu(rg.astype(F32))


def _ret_prompt_kernel(q_ref, k_ref, v_ref, rg_ref, gnw_ref, dm_ref, qd_ref, kd_ref, gc_ref,
                       o_ref, sfin_ref, s_ref, *, n_chunks, nh):
    s_ref[...] = jnp.zeros_like(s_ref)

    def chunk(c, carry):
        rows = pl.ds(pl.multiple_of(c * RET_CHUNK, RET_CHUNK), RET_CHUNK)
        for h in range(nh):
            qk_cols = slice(h * RET_DK, (h + 1) * RET_DK)
            v_cols = slice(h * RET_DV, (h + 1) * RET_DV)
            o, s_new = _ret_chunk(q_ref[rows, qk_cols], k_ref[rows, qk_cols], v_ref[rows, v_cols], s_ref[h],
                                  dm_ref[h], qd_ref[h], kd_ref[h], gc_ref[h])
            s_ref[h] = s_new
            o_ref[rows, v_cols] = _ret_act(o, rg_ref[rows, v_cols], gnw_ref[:, v_cols]).astype(o_ref.dtype)
        return carry

    lax.fori_loop(0, n_chunks, chunk, 0, unroll=4)
    sfin_ref[0] = s_ref[...]


def _ret_prompt(qk, slab, gn_w, dmask, qdec, kdec, gcs, *, batch, seq):
    nh = HEADS_PER_STEP
    qw, vw = nh * RET_DK, nh * RET_DV
    k_blk = RET_QK // qw
    rg_blk = _S_RG // vw
    return pl.pallas_call(
        functools.partial(_ret_prompt_kernel, n_chunks=seq // RET_CHUNK, nh=nh),
        grid=(batch, RET_HEADS // nh),
        in_specs=[
            pl.BlockSpec((seq, qw), lambda b, hp: (b, hp)),
            pl.BlockSpec((seq, qw), lambda b, hp: (b, k_blk + hp)),
            pl.BlockSpec((seq, vw), lambda b, hp: (b, hp)),
            pl.BlockSpec((seq, vw), lambda b, hp: (b, rg_blk + hp)),
            pl.BlockSpec((1, vw), lambda b, hp: (0, hp)),
            pl.BlockSpec((nh, RET_CHUNK, RET_CHUNK), lambda b, hp: (hp, 0, 0)),
            pl.BlockSpec((nh, RET_CHUNK, RET_DV), lambda b, hp: (hp, 0, 0)),
            pl.BlockSpec((nh, RET_CHUNK, RET_DK), lambda b, hp: (hp, 0, 0)),
            pl.BlockSpec((nh, 1, RET_DV), lambda b, hp: (hp, 0, 0)),
        ],
        out_specs=[
            pl.BlockSpec((seq, vw), lambda b, hp: (b, hp)),
            pl.BlockSpec((1, nh, RET_DK, RET_DV), lambda b, hp: (b, hp, 0, 0)),
        ],
        out_shape=[
            jax.ShapeDtypeStruct((batch * seq, RET_VW), BF16),
            jax.ShapeDtypeStruct((batch, RET_HEADS, RET_DK, RET_DV), F32),
        ],
        scratch_shapes=[pltpu.VMEM((nh, RET_DK, RET_DV), F32)],
        compiler_params=_params(("arbitrary", "arbitrary")),
        name="ret_prompt",
    )(qk, qk, slab, slab, gn_w, dmask, qdec, kdec, gcs)


def _ret_sample_kernel(qk_ref, vg_ref, gnw_ref, st_ref, dm_ref, qd_ref, kd_ref, gc_ref,
                       o_ref, snew_ref, qp_ref, kp_ref, vp_ref, act_ref, *, t, nb):
    qp_ref[...] = jnp.zeros_like(qp_ref)
    kp_ref[...] = jnp.zeros_like(kp_ref)
    vp_ref[...] = jnp.zeros_like(vp_ref)
    qk = qk_ref[...].astype(F32)
    vg = vg_ref[...].astype(F32)
    for bb in range(nb):
        rows = slice(bb * t, (bb + 1) * t)
        for h in range(RET_HEADS):
            qp_ref[0:t, :] = qk[rows, h * RET_DK:(h + 1) * RET_DK]
            kp_ref[0:t, :] = qk[rows, RET_QK + h * RET_DK:RET_QK + (h + 1) * RET_DK]
            vp_ref[0:t, :] = vg[rows, _S_RV + h * RET_DV:_S_RV + (h + 1) * RET_DV]
            o, s_new = _ret_chunk(qp_ref[...].astype(BF16), kp_ref[...].astype(BF16), vp_ref[...].astype(BF16),
                                  st_ref[bb, h], dm_ref[h], qd_ref[h], kd_ref[h], gc_ref[h])
            snew_ref[bb, h] = s_new
            act_ref[rows, h * RET_DV:(h + 1) * RET_DV] = _ret_act(
                o[0:t, :], vg[rows, _S_RG + h * RET_DV:_S_RG + (h + 1) * RET_DV],
                gnw_ref[:, h * RET_DV:(h + 1) * RET_DV])
    o_ref[...] = act_ref[...].astype(o_ref.dtype)


def _ret_sample(qk, slab, gn_w, state, dmask, qdec, kdec, gcs, *, batch, t):
    h = RET_HEADS
    whole = lambda b: (0, 0, 0)
    nb = _row_tile(batch, RET_SAMPLE_SEQS)
    return pl.pallas_call(
        functools.partial(_ret_sample_kernel, t=t, nb=nb),
        grid=(batch // nb,),
        in_specs=[
            pl.BlockSpec((nb * t, qk.shape[1]), lambda b: (b, 0)),
            pl.BlockSpec((nb * t, _S_MG), lambda b: (b, 0)),
            pl.BlockSpec((1, RET_VW), lambda b: (0, 0)),
            pl.BlockSpec((nb, h, RET_DK, RET_DV), lambda b: (b, 0, 0, 0)),
            pl.BlockSpec(dmask.shape, whole),
            pl.BlockSpec(qdec.shape, whole),
            pl.BlockSpec(kdec.shape, whole),
            pl.BlockSpec(gcs.shape, whole),
        ],
        out_specs=[
            pl.BlockSpec((nb * t, RET_VW), lambda b: (b, 0)),
            pl.BlockSpec((nb, h, RET_DK, RET_DV), lambda b: (b, 0, 0, 0)),
        ],
        out_shape=[
            jax.ShapeDtypeStruct((batch * t, RET_VW), BF16),
            jax.ShapeDtypeStruct(state.shape, state.dtype),
        ],
        scratch_shapes=[pltpu.VMEM((RET_CHUNK, RET_DK), F32), pltpu.VMEM((RET_CHUNK, RET_DK), F32),
                        pltpu.VMEM((RET_CHUNK, RET_DV), F32), pltpu.VMEM((nb * t, RET_VW), F32)],
        compiler_params=_params(("arbitrary",)),
        name="ret_sample",
    )(qk, slab, gn_w, state, dmask, qdec, kdec, gcs)


def _mla_prompt_kernel(q_ref, kv_ref, krp_ref, mg_ref, o_ref, m_ref, l_ref, acc_ref, *, tq, tk, nh):
    qi = pl.program_id(2)
    ratio = tq // tk

    def scores(kb, h):
        rows = pl.ds(pl.multiple_of(kb * tk, tk), tk)
        k = jnp.concatenate([kv_ref[rows, h * MLA_KVW:h * MLA_KVW + MLA_NOPE], krp_ref[rows, :]], axis=1)
        s = lax.dot_general(q_ref[:, h * MLA_QPAD:(h + 1) * MLA_QPAD], k, _NT, preferred_element_type=F32)
        return s, kv_ref[rows, h * MLA_KVW + MLA_NOPE:(h + 1) * MLA_KVW]

    def update(h, s, v, first=False):
        m_cur = _lane_rowmax(s)
        if first:
            m_new = m_cur
        else:
            m_old = m_ref[h]
            m_new = jnp.maximum(m_old, m_cur)
            alpha = jnp.exp2(m_old - m_new)
        p, l_part = _lane_exp(s, m_new)
        pv = jnp.dot(p, v, preferred_element_type=F32)
        m_ref[h] = m_new
        l_ref[h] = l_part if first else alpha * l_ref[h] + l_part
        acc_ref[h] = pv if first else alpha * acc_ref[h] + pv

    r_id = lax.broadcasted_iota(jnp.int32, (tq, tk), 0)
    c_id = lax.broadcasted_iota(jnp.int32, (tq, tk), 1)
    for d in range(ratio):
        for h in range(nh):
            s, v = scores(qi * ratio + d, h)
            update(h, jnp.where(c_id + d * tk <= r_id, s, -jnp.inf), v, first=d == 0)

    def body(kb, carry):
        for h in range(nh):
            s, v = scores(kb, h)
            update(h, s, v)
        return carry

    lax.fori_loop(0, qi * ratio, body, 0)
    for h in range(nh):
        cols = slice(h * MLA_V, (h + 1) * MLA_V)
        l = jnp.sum(l_ref[h], axis=-1, keepdims=True)
        o_ref[:, cols] = (acc_ref[h] / l * _silu(mg_ref[:, cols].astype(F32))).astype(o_ref.dtype)


def _mla_prompt(q, kv, krp, slab, *, batch, seq, tq, tk):
    nq = seq // tq
    nh = MLA_HEADS
    mg_blk = _S_MG // (nh * MLA_V)
    return pl.pallas_call(
        functools.partial(_mla_prompt_kernel, tq=tq, tk=tk, nh=nh),
        grid=(batch, MLA_HEADS // nh, nq),
        in_specs=[
            pl.BlockSpec((tq, nh * MLA_QPAD), lambda b, hp, i: (b * nq + i, hp)),
            pl.BlockSpec((seq, nh * MLA_KVW), lambda b, hp, i: (b, hp)),
            pl.BlockSpec((seq, LANE), lambda b, hp, i: (b, 0)),
            pl.BlockSpec((tq, nh * MLA_V), lambda b, hp, i: (b * nq + i, mg_blk + hp)),
        ],
        out_specs=pl.BlockSpec((tq, nh * MLA_V), lambda b, hp, i: (b * nq + i, hp)),
        out_shape=jax.ShapeDtypeStruct((batch * seq, MLA_VW), BF16),
        scratch_shapes=[pltpu.VMEM((nh, tq, LANE), F32), pltpu.VMEM((nh, tq, LANE), F32),
                        pltpu.VMEM((nh, tq, MLA_V), F32)],
        compiler_params=_params(("arbitrary", "arbitrary", "arbitrary")),
        name="mla_prompt",
    )(q, kv, krp, slab)


def _headproj_kernel(x_ref, w_ref, *rest, gated):
    if gated:
        g_ref, o_ref = rest
    else:
        (o_ref,) = rest
    y = jnp.dot(x_ref[...], w_ref[0], preferred_element_type=F32)
    if gated:
        y = y * _silu(g_ref[...].astype(F32))
    o_ref[...] = y.astype(o_ref.dtype)


def _headproj(x, w, *, x_stride, gate=None, gate_blk0=0):
    m = x.shape[0]
    nh, kdim, n = w.shape
    in_specs = [pl.BlockSpec((m, kdim), lambda h: (0, h * x_stride)),
                pl.BlockSpec((1, kdim, n), lambda h: (h, 0, 0))]
    args = [x, w]
    if gate is not None:
        in_specs.append(pl.BlockSpec((m, n), lambda h: (0, gate_blk0 + h)))
        args.append(gate)
    return pl.pallas_call(
        functools.partial(_headproj_kernel, gated=gate is not None),
        grid=(nh,),
        in_specs=in_specs,
        out_specs=pl.BlockSpec((m, n), lambda h: (0, h)),
        out_shape=jax.ShapeDtypeStruct((m, nh * n), BF16),
        compiler_params=_params(("arbitrary",)),
        name="headproj",
    )(*args)


def _page_copies(pt_ref, cache_c, cache_rt, cbuf, rbuf, sem, idx, slot, *, g, n_steps, page):
    b = idx // n_steps
    first = (idx % n_steps) * g
    out = []
    for j in range(g):
        p = pt_ref[b, first + j]
        out.append(pltpu.make_async_copy(cache_c.at[p], cbuf.at[slot, pl.ds(j * page, page), :], sem.at[slot, 0]))
        out.append(pltpu.make_async_copy(cache_rt.at[p], rbuf.at[slot, j], sem.at[slot, 1]))
    return out


def _start_all(copies):
    for n, c in enumerate(copies):
        c.start(priority=(n // 2) % 2)


def _mla_sample_kernel(pt_ref, ql_ref, qr_ref, cn_ref, rn_ref, cache_c, cache_rt, o_ref,
                       cbuf, rbuf, sem, m_ref, l_ref, acc_ref, *, t, page, g, n_steps):
    i = pl.program_id(0)
    last = pl.num_programs(0) - 1
    step = i % n_steps
    slot = i % N_SLOTS
    copies = functools.partial(_page_copies, pt_ref, cache_c, cache_rt, cbuf, rbuf, sem,
                               g=g, n_steps=n_steps, page=page)

    @pl.when(i == 0)
    def _():
        for ahead in range(N_SLOTS - 1):
            _start_all(copies(jnp.minimum(ahead, last), ahead))

    ql = ql_ref[0]
    qr = qr_ref[0]
    rows = ql.shape[0]

    @pl.when(step == 0)
    def _():
        qlf = ql.astype(F32)
        qrf = qr.astype(F32)
        tok = lax.broadcasted_iota(jnp.int32, (rows, 1), 0) // (rows // t)
        cn = cn_ref[0].astype(BF16).astype(F32)
        rn = rn_ref[0].astype(BF16).astype(F32)
        s_cols = []
        for j in range(t):
            sj = (jnp.sum(qlf * cn[j:j + 1, :], axis=-1, keepdims=True)
                  + jnp.sum(qrf * rn[j:j + 1, :], axis=-1, keepdims=True))
            s_cols.append(jnp.where(tok >= j, sj, -jnp.inf))
        m0 = s_cols[0]
        for j in range(1, t):
            m0 = jnp.maximum(m0, s_cols[j])
        l0 = jnp.zeros((rows, 1), F32)
        a0 = jnp.zeros((rows, KV_LORA), F32)
        for j in range(t):
            pj = jnp.exp2(s_cols[j] - m0)
            l0 = l0 + pj
            a0 = a0 + pj.astype(BF16).astype(F32) * cn[j:j + 1, :]
        lane = lax.broadcasted_iota(jnp.int32, (rows, LANE), 1)
        m_ref[...] = jnp.broadcast_to(m0, (rows, LANE))
        l_ref[...] = jnp.where(lane == 0, l0, 0.0)
        acc_ref[...] = a0

    for c in copies(i, slot):
        c.wait()
    ppk = g // (KEY_GROUPS * KEY_SPLITS)
    sub = ppk * page
    m_run, l_run, acc_run = m_ref[...], l_ref[...], acc_ref[...]
    for grp in range(KEY_GROUPS):
        cbs, parts = [], []
        for k in range(grp * KEY_SPLITS, (grp + 1) * KEY_SPLITS):
            c32 = cbuf[slot, k * sub:(k + 1) * sub, :]
            rb = jnp.concatenate([rbuf[slot, k * ppk + j] for j in range(ppk)], axis=1).astype(BF16)
            cbs.append(c32.astype(BF16))
            ct = c32.T.astype(BF16)
            parts.append(jnp.dot(ql, ct, preferred_element_type=F32)
                         + jnp.dot(qr, rb, preferred_element_type=F32))
        s = jnp.concatenate(parts, axis=1)
        m_new = jnp.maximum(m_run, _lane_rowmax(s))
        alpha = jnp.exp2(m_run - m_new)
        pb, l_part = _lane_exp(s, m_new)
        pv = [jnp.dot(pb[:, k * sub:(k + 1) * sub], cbs[k], preferred_element_type=F32) for k in range(KEY_SPLITS)]
        while len(pv) > 1:
            pv = [pv[j] + pv[j + 1] for j in range(0, len(pv), 2)]
        m_run = m_new
        l_run = alpha * l_run + l_part
        acc_run = jnp.concatenate([alpha] * (KV_LORA // LANE), axis=1) * acc_run + pv[0]
    m_ref[...] = m_run
    l_ref[...] = l_run
    acc_ref[...] = acc_run

    ahead = N_SLOTS - 1
    _start_all(copies(jnp.minimum(i + ahead, last), (i + ahead) % N_SLOTS))

    @pl.when(step == n_steps - 1)
    def _():
        o_ref[0] = (acc_ref[...] / jnp.sum(l_ref[...], axis=-1, keepdims=True)).astype(o_ref.dtype)

    @pl.when(i == last)
    def _():
        for extra in range(1, N_SLOTS):
            for c in copies(last, (last + extra) % N_SLOTS):
                c.wait()


def _mla_sample(page_table, ql, qr, c_new, r_new, cache_c, cache_rt, *, t):
    batch, n_pages = page_table.shape
    page = cache_c.shape[1]
    g = PAGES_PER_STEP
    n_steps = n_pages // g
    rows = ql.shape[1]
    per_b = lambda i, pt: (i // n_steps, 0, 0)
    grid_spec = pltpu.PrefetchScalarGridSpec(
        num_scalar_prefetch=1,
        grid=(batch * n_steps,),
        in_specs=[
            pl.BlockSpec((1, rows, KV_LORA), per_b),
            pl.BlockSpec((1, rows, MLA_ROPE), per_b),
            pl.BlockSpec((1, t, KV_LORA), per_b),
            pl.BlockSpec((1, t, MLA_ROPE), per_b),
            pl.BlockSpec(memory_space=pl.ANY),
            pl.BlockSpec(memory_space=pl.ANY),
        ],
        out_specs=pl.BlockSpec((1, rows, KV_LORA), per_b),
        scratch_shapes=[
            pltpu.VMEM((N_SLOTS, g * page, KV_LORA), F32),
            pltpu.VMEM((N_SLOTS, g, MLA_ROPE, page), F32),
            pltpu.SemaphoreType.DMA((N_SLOTS, 2)),
            pltpu.VMEM((rows, LANE), F32), pltpu.VMEM((rows, LANE), F32), pltpu.VMEM((rows, KV_LORA), F32),
        ],
    )
    return pl.pallas_call(
        functools.partial(_mla_sample_kernel, t=t, page=page, g=g, n_steps=n_steps),
        grid_spec=grid_spec,
        out_shape=jax.ShapeDtypeStruct((batch, rows, KV_LORA), BF16),
        compiler_params=_params(("arbitrary",)),
        name="mla_sample",
    )(page_table, ql, qr, c_new, r_new, cache_c, cache_rt)


def _finish_kernel(x_ref, ar_ref, am_ref, ga_ref, gb_ref, ple_ref, wdr_ref, wdm_ref, wo_ref, wpg_ref, wpp_ref,
                   fw_ref, o_ref, *, final):
    h_ret = jnp.dot(ar_ref[...], wdr_ref[...], preferred_element_type=F32)
    h_mla = jnp.dot(am_ref[...], wdm_ref[...], preferred_element_type=F32)
    merged = (jax.nn.sigmoid(ga_ref[...].astype(F32)) * h_ret
              + jax.nn.sigmoid(gb_ref[...].astype(F32)) * h_mla)
    x2 = x_ref[...] + jnp.dot(merged.astype(BF16), wo_ref[...], preferred_element_type=F32)
    gate = jax.nn.sigmoid(jnp.dot(x2.astype(BF16), wpg_ref[...], preferred_element_type=F32))
    y = x2 + gate * jnp.dot(ple_ref[...].astype(BF16), wpp_ref[...], preferred_element_type=F32)
    if final:
        y = _rms(y, fw_ref[...])
    o_ref[...] = y


def _finish(x, act_ret, act_mla, slab, ple, wdr, wdm, wo, wpg, wpp, fw, *, tm, final):
    m, d = x.shape
    row = lambda i: (i, 0)
    fixed = lambda i: (0, 0)
    once = pl.Buffered(1)
    return pl.pallas_call(
        functools.partial(_finish_kernel, final=final),
        grid=(m // tm,),
        in_specs=[
            pl.BlockSpec((tm, d), row),
            pl.BlockSpec((tm, RET_VW), row),
            pl.BlockSpec((tm, MLA_VW), row),
            pl.BlockSpec((tm, d), lambda i: (i, _S_GA // D_MODEL)),
            pl.BlockSpec((tm, d), lambda i: (i, _S_GB // D_MODEL)),
            pl.BlockSpec((tm, PLE_DIM), row),
            pl.BlockSpec(wdr.shape, fixed, pipeline_mode=once),
            pl.BlockSpec(wdm.shape, fixed, pipeline_mode=once),
            pl.BlockSpec(wo.shape, fixed, pipeline_mode=once),
            pl.BlockSpec(wpg.shape, fixed, pipeline_mode=once),
            pl.BlockSpec(wpp.shape, fixed, pipeline_mode=once),
            pl.BlockSpec((1, d), fixed),
        ],
        out_specs=pl.BlockSpec((tm, d), row),
        out_shape=jax.ShapeDtypeStruct((m, d), F32),
        compiler_params=_params(("arbitrary",)),
        name="finish",
    )(x, act_ret, act_mla, slab, slab, ple, wdr, wdm, wo, wpg, wpp, fw)


def _rope_tables(pos, half):
    inv_freq = ROPE_BASE ** (-np.arange(half, dtype=np.float64) / half)
    ang = np.asarray(pos, np.float64)[:, None] * inv_freq[None, :]
    return np.cos(ang).astype(np.float32), np.sin(ang).astype(np.float32)


def _mla_rope_tables(pos):
    cos, sin = _rope_tables(pos, MLA_ROPE // 2)
    z = np.zeros((len(pos), LANE - MLA_ROPE), np.float32)
    return np.concatenate([cos, cos, z], axis=1), np.concatenate([sin, sin, z], axis=1)


def _decay_tables(c, rows):
    log_g = np.log1p(-np.exp2(-5.0 - np.arange(RET_HEADS, dtype=np.float64)))
    idx = np.arange(c, dtype=np.float64)
    diff = idx[:, None] - idx[None, :]
    dmask = np.where(diff >= 0, np.exp(np.maximum(diff, 0.0)[None] * log_g[:, None, None]), 0.0)
    qdec = np.exp((idx[None, :] + 1.0) * log_g[:, None])[:, :, None]
    kdec = np.exp((c - 1.0 - idx)[None, :] * log_g[:, None])[:, :, None]
    pad = rows - c
    dmask = np.pad(dmask, ((0, 0), (0, pad), (0, pad)))
    qdec = np.broadcast_to(np.pad(qdec, ((0, 0), (0, pad), (0, 0))), (RET_HEADS, rows, RET_DV))
    kdec = np.broadcast_to(np.pad(kdec, ((0, 0), (0, pad), (0, 0))), (RET_HEADS, rows, RET_DK))
    gcs = np.broadcast_to(np.exp(c * log_g)[:, None, None], (RET_HEADS, 1, RET_DV))
    return tuple(np.ascontiguousarray(a, np.float32) for a in (dmask, qdec, kdec, gcs))


def _rot_cols(w):
    half = w.shape[-1] // 2
    return jnp.concatenate([-w[..., half:], w[..., :half]], axis=-1)


def _layer_weights(w_in, w_uq, w_ukv, w_down_ret, w_down_mla, w_out, w_ple_gate, w_ple_proj):
    d = w_in.shape[0]
    src = [r for lo, hi in ((_OFF_RV, _OFF_CQ), (_OFF_MG, w_in.shape[1]), (_OFF_RQ, _OFF_RV))
           for r in range(lo, hi, WT_TILE)]
    n_main = len(src) * WT_TILE
    src += list(range(_OFF_CQ, _OFF_MG, WT_TILE))
    w_main = _transpose_weight(w_in.T, src)
    c_cols = w_main[:, n_main:n_main + _OFF_MG - _OFF_CQ]
    kr = c_cols[:, _OFF_KR - _OFF_CQ:]
    zpad = jnp.zeros((d, LANE - MLA_ROPE), BF16)
    w_c = jnp.concatenate([c_cols, zpad, _rot_cols(kr), zpad], axis=1)
    uq = w_uq.reshape(Q_LORA, MLA_HEADS, MLA_QK)
    uq_rope = uq[:, :, MLA_NOPE:]
    zq = jnp.zeros((Q_LORA, MLA_HEADS, MLA_QPAD - MLA_QK), F32)
    w_qa = jnp.concatenate([uq, zq], axis=2).reshape(Q_LORA, MLA_HEADS * MLA_QPAD).astype(BF16)
    w_qb = jnp.concatenate([_rot_cols(uq_rope), zq], axis=2).reshape(Q_LORA, MLA_HEADS * LANE).astype(BF16)
    ukv = w_ukv.reshape(KV_LORA, MLA_HEADS, MLA_KVW)
    w_uk_t = jnp.transpose(ukv[:, :, :MLA_NOPE], (1, 2, 0)).astype(BF16)
    w_uv = jnp.transpose(ukv[:, :, MLA_NOPE:], (1, 0, 2)).astype(BF16)
    return dict(w_main=w_main, w_c=w_c, w_qa=w_qa, w_qb=w_qb, w_kv=w_ukv.astype(BF16),
                w_uk_t=w_uk_t, w_uv=w_uv, wdr=w_down_ret.astype(BF16), wdm=w_down_mla.astype(BF16),
                wo=w_out.astype(BF16), wpg=w_ple_gate.astype(BF16), wpp=w_ple_proj.astype(BF16))


def _row_tile(m, cap):
    t = min(m, cap)
    assert m % t == 0
    return t


def _project(x, ln_w, lw, cos_r, sin_r):
    tm = _row_tile(cos_r.shape[0], 1024)
    assert x.shape[0] % tm == 0
    qk, xn = _inproj_rope(x, ln_w, lw["w_main"], cos_r, sin_r, col0=_S_W, n=2 * RET_QK, tm=tm, tn=1024,
                          n_q=RET_QK // 1024, k_scale=RET_DK ** -0.5)
    slab = _inproj(xn, lw["w_main"], col0=0, n=_S_W, tm=tm, tn=SLAB_TN)
    return qk, slab, xn


def kernel(x_prompt, x_sample, cache_ckv, cache_krope, state_ret, page_table, p_prompt, p_sample, ln_w, w_in,
           q_norm_w, w_uq, kv_norm_w, w_ukv, ret_gn_w, w_down_ret, w_down_mla, w_out, w_ple_gate, w_ple_proj,
           final_norm_w):
    depth = w_in.shape[0]
    b, s, d = x_prompt.shape
    bd, t, _ = x_sample.shape
    n_pages = page_table.shape[1]
    page = cache_ckv.shape[2]
    past_len = n_pages * page
    assert s % RET_CHUNK == 0 and n_pages % PAGES_PER_STEP == 0

    pos_p = np.arange(s)
    pos_s = np.tile(past_len + np.arange(t), bd)
    cos_rp, sin_rp = _rope_tables(pos_p, RET_DK // 2)
    cos_rs, sin_rs = _rope_tables(pos_s, RET_DK // 2)
    cos_mp, sin_mp = _mla_rope_tables(pos_p)
    cos_ms, sin_ms = _mla_rope_tables(pos_s)
    dec_p = _decay_tables(RET_CHUNK, RET_CHUNK)
    dec_s = _decay_tables(t, RET_CHUNK)

    y_p = x_prompt.reshape(b * s, d)
    y_s = x_sample.reshape(bd * t, d)
    tq = _row_tile(s, 512)
    tk = tq
    outs = [[] for _ in range(6)]
    for i in range(depth):
        lw = _layer_weights(w_in[i], w_uq[i], w_ukv[i], w_down_ret[i], w_down_mla[i], w_out[i], w_ple_gate[i],
                            w_ple_proj[i])
        lnw = ln_w[i][None, :]
        qnw = q_norm_w[i][None, :]
        kvnw = kv_norm_w[i][None, :]
        gnw = ret_gn_w[i][None, :]
        fw = final_norm_w[None, :]
        final = i == depth - 1

        qk, slab, xn = _project(y_p, lnw, lw, cos_rp, sin_rp)
        q, ckv, kr, krp, kv = _mla_prep(xn, lw["w_c"], qnw, lw["w_qa"], lw["w_qb"], kvnw, cos_mp, sin_mp,
                                        lw["w_kv"], tm=_row_tile(s, 512), with_kv=True)
        act_ret, s_fin = _ret_prompt(qk, slab, gnw, *dec_p, batch=b, seq=s)
        act_mla = _mla_prompt(q, kv, krp, slab, batch=b, seq=s, tq=tq, tk=tk)
        y_p = _finish(y_p, act_ret, act_mla, slab, p_prompt[i].reshape(b * s, -1), lw["wdr"], lw["wdm"], lw["wo"],
                      lw["wpg"], lw["wpp"], fw, tm=_row_tile(b * s, 512), final=final)
        outs[0].append(ckv.reshape(b, s, KV_LORA))
        outs[1].append(kr.reshape(b, s, MLA_ROPE))
        outs[2].append(s_fin.astype(x_prompt.dtype))

        ms = bd * t
        qk, slab, xn = _project(y_s, lnw, lw, cos_rs, sin_rs)
        q, ckv, kr = _mla_prep(xn, lw["w_c"], qnw, lw["w_qa"], lw["w_qb"], kvnw, cos_ms, sin_ms, None,
                               tm=ms, with_kv=False)
        act_ret, s_new = _ret_sample(qk, slab, gnw, state_ret[i], *dec_s, batch=bd, t=t)
        q_lat = _headproj(q, lw["w_uk_t"], x_stride=MLA_QPAD // LANE)
        q_lat = q_lat.reshape(bd, t * MLA_HEADS, KV_LORA)
        q_rope = q.reshape(bd, t * MLA_HEADS, MLA_QPAD)[..., MLA_NOPE:MLA_QK]
        cache_rt = jnp.swapaxes(cache_krope[i], 1, 2)
        o_lat = _mla_sample(page_table, q_lat, q_rope, ckv.reshape(bd, t, KV_LORA), kr.reshape(bd, t, MLA_ROPE),
                            cache_ckv[i], cache_rt, t=t)
        o_lat = o_lat.reshape(ms, MLA_HEADS * KV_LORA)
        act_mla = _headproj(o_lat, lw["w_uv"], x_stride=1, gate=slab, gate_blk0=_S_MG // MLA_V)
        y_s = _finish(y_s, act_ret, act_mla, slab, p_sample[i].reshape(ms, -1), lw["wdr"], lw["wdm"], lw["wo"],
                      lw["wpg"], lw["wpp"], fw, tm=_row_tile(ms, 512), final=final)
        outs[3].append(ckv.reshape(bd, t, KV_LORA))
        outs[4].append(kr.reshape(bd, t, MLA_ROPE))
        outs[5].append(s_new.astype(state_ret.dtype))

    y_prompt = y_p.reshape(b, s, d)
    y_sample = y_s.reshape(bd, t, d)
    ckv_p, kr_p, ret_p, ckv_s, kr_s, ret_s = [jnp.stack(o) for o in outs]
    return (y_prompt, y_sample, ckv_p, kr_p, ret_p, ckv_s, kr_s, ret_s)
```

```python
import functools
import math

import numpy as np
import jax
import jax.numpy as jnp
from jax import lax
from jax.experimental import pallas as pl
from jax.experimental.pallas import tpu as pltpu

F32 = jnp.float32
BF16 = jnp.bfloat16

D_MODEL = 1024
PLE_DIM = 256
RET_HEADS = 4
RET_DK = 256
RET_DV = 512
RET_CHUNK = 128
MLA_HEADS = 8
MLA_NOPE = 128
MLA_ROPE = 64
MLA_V = 128
Q_LORA = 384
KV_LORA = 256
ROPE_BASE = 10000.0
EPS = 1e-6

RET_QK = RET_HEADS * RET_DK
RET_VW = RET_HEADS * RET_DV
MLA_QK = MLA_NOPE + MLA_ROPE
MLA_VW = MLA_HEADS * MLA_V
MLA_QPAD = 256
MLA_KVW = MLA_NOPE + MLA_V
Q_SCALE = MLA_QK ** -0.5 * math.log2(math.e)
LANE = 128
PAGES_PER_STEP = 32
N_SLOTS = 4
KEY_GROUPS = 2
KEY_SPLITS = 2
HEADS_PER_STEP = 2
SLAB_TN = 1792
WT_TILE = 512
WT_ROWS = 64
RET_SAMPLE_SEQS = 4
VMEM_LIMIT = 48 * 1024 * 1024
FUSED_VMEM_LIMIT = 56 * 1024 * 1024
FUSED_FINISH_ROWS = 256

_OFF_RQ = 0
_OFF_RK = _OFF_RQ + RET_QK
_OFF_RV = _OFF_RK + RET_QK
_OFF_RG = _OFF_RV + RET_VW
_OFF_CQ = _OFF_RG + RET_VW
_OFF_CKV = _OFF_CQ + Q_LORA
_OFF_KR = _OFF_CKV + KV_LORA
_OFF_MG = _OFF_KR + MLA_ROPE
_OFF_GA = _OFF_MG + MLA_VW
_OFF_GB = _OFF_GA + D_MODEL
_S_RV = 0
_S_RG = _S_RV + RET_VW
_S_MG = _S_RG + RET_VW
_S_GA = _S_MG + MLA_VW
_S_GB = _S_GA + D_MODEL
_S_W = _S_GB + D_MODEL

_NT = (((1,), (1,)), ((), ()))
_TN = (((0,), (0,)), ((), ()))


def _params(sem):
    return pltpu.CompilerParams(dimension_semantics=sem, vmem_limit_bytes=VMEM_LIMIT)


def _rms(x, w):
    return x * lax.rsqrt(jnp.mean(x * x, axis=-1, keepdims=True) + EPS) * w


def _silu(x):
    return x * jax.nn.sigmoid(x)


def _lane_blocks(s):
    return [s[:, c * LANE:(c + 1) * LANE] for c in range(s.shape[1] // LANE)]


def _lane_rowmax(s):
    blocks = _lane_blocks(s)
    mx = blocks[0]
    for blk in blocks[1:]:
        mx = jnp.maximum(mx, blk)
    return jnp.broadcast_to(jnp.max(mx, axis=-1, keepdims=True), mx.shape)


def _lane_exp(s, m):
    ps = [jnp.exp2(blk - m) for blk in _lane_blocks(s)]
    part = ps[0]
    for p in ps[1:]:
        part = part + p
    return jnp.concatenate([p.astype(BF16) for p in ps], axis=1), part


def _matmul_kernel(x_ref, w_ref, o_ref):
    o_ref[...] = jnp.dot(x_ref[...], w_ref[...], preferred_element_type=F32).astype(o_ref.dtype)


def _inproj_rope_kernel(x_ref, lnw_ref, w_ref, cos_ref, sin_ref, o_ref, xn_ref, *, n_q, k_scale, tn):
    j = pl.program_id(1)

    @pl.when(j == 0)
    def _():
        xn_ref[...] = _rms(x_ref[...], lnw_ref[...]).astype(BF16)

    acc = jnp.dot(xn_ref[...], w_ref[...], preferred_element_type=F32)
    scale = jnp.where(j >= n_q, k_scale, 1.0).astype(F32)
    cos = cos_ref[...] * scale
    sin = sin_ref[...] * scale
    for h in range(tn // RET_DK):
        lo = h * RET_DK
        a1 = acc[:, lo:lo + LANE]
        a2 = acc[:, lo + LANE:lo + RET_DK]
        o_ref[:, lo:lo + LANE] = (a1 * cos - a2 * sin).astype(o_ref.dtype)
        o_ref[:, lo + LANE:lo + RET_DK] = (a2 * cos + a1 * sin).astype(o_ref.dtype)


def _inproj_rope(x, ln_w, w, cos, sin, *, col0, n, tm, tn, n_q, k_scale):
    m, d = x.shape
    blk0 = col0 // tn
    p_tiles = cos.shape[0] // tm
    return pl.pallas_call(
        functools.partial(_inproj_rope_kernel, n_q=n_q, k_scale=k_scale, tn=tn),
        grid=(m // tm, n // tn),
        in_specs=[
            pl.BlockSpec((tm, d), lambda i, j: (i, 0)),
            pl.BlockSpec((1, d), lambda i, j: (0, 0)),
            pl.BlockSpec((d, tn), lambda i, j: (0, blk0 + j)),
            pl.BlockSpec((tm, LANE), lambda i, j: (i % p_tiles, 0)),
            pl.BlockSpec((tm, LANE), lambda i, j: (i % p_tiles, 0)),
        ],
        out_specs=[pl.BlockSpec((tm, tn), lambda i, j: (i, j)), pl.BlockSpec((tm, d), lambda i, j: (i, 0))],
        out_shape=[jax.ShapeDtypeStruct((m, n), BF16), jax.ShapeDtypeStruct((m, d), BF16)],
        compiler_params=_params(("arbitrary", "arbitrary")),
        name="inproj_rope",
    )(x, ln_w, w, cos, sin)


def _inproj(xn, w, *, col0, n, tm, tn):
    m, d = xn.shape
    blk0 = col0 // tn
    return pl.pallas_call(
        _matmul_kernel,
        grid=(m // tm, n // tn),
        in_specs=[pl.BlockSpec((tm, d), lambda i, j: (i, 0)), pl.BlockSpec((d, tn), lambda i, j: (0, blk0 + j))],
        out_specs=pl.BlockSpec((tm, tn), lambda i, j: (i, j)),
        out_shape=jax.ShapeDtypeStruct((m, n), BF16),
        compiler_params=_params(("arbitrary", "arbitrary")),
        name="inproj",
    )(xn, w)


def _wt_kernel(tbl_ref, *refs):
    del tbl_ref
    o_ref = refs[-1]
    rows = jnp.concatenate([r[...] for r in refs[:-1]], axis=0)
    o_ref[...] = rows.T.astype(BF16)


def _transpose_weight(w_t, src_rows):
    d = w_t.shape[1]
    parts = WT_TILE // WT_ROWS
    assert all(r % WT_ROWS == 0 for r in src_rows)
    tbl = np.asarray([r // WT_ROWS for r in src_rows], np.int32)
    grid_spec = pltpu.PrefetchScalarGridSpec(
        num_scalar_prefetch=1,
        grid=(len(src_rows),),
        in_specs=[pl.BlockSpec((WT_ROWS, d), lambda j, tbl, r=r: (tbl[j] + r, 0)) for r in range(parts)],
        out_specs=pl.BlockSpec((d, WT_TILE), lambda j, tbl: (0, j)),
    )
    return pl.pallas_call(
        _wt_kernel,
        grid_spec=grid_spec,
        out_shape=jax.ShapeDtypeStruct((d, len(src_rows) * WT_TILE), BF16),
        compiler_params=_params(("arbitrary",)),
        name="wt",
    )(tbl, *([w_t] * parts))


def _mla_prep_kernel(xn_ref, wc_ref, qnw_ref, wqa_ref, wqb_ref, kvnw_ref, cos_ref, sin_ref, *rest,
                     with_kv):
    if with_kv:
        wkv_ref, q_ref, ckv_ref, kr_ref, krp_ref, kv_ref = rest
    else:
        q_ref, ckv_ref, kr_ref = rest
    zc = jnp.dot(xn_ref[...], wc_ref[...], preferred_element_type=F32)
    cos = cos_ref[...]
    sin = sin_ref[...]
    cqn = _rms(zc[:, :Q_LORA], qnw_ref[...]).astype(BF16)
    qa = jnp.dot(cqn, wqa_ref[...], preferred_element_type=F32)
    qb = jnp.dot(cqn, wqb_ref[...], preferred_element_type=F32)
    qcos = cos * Q_SCALE
    qsin = sin * Q_SCALE
    for h in range(MLA_HEADS):
        lo = h * MLA_QPAD
        q_ref[:, lo:lo + LANE] = (qa[:, lo:lo + LANE] * Q_SCALE).astype(q_ref.dtype)
        q_ref[:, lo + LANE:lo + MLA_QPAD] = (
            qa[:, lo + LANE:lo + MLA_QPAD] * qcos + qb[:, h * LANE:(h + 1) * LANE] * qsin).astype(q_ref.dtype)
    c0 = Q_LORA
    ckvn = _rms(zc[:, c0:c0 + KV_LORA], kvnw_ref[...])
    ckv_ref[...] = ckvn
    k0 = c0 + KV_LORA
    krp = zc[:, k0:k0 + LANE] * cos + zc[:, k0 + LANE:k0 + 2 * LANE] * sin
    kr_ref[...] = krp[:, :MLA_ROPE]
    if with_kv:
        krp_ref[...] = krp.astype(BF16)
        kv_ref[...] = jnp.dot(ckvn.astype(BF16), wkv_ref[...], preferred_element_type=F32).astype(BF16)


def _mla_prep(xn, wc, qnw, wqa, wqb, kvnw, cos, sin, wkv, *, tm, with_kv):
    m, d = xn.shape
    p_tiles = cos.shape[0] // tm
    row = lambda i: (i, 0)
    fixed = lambda i: (0, 0)
    in_specs = [
        pl.BlockSpec((tm, d), row),
        pl.BlockSpec(wc.shape, fixed),
        pl.BlockSpec((1, Q_LORA), fixed),
        pl.BlockSpec(wqa.shape, fixed),
        pl.BlockSpec(wqb.shape, fixed),
        pl.BlockSpec((1, KV_LORA), fixed),
        pl.BlockSpec((tm, LANE), lambda i: (i % p_tiles, 0)),
        pl.BlockSpec((tm, LANE), lambda i: (i % p_tiles, 0)),
    ]
    args = [xn, wc, qnw, wqa, wqb, kvnw, cos, sin]
    qw = MLA_HEADS * MLA_QPAD
    out_specs = [pl.BlockSpec((tm, qw), row), pl.BlockSpec((tm, KV_LORA), row), pl.BlockSpec((tm, MLA_ROPE), row)]
    out_shape = [jax.ShapeDtypeStruct((m, qw), BF16), jax.ShapeDtypeStruct((m, KV_LORA), F32),
                 jax.ShapeDtypeStruct((m, MLA_ROPE), F32)]
    if with_kv:
        in_specs.append(pl.BlockSpec(wkv.shape, fixed))
        args.append(wkv)
        out_specs += [pl.BlockSpec((tm, LANE), row), pl.BlockSpec((tm, wkv.shape[1]), row)]
        out_shape += [jax.ShapeDtypeStruct((m, LANE), BF16), jax.ShapeDtypeStruct((m, wkv.shape[1]), BF16)]
    return pl.pallas_call(
        functools.partial(_mla_prep_kernel, with_kv=with_kv),
        grid=(m // tm,),
        in_specs=in_specs,
        out_specs=out_specs,
        out_shape=out_shape,
        compiler_params=_params(("arbitrary",)),
        name="mla_prep",
    )(*args)


def _ret_chunk(q, k, v, s_prev, dmask, qdec, kdec, gc):
    sc = lax.dot_general(q, k, _NT, preferred_element_type=F32) * dmask
    intra = jnp.dot(sc.astype(BF16), v, preferred_element_type=F32)
    cross = jnp.dot(q, s_prev.astype(BF16), preferred_element_type=F32) * qdec
    kd = (k.astype(F32) * kdec).astype(BF16)
    s_new = gc * s_prev + lax.dot_general(kd, v, _TN, preferred_element_type=F32)
    return intra + cross, s_new


def _ret_act(o, rg, gnw):
    mu = jnp.mean(o, axis=-1, keepdims=True)
    d = o - mu
    var = jnp.mean(d * d, axis=-1, keepdims=True)
    return d * lax.rsqrt(var + EPS) * gnw * _silu(rg.astype(F32))


def _ret_prompt_kernel(q_ref, k_ref, v_ref, rg_ref, gnw_ref, dm_ref, qd_ref, kd_ref, gc_ref,
                       o_ref, sfin_ref, s_ref, *, n_chunks, nh):
    s_ref[...] = jnp.zeros_like(s_ref)

    def chunk(c, carry):
        rows = pl.ds(pl.multiple_of(c * RET_CHUNK, RET_CHUNK), RET_CHUNK)
        for h in range(nh):
            qk_cols = slice(h * RET_DK, (h + 1) * RET_DK)
            v_cols = slice(h * RET_DV, (h + 1) * RET_DV)
            o, s_new = _ret_chunk(q_ref[rows, qk_cols], k_ref[rows, qk_cols], v_ref[rows, v_cols], s_ref[h],
                                  dm_ref[h], qd_ref[h], kd_ref[h], gc_ref[h])
            s_ref[h] = s_new
            o_ref[rows, v_cols] = _ret_act(o, rg_ref[rows, v_cols], gnw_ref[:, v_cols]).astype(o_ref.dtype)
        return carry

    lax.fori_loop(0, n_chunks, chunk, 0, unroll=4)
    sfin_ref[0] = s_ref[...]


def _ret_prompt(qk, slab, gn_w, dmask, qdec, kdec, gcs, *, batch, seq):
    nh = HEADS_PER_STEP
    qw, vw = nh * RET_DK, nh * RET_DV
    k_blk = RET_QK // qw
    rg_blk = _S_RG // vw
    return pl.pallas_call(
        functools.partial(_ret_prompt_kernel, n_chunks=seq // RET_CHUNK, nh=nh),
        grid=(batch, RET_HEADS // nh),
        in_specs=[
            pl.BlockSpec((seq, qw), lambda b, hp: (b, hp)),
            pl.BlockSpec((seq, qw), lambda b, hp: (b, k_blk + hp)),
            pl.BlockSpec((seq, vw), lambda b, hp: (b, hp)),
            pl.BlockSpec((seq, vw), lambda b, hp: (b, rg_blk + hp)),
            pl.BlockSpec((1, vw), lambda b, hp: (0, hp)),
            pl.BlockSpec((nh, RET_CHUNK, RET_CHUNK), lambda b, hp: (hp, 0, 0)),
            pl.BlockSpec((nh, RET_CHUNK, RET_DV), lambda b, hp: (hp, 0, 0)),
            pl.BlockSpec((nh, RET_CHUNK, RET_DK), lambda b, hp: (hp, 0, 0)),
            pl.BlockSpec((nh, 1, RET_DV), lambda b, hp: (hp, 0, 0)),
        ],
        out_specs=[
            pl.BlockSpec((seq, vw), lambda b, hp: (b, hp)),
            pl.BlockSpec((1, nh, RET_DK, RET_DV), lambda b, hp: (b, hp, 0, 0)),
        ],
        out_shape=[
            jax.ShapeDtypeStruct((batch * seq, RET_VW), BF16),
            jax.ShapeDtypeStruct((batch, RET_HEADS, RET_DK, RET_DV), F32),
        ],
        scratch_shapes=[pltpu.VMEM((nh, RET_DK, RET_DV), F32)],
        compiler_params=_params(("arbitrary", "arbitrary")),
        name="ret_prompt",
    )(qk, qk, slab, slab, gn_w, dmask, qdec, kdec, gcs)


def _ret_sample_kernel(qk_ref, vg_ref, gnw_ref, st_ref, dm_ref, qd_ref, kd_ref, gc_ref,
                       o_ref, snew_ref, qp_ref, kp_ref, vp_ref, act_ref, *, t, nb):
    qp_ref[...] = jnp.zeros_like(qp_ref)
    kp_ref[...] = jnp.zeros_like(kp_ref)
    vp_ref[...] = jnp.zeros_like(vp_ref)
    qk = qk_ref[...].astype(F32)
    vg = vg_ref[...].astype(F32)
    for bb in range(nb):
        rows = slice(bb * t, (bb + 1) * t)
        for h in range(RET_HEADS):
            qp_ref[0:t, :] = qk[rows, h * RET_DK:(h + 1) * RET_DK]
            kp_ref[0:t, :] = qk[rows, RET_QK + h * RET_DK:RET_QK + (h + 1) * RET_DK]
            vp_ref[0:t, :] = vg[rows, _S_RV + h * RET_DV:_S_RV + (h + 1) * RET_DV]
            o, s_new = _ret_chunk(qp_ref[...].astype(BF16), kp_ref[...].astype(BF16), vp_ref[...].astype(BF16),
                                  st_ref[bb, h], dm_ref[h], qd_ref[h], kd_ref[h], gc_ref[h])
            snew_ref[bb, h] = s_new
            act_ref[rows, h * RET_DV:(h + 1) * RET_DV] = _ret_act(
                o[0:t, :], vg[rows, _S_RG + h * RET_DV:_S_RG + (h + 1) * RET_DV],
                gnw_ref[:, h * RET_DV:(h + 1) * RET_DV])
    o_ref[...] = act_ref[...].astype(o_ref.dtype)


def _ret_sample(qk, slab, gn_w, state, dmask, qdec, kdec, gcs, *, batch, t):
    h = RET_HEADS
    whole = lambda b: (0, 0, 0)
    nb = _row_tile(batch, RET_SAMPLE_SEQS)
    return pl.pallas_call(
        functools.partial(_ret_sample_kernel, t=t, nb=nb),
        grid=(batch // nb,),
        in_specs=[
            pl.BlockSpec((nb * t, qk.shape[1]), lambda b: (b, 0)),
            pl.BlockSpec((nb * t, _S_MG), lambda b: (b, 0)),
            pl.BlockSpec((1, RET_VW), lambda b: (0, 0)),
            pl.BlockSpec((nb, h, RET_DK, RET_DV), lambda b: (b, 0, 0, 0)),
            pl.BlockSpec(dmask.shape, whole),
            pl.BlockSpec(qdec.shape, whole),
            pl.BlockSpec(kdec.shape, whole),
            pl.BlockSpec(gcs.shape, whole),
        ],
        out_specs=[
            pl.BlockSpec((nb * t, RET_VW), lambda b: (b, 0)),
            pl.BlockSpec((nb, h, RET_DK, RET_DV), lambda b: (b, 0, 0, 0)),
        ],
        out_shape=[
            jax.ShapeDtypeStruct((batch * t, RET_VW), BF16),
            jax.ShapeDtypeStruct(state.shape, state.dtype),
        ],
        scratch_shapes=[pltpu.VMEM((RET_CHUNK, RET_DK), F32), pltpu.VMEM((RET_CHUNK, RET_DK), F32),
                        pltpu.VMEM((RET_CHUNK, RET_DV), F32), pltpu.VMEM((nb * t, RET_VW), F32)],
        compiler_params=_params(("arbitrary",)),
        name="ret_sample",
    )(qk, slab, gn_w, state, dmask, qdec, kdec, gcs)


def _mla_prompt_kernel(q_ref, kv_ref, krp_ref, mg_ref, o_ref, m_ref, l_ref, acc_ref, *, tq, tk, nh):
    qi = pl.program_id(2)
    ratio = tq // tk

    def scores(kb, h):
        rows = pl.ds(pl.multiple_of(kb * tk, tk), tk)
        k = jnp.concatenate([kv_ref[rows, h * MLA_KVW:h * MLA_KVW + MLA_NOPE], krp_ref[rows, :]], axis=1)
        s = lax.dot_general(q_ref[:, h * MLA_QPAD:(h + 1) * MLA_QPAD], k, _NT, preferred_element_type=F32)
        return s, kv_ref[rows, h * MLA_KVW + MLA_NOPE:(h + 1) * MLA_KVW]

    def update(h, s, v, first=False):
        m_cur = _lane_rowmax(s)
        if first:
            m_new = m_cur
        else:
            m_old = m_ref[h]
            m_new = jnp.maximum(m_old, m_cur)
            alpha = jnp.exp2(m_old - m_new)
        p, l_part = _lane_exp(s, m_new)
        pv = jnp.dot(p, v, preferred_element_type=F32)
        m_ref[h] = m_new
        l_ref[h] = l_part if first else alpha * l_ref[h] + l_part
        acc_ref[h] = pv if first else alpha * acc_ref[h] + pv

    r_id = lax.broadcasted_iota(jnp.int32, (tq, tk), 0)
    c_id = lax.broadcasted_iota(jnp.int32, (tq, tk), 1)
    for d in range(ratio):
        for h in range(nh):
            s, v = scores(qi * ratio + d, h)
            update(h, jnp.where(c_id + d * tk <= r_id, s, -jnp.inf), v, first=d == 0)

    def body(kb, carry):
        for h in range(nh):
            s, v = scores(kb, h)
            update(h, s, v)
        return carry

    lax.fori_loop(0, qi * ratio, body, 0)
    for h in range(nh):
        cols = slice(h * MLA_V, (h + 1) * MLA_V)
        l = jnp.sum(l_ref[h], axis=-1, keepdims=True)
        o_ref[:, cols] = (acc_ref[h] / l * _silu(mg_ref[:, cols].astype(F32))).astype(o_ref.dtype)


def _mla_prompt(q, kv, krp, slab, *, batch, seq, tq, tk):
    nq = seq // tq
    nh = MLA_HEADS
    mg_blk = _S_MG // (nh * MLA_V)
    return pl.pallas_call(
        functools.partial(_mla_prompt_kernel, tq=tq, tk=tk, nh=nh),
        grid=(batch, MLA_HEADS // nh, nq),
        in_specs=[
            pl.BlockSpec((tq, nh * MLA_QPAD), lambda b, hp, i: (b * nq + i, hp)),
            pl.BlockSpec((seq, nh * MLA_KVW), lambda b, hp, i: (b, hp)),
            pl.BlockSpec((seq, LANE), lambda b, hp, i: (b, 0)),
            pl.BlockSpec((tq, nh * MLA_V), lambda b, hp, i: (b * nq + i, mg_blk + hp)),
        ],
        out_specs=pl.BlockSpec((tq, nh * MLA_V), lambda b, hp, i: (b * nq + i, hp)),
        out_shape=jax.ShapeDtypeStruct((batch * seq, MLA_VW), BF16),
        scratch_shapes=[pltpu.VMEM((nh, tq, LANE), F32), pltpu.VMEM((nh, tq, LANE), F32),
                        pltpu.VMEM((nh, tq, MLA_V), F32)],
        compiler_params=_params(("arbitrary", "arbitrary", "arbitrary")),
        name="mla_prompt",
    )(q, kv, krp, slab)


def _headproj_kernel(x_ref, w_ref, *rest, gated):
    if gated:
        g_ref, o_ref = rest
    else:
        (o_ref,) = rest
    y = jnp.dot(x_ref[...], w_ref[0], preferred_element_type=F32)
    if gated:
        y = y * _silu(g_ref[...].astype(F32))
    o_ref[...] = y.astype(o_ref.dtype)


def _headproj(x, w, *, x_stride, gate=None, gate_blk0=0):
    m = x.shape[0]
    nh, kdim, n = w.shape
    in_specs = [pl.BlockSpec((m, kdim), lambda h: (0, h * x_stride)),
                pl.BlockSpec((1, kdim, n), lambda h: (h, 0, 0))]
    args = [x, w]
    if gate is not None:
        in_specs.append(pl.BlockSpec((m, n), lambda h: (0, gate_blk0 + h)))
        args.append(gate)
    return pl.pallas_call(
        functools.partial(_headproj_kernel, gated=gate is not None),
        grid=(nh,),
        in_specs=in_specs,
        out_specs=pl.BlockSpec((m, n), lambda h: (0, h)),
        out_shape=jax.ShapeDtypeStruct((m, nh * n), BF16),
        compiler_params=_params(("arbitrary",)),
        name="headproj",
    )(*args)


def _page_copies(pt_ref, cache_c, cache_rt, cbuf, rbuf, sem, idx, slot, *, g, n_steps, page):
    b = idx // n_steps
    first = (idx % n_steps) * g
    out = []
    for j in range(g):
        p = pt_ref[b, first + j]
        out.append(pltpu.make_async_copy(cache_c.at[p], cbuf.at[slot, pl.ds(j * page, page), :], sem.at[slot, 0]))
        out.append(pltpu.make_async_copy(cache_rt.at[p], rbuf.at[slot, j], sem.at[slot, 1]))
    return out


def _start_all(copies):
    for n, c in enumerate(copies):
        c.start(priority=(n // 2) % 2)


def _mla_sample_kernel(pt_ref, ql_ref, qr_ref, cn_ref, rn_ref, cache_c, cache_rt, o_ref,
                       cbuf, rbuf, sem, m_ref, l_ref, acc_ref, *, t, page, g, n_steps):
    i = pl.program_id(0)
    last = pl.num_programs(0) - 1
    step = i % n_steps
    slot = i % N_SLOTS
    copies = functools.partial(_page_copies, pt_ref, cache_c, cache_rt, cbuf, rbuf, sem,
                               g=g, n_steps=n_steps, page=page)

    @pl.when(i == 0)
    def _():
        for ahead in range(N_SLOTS - 1):
            _start_all(copies(jnp.minimum(ahead, last), ahead))

    ql = ql_ref[0]
    qr = qr_ref[0]
    rows = ql.shape[0]

    @pl.when(step == 0)
    def _():
        qlf = ql.astype(F32)
        qrf = qr.astype(F32)
        tok = lax.broadcasted_iota(jnp.int32, (rows, 1), 0) // (rows // t)
        cn = cn_ref[0].astype(BF16).astype(F32)
        rn = rn_ref[0].astype(BF16).astype(F32)
        s_cols = []
        for j in range(t):
            sj = (jnp.sum(qlf * cn[j:j + 1, :], axis=-1, keepdims=True)
                  + jnp.sum(qrf * rn[j:j + 1, :], axis=-1, keepdims=True))
            s_cols.append(jnp.where(tok >= j, sj, -jnp.inf))
        m0 = s_cols[0]
        for j in range(1, t):
            m0 = jnp.maximum(m0, s_cols[j])
        l0 = jnp.zeros((rows, 1), F32)
        a0 = jnp.zeros((rows, KV_LORA), F32)
        for j in range(t):
            pj = jnp.exp2(s_cols[j] - m0)
            l0 = l0 + pj
            a0 = a0 + pj.astype(BF16).astype(F32) * cn[j:j + 1, :]
        lane = lax.broadcasted_iota(jnp.int32, (rows, LANE), 1)
        m_ref[...] = jnp.broadcast_to(m0, (rows, LANE))
        l_ref[...] = jnp.where(lane == 0, l0, 0.0)
        acc_ref[...] = a0

    for c in copies(i, slot):
        c.wait()
    ppk = g // (KEY_GROUPS * KEY_SPLITS)
    sub = ppk * page
    m_run, l_run, acc_run = m_ref[...], l_ref[...], acc_ref[...]
    for grp in range(KEY_GROUPS):
        cbs, parts = [], []
        for k in range(grp * KEY_SPLITS, (grp + 1) * KEY_SPLITS):
            c32 = cbuf[slot, k * sub:(k + 1) * sub, :]
            rb = jnp.concatenate([rbuf[slot, k * ppk + j] for j in range(ppk)], axis=1).astype(BF16)
            cbs.append(c32.astype(BF16))
            ct = c32.T.astype(BF16)
            parts.append(jnp.dot(ql, ct, preferred_element_type=F32)
                         + jnp.dot(qr, rb, preferred_element_type=F32))
        s = jnp.concatenate(parts, axis=1)
        m_new = jnp.maximum(m_run, _lane_rowmax(s))
        alpha = jnp.exp2(m_run - m_new)
        pb, l_part = _lane_exp(s, m_new)
        pv = [jnp.dot(pb[:, k * sub:(k + 1) * sub], cbs[k], preferred_element_type=F32) for k in range(KEY_SPLITS)]
        while len(pv) > 1:
            pv = [pv[j] + pv[j + 1] for j in range(0, len(pv), 2)]
        m_run = m_new
        l_run = alpha * l_run + l_part
        acc_run = jnp.concatenate([alpha] * (KV_LORA // LANE), axis=1) * acc_run + pv[0]
    m_ref[...] = m_run
    l_ref[...] = l_run
    acc_ref[...] = acc_run

    ahead = N_SLOTS - 1
    _start_all(copies(jnp.minimum(i + ahead, last), (i + ahead) % N_SLOTS))

    @pl.when(step == n_steps - 1)
    def _():
        o_ref[0] = (acc_ref[...] / jnp.sum(l_ref[...], axis=-1, keepdims=True)).astype(o_ref.dtype)

    @pl.when(i == last)
    def _():
        for extra in range(1, N_SLOTS):
            for c in copies(last, (last + extra) % N_SLOTS):
                c.wait()


def _mla_sample(page_table, ql, qr, c_new, r_new, cache_c, cache_rt, *, t):
    batch, n_pages = page_table.shape
    page = cache_c.shape[1]
    g = PAGES_PER_STEP
    n_steps = n_pages // g
    rows = ql.shape[1]
    per_b = lambda i, pt: (i // n_steps, 0, 0)
    grid_spec = pltpu.PrefetchScalarGridSpec(
        num_scalar_prefetch=1,
        grid=(batch * n_steps,),
        in_specs=[
            pl.BlockSpec((1, rows, KV_LORA), per_b),
            pl.BlockSpec((1, rows, MLA_ROPE), per_b),
            pl.BlockSpec((1, t, KV_LORA), per_b),
            pl.BlockSpec((1, t, MLA_ROPE), per_b),
            pl.BlockSpec(memory_space=pl.ANY),
            pl.BlockSpec(memory_space=pl.ANY),
        ],
        out_specs=pl.BlockSpec((1, rows, KV_LORA), per_b),
        scratch_shapes=[
            pltpu.VMEM((N_SLOTS, g * page, KV_LORA), F32),
            pltpu.VMEM((N_SLOTS, g, MLA_ROPE, page), F32),
            pltpu.SemaphoreType.DMA((N_SLOTS, 2)),
            pltpu.VMEM((rows, LANE), F32), pltpu.VMEM((rows, LANE), F32), pltpu.VMEM((rows, KV_LORA), F32),
        ],
    )
    return pl.pallas_call(
        functools.partial(_mla_sample_kernel, t=t, page=page, g=g, n_steps=n_steps),
        grid_spec=grid_spec,
        out_shape=jax.ShapeDtypeStruct((batch, rows, KV_LORA), BF16),
        compiler_params=_params(("arbitrary",)),
        name="mla_sample",
    )(page_table, ql, qr, c_new, r_new, cache_c, cache_rt)


def _finish_kernel(x_ref, ar_ref, am_ref, ga_ref, gb_ref, ple_ref, wdr_ref, wdm_ref, wo_ref, wpg_ref, wpp_ref,
                   fw_ref, o_ref, *, final):
    h_ret = jnp.dot(ar_ref[...], wdr_ref[...], preferred_element_type=F32)
    h_mla = jnp.dot(am_ref[...], wdm_ref[...], preferred_element_type=F32)
    merged = (jax.nn.sigmoid(ga_ref[...].astype(F32)) * h_ret
              + jax.nn.sigmoid(gb_ref[...].astype(F32)) * h_mla)
    x2 = x_ref[...] + jnp.dot(merged.astype(BF16), wo_ref[...], preferred_element_type=F32)
    gate = jax.nn.sigmoid(jnp.dot(x2.astype(BF16), wpg_ref[...], preferred_element_type=F32))
    y = x2 + gate * jnp.dot(ple_ref[...].astype(BF16), wpp_ref[...], preferred_element_type=F32)
    if final:
        y = _rms(y, fw_ref[...])
    o_ref[...] = y


def _finish(x, act_ret, act_mla, slab, ple, wdr, wdm, wo, wpg, wpp, fw, *, tm, final):
    m, d = x.shape
    row = lambda i: (i, 0)
    fixed = lambda i: (0, 0)
    once = pl.Buffered(1)
    return pl.pallas_call(
        functools.partial(_finish_kernel, final=final),
        grid=(m // tm,),
        in_specs=[
            pl.BlockSpec((tm, d), row),
            pl.BlockSpec((tm, RET_VW), row),
            pl.BlockSpec((tm, MLA_VW), row),
            pl.BlockSpec((tm, d), lambda i: (i, _S_GA // D_MODEL)),
            pl.BlockSpec((tm, d), lambda i: (i, _S_GB // D_MODEL)),
            pl.BlockSpec((tm, PLE_DIM), row),
            pl.BlockSpec(wdr.shape, fixed, pipeline_mode=once),
            pl.BlockSpec(wdm.shape, fixed, pipeline_mode=once),
            pl.BlockSpec(wo.shape, fixed, pipeline_mode=once),
            pl.BlockSpec(wpg.shape, fixed, pipeline_mode=once),
            pl.BlockSpec(wpp.shape, fixed, pipeline_mode=once),
            pl.BlockSpec((1, d), fixed),
        ],
        out_specs=pl.BlockSpec((tm, d), row),
        out_shape=jax.ShapeDtypeStruct((m, d), F32),
        compiler_params=_params(("arbitrary",)),
        name="finish",
    )(x, act_ret, act_mla, slab, slab, ple, wdr, wdm, wo, wpg, wpp, fw)


def _sample_attn_prompt_finish_kernel(pt_ref, ql_ref, qr_ref, cn_ref, rn_ref, cache_c, cache_rt,
                                      x_ref, ar_ref, am_ref, ga_ref, gb_ref, ple_ref, wdr_ref, wdm_ref, wo_ref,
                                      wpg_ref, wpp_ref, fw_ref, o_ref, y_ref, cbuf, rbuf, sem, m_ref, l_ref, acc_ref,
                                      *, t, page, g, n_steps, every, final):
    _mla_sample_kernel(pt_ref, ql_ref, qr_ref, cn_ref, rn_ref, cache_c, cache_rt, o_ref,
                       cbuf, rbuf, sem, m_ref, l_ref, acc_ref, t=t, page=page, g=g, n_steps=n_steps)

    @pl.when(pl.program_id(0) % every == 0)
    def _():
        _finish_kernel(x_ref, ar_ref, am_ref, ga_ref, gb_ref, ple_ref, wdr_ref, wdm_ref, wo_ref, wpg_ref, wpp_ref,
                       fw_ref, y_ref, final=final)


def _sample_attn_prompt_finish(page_table, ql, qr, c_new, r_new, cache_c, cache_rt,
                               x, act_ret, act_mla, slab, ple, wdr, wdm, wo, wpg, wpp, fw, *, t, tm, final):
    batch, n_pages = page_table.shape
    page = cache_c.shape[1]
    g = PAGES_PER_STEP
    n_steps = n_pages // g
    rows = ql.shape[1]
    m, d = x.shape
    total = batch * n_steps
    assert m % tm == 0 and total % (m // tm) == 0
    every = total // (m // tm)
    per_b = lambda i, pt: (i // n_steps, 0, 0)
    row = lambda i, pt: (i // every, 0)
    fixed = lambda i, pt: (0, 0)
    once = pl.Buffered(1)
    grid_spec = pltpu.PrefetchScalarGridSpec(
        num_scalar_prefetch=1,
        grid=(total,),
        in_specs=[
            pl.BlockSpec((1, rows, KV_LORA), per_b),
            pl.BlockSpec((1, rows, MLA_ROPE), per_b),
            pl.BlockSpec((1, t, KV_LORA), per_b),
            pl.BlockSpec((1, t, MLA_ROPE), per_b),
            pl.BlockSpec(memory_space=pl.ANY),
            pl.BlockSpec(memory_space=pl.ANY),
            pl.BlockSpec((tm, d), row),
            pl.BlockSpec((tm, RET_VW), row),
            pl.BlockSpec((tm, MLA_VW), row),
            pl.BlockSpec((tm, d), lambda i, pt: (i // every, _S_GA // D_MODEL)),
            pl.BlockSpec((tm, d), lambda i, pt: (i // every, _S_GB // D_MODEL)),
            pl.BlockSpec((tm, PLE_DIM), row),
            pl.BlockSpec(wdr.shape, fixed, pipeline_mode=once),
            pl.BlockSpec(wdm.shape, fixed, pipeline_mode=once),
            pl.BlockSpec(wo.shape, fixed, pipeline_mode=once),
            pl.BlockSpec(wpg.shape, fixed, pipeline_mode=once),
            pl.BlockSpec(wpp.shape, fixed, pipeline_mode=once),
            pl.BlockSpec((1, d), fixed),
        ],
        out_specs=[pl.BlockSpec((1, rows, KV_LORA), per_b), pl.BlockSpec((tm, d), row)],
        scratch_shapes=[
            pltpu.VMEM((N_SLOTS, g * page, KV_LORA), F32),
            pltpu.VMEM((N_SLOTS, g, MLA_ROPE, page), F32),
            pltpu.SemaphoreType.DMA((N_SLOTS, 2)),
            pltpu.VMEM((rows, LANE), F32), pltpu.VMEM((rows, LANE), F32), pltpu.VMEM((rows, KV_LORA), F32),
        ],
    )
    return pl.pallas_call(
        functools.partial(_sample_attn_prompt_finish_kernel, t=t, page=page, g=g, n_steps=n_steps, every=every,
                          final=final),
        grid_spec=grid_spec,
        out_shape=[jax.ShapeDtypeStruct((batch, rows, KV_LORA), BF16), jax.ShapeDtypeStruct((m, d), F32)],
        compiler_params=pltpu.CompilerParams(dimension_semantics=("arbitrary",), vmem_limit_bytes=FUSED_VMEM_LIMIT),
        name="mla_sample_finish",
    )(page_table, ql, qr, c_new, r_new, cache_c, cache_rt, x, act_ret, act_mla, slab, slab, ple,
      wdr, wdm, wo, wpg, wpp, fw)


def _rope_tables(pos, half):
    inv_freq = ROPE_BASE ** (-np.arange(half, dtype=np.float64) / half)
    ang = np.asarray(pos, np.float64)[:, None] * inv_freq[None, :]
    return np.cos(ang).astype(np.float32), np.sin(ang).astype(np.float32)


def _mla_rope_tables(pos):
    cos, sin = _rope_tables(pos, MLA_ROPE // 2)
    z = np.zeros((len(pos), LANE - MLA_ROPE), np.float32)
    return np.concatenate([cos, cos, z], axis=1), np.concatenate([sin, sin, z], axis=1)


def _decay_tables(c, rows):
    log_g = np.log1p(-np.exp2(-5.0 - np.arange(RET_HEADS, dtype=np.float64)))
    idx = np.arange(c, dtype=np.float64)
    diff = idx[:, None] - idx[None, :]
    dmask = np.where(diff >= 0, np.exp(np.maximum(diff, 0.0)[None] * log_g[:, None, None]), 0.0)
    qdec = np.exp((idx[None, :] + 1.0) * log_g[:, None])[:, :, None]
    kdec = np.exp((c - 1.0 - idx)[None, :] * log_g[:, None])[:, :, None]
    pad = rows - c
    dmask = np.pad(dmask, ((0, 0), (0, pad), (0, pad)))
    qdec = np.broadcast_to(np.pad(qdec, ((0, 0), (0, pad), (0, 0))), (RET_HEADS, rows, RET_DV))
    kdec = np.broadcast_to(np.pad(kdec, ((0, 0), (0, pad), (0, 0))), (RET_HEADS, rows, RET_DK))
    gcs = np.broadcast_to(np.exp(c * log_g)[:, None, None], (RET_HEADS, 1, RET_DV))
    return tuple(np.ascontiguousarray(a, np.float32) for a in (dmask, qdec, kdec, gcs))


def _rot_cols(w):
    half = w.shape[-1] // 2
    return jnp.concatenate([-w[..., half:], w[..., :half]], axis=-1)


def _layer_weights(w_in, w_uq, w_ukv, w_down_ret, w_down_mla, w_out, w_ple_gate, w_ple_proj):
    d = w_in.shape[0]
    src = [r for lo, hi in ((_OFF_RV, _OFF_CQ), (_OFF_MG, w_in.shape[1]), (_OFF_RQ, _OFF_RV))
           for r in range(lo, hi, WT_TILE)]
    n_main = len(src) * WT_TILE
    src += list(range(_OFF_CQ, _OFF_MG, WT_TILE))
    w_main = _transpose_weight(w_in.T, src)
    c_cols = w_main[:, n_main:n_main + _OFF_MG - _OFF_CQ]
    kr = c_cols[:, _OFF_KR - _OFF_CQ:]
    zpad = jnp.zeros((d, LANE - MLA_ROPE), BF16)
    w_c = jnp.concatenate([c_cols, zpad, _rot_cols(kr), zpad], axis=1)
    uq = w_uq.reshape(Q_LORA, MLA_HEADS, MLA_QK)
    uq_rope = uq[:, :, MLA_NOPE:]
    zq = jnp.zeros((Q_LORA, MLA_HEADS, MLA_QPAD - MLA_QK), F32)
    w_qa = jnp.concatenate([uq, zq], axis=2).reshape(Q_LORA, MLA_HEADS * MLA_QPAD).astype(BF16)
    w_qb = jnp.concatenate([_rot_cols(uq_rope), zq], axis=2).reshape(Q_LORA, MLA_HEADS * LANE).astype(BF16)
    ukv = w_ukv.reshape(KV_LORA, MLA_HEADS, MLA_KVW)
    w_uk_t = jnp.transpose(ukv[:, :, :MLA_NOPE], (1, 2, 0)).astype(BF16)
    w_uv = jnp.transpose(ukv[:, :, MLA_NOPE:], (1, 0, 2)).astype(BF16)
    return dict(w_main=w_main, w_c=w_c, w_qa=w_qa, w_qb=w_qb, w_kv=w_ukv.astype(BF16),
                w_uk_t=w_uk_t, w_uv=w_uv, wdr=w_down_ret.astype(BF16), wdm=w_down_mla.astype(BF16),
                wo=w_out.astype(BF16), wpg=w_ple_gate.astype(BF16), wpp=w_ple_proj.astype(BF16))


def _row_tile(m, cap):
    t = min(m, cap)
    assert m % t == 0
    return t


def _project(x, ln_w, lw, cos_r, sin_r):
    tm = _row_tile(cos_r.shape[0], 1024)
    assert x.shape[0] % tm == 0
    qk, xn = _inproj_rope(x, ln_w, lw["w_main"], cos_r, sin_r, col0=_S_W, n=2 * RET_QK, tm=tm, tn=1024,
                          n_q=RET_QK // 1024, k_scale=RET_DK ** -0.5)
    slab = _inproj(xn, lw["w_main"], col0=0, n=_S_W, tm=tm, tn=SLAB_TN)
    return qk, slab, xn


def kernel(x_prompt, x_sample, cache_ckv, cache_krope, state_ret, page_table, p_prompt, p_sample, ln_w, w_in,
           q_norm_w, w_uq, kv_norm_w, w_ukv, ret_gn_w, w_down_ret, w_down_mla, w_out, w_ple_gate, w_ple_proj,
           final_norm_w):
    depth = w_in.shape[0]
    b, s, d = x_prompt.shape
    bd, t, _ = x_sample.shape
    n_pages = page_table.shape[1]
    page = cache_ckv.shape[2]
    past_len = n_pages * page
    assert s % RET_CHUNK == 0 and n_pages % PAGES_PER_STEP == 0

    pos_p = np.arange(s)
    pos_s = np.tile(past_len + np.arange(t), bd)
    cos_rp, sin_rp = _rope_tables(pos_p, RET_DK // 2)
    cos_rs, sin_rs = _rope_tables(pos_s, RET_DK // 2)
    cos_mp, sin_mp = _mla_rope_tables(pos_p)
    cos_ms, sin_ms = _mla_rope_tables(pos_s)
    dec_p = _decay_tables(RET_CHUNK, RET_CHUNK)
    dec_s = _decay_tables(t, RET_CHUNK)

    y_p = x_prompt.reshape(b * s, d)
    y_s = x_sample.reshape(bd * t, d)
    tq = _row_tile(s, 512)
    tk = tq
    outs = [[] for _ in range(6)]
    for i in range(depth):
        lw = _layer_weights(w_in[i], w_uq[i], w_ukv[i], w_down_ret[i], w_down_mla[i], w_out[i], w_ple_gate[i],
                            w_ple_proj[i])
        lnw = ln_w[i][None, :]
        qnw = q_norm_w[i][None, :]
        kvnw = kv_norm_w[i][None, :]
        gnw = ret_gn_w[i][None, :]
        fw = final_norm_w[None, :]
        final = i == depth - 1

        qk, slab_p, xn = _project(y_p, lnw, lw, cos_rp, sin_rp)
        q, ckv, kr, krp, kv = _mla_prep(xn, lw["w_c"], qnw, lw["w_qa"], lw["w_qb"], kvnw, cos_mp, sin_mp,
                                        lw["w_kv"], tm=_row_tile(s, 512), with_kv=True)
        act_ret_p, s_fin = _ret_prompt(qk, slab_p, gnw, *dec_p, batch=b, seq=s)
        act_mla_p = _mla_prompt(q, kv, krp, slab_p, batch=b, seq=s, tq=tq, tk=tk)
        outs[0].append(ckv.reshape(b, s, KV_LORA))
        outs[1].append(kr.reshape(b, s, MLA_ROPE))
        outs[2].append(s_fin.astype(x_prompt.dtype))

        ms = bd * t
        qk, slab, xn = _project(y_s, lnw, lw, cos_rs, sin_rs)
        q, ckv, kr = _mla_prep(xn, lw["w_c"], qnw, lw["w_qa"], lw["w_qb"], kvnw, cos_ms, sin_ms, None,
                               tm=ms, with_kv=False)
        act_ret, s_new = _ret_sample(qk, slab, gnw, state_ret[i], *dec_s, batch=bd, t=t)
        q_lat = _headproj(q, lw["w_uk_t"], x_stride=MLA_QPAD // LANE)
        q_lat = q_lat.reshape(bd, t * MLA_HEADS, KV_LORA)
        q_rope = q.reshape(bd, t * MLA_HEADS, MLA_QPAD)[..., MLA_NOPE:MLA_QK]
        cache_rt = jnp.swapaxes(cache_krope[i], 1, 2)
        o_lat, y_p = _sample_attn_prompt_finish(
            page_table, q_lat, q_rope, ckv.reshape(bd, t, KV_LORA), kr.reshape(bd, t, MLA_ROPE), cache_ckv[i],
            cache_rt, y_p, act_ret_p, act_mla_p, slab_p, p_prompt[i].reshape(b * s, -1), lw["wdr"], lw["wdm"],
            lw["wo"], lw["wpg"], lw["wpp"], fw, t=t, tm=_row_tile(b * s, FUSED_FINISH_ROWS), final=final)
        o_lat = o_lat.reshape(ms, MLA_HEADS * KV_LORA)
        act_mla = _headproj(o_lat, lw["w_uv"], x_stride=1, gate=slab, gate_blk0=_S_MG // MLA_V)
        y_s = _finish(y_s, act_ret, act_mla, slab, p_sample[i].reshape(ms, -1), lw["wdr"], lw["wdm"], lw["wo"],
                      lw["wpg"], lw["wpp"], fw, tm=_row_tile(ms, 512), final=final)
        outs[3].append(ckv.reshape(bd, t, KV_LORA))
        outs[4].append(kr.reshape(bd, t, MLA_ROPE))
        outs[5].append(s_new.astype(state_ret.dtype))

    y_prompt = y_p.reshape(b, s, d)
    y_sample = y_s.reshape(bd, t, d)
    ckv_p, kr_p, ret_p, ckv_s, kr_s, ret_s = [jnp.stack(o) for o in outs]
    return (y_prompt, y_sample, ckv_p, kr_p, ret_p, ckv_s, kr_s, ret_s)
```

```python
import functools
import math

import numpy as np
import jax
import jax.numpy as jnp
from jax import lax
from jax.experimental import pallas as pl
from jax.experimental.pallas import tpu as pltpu

F32 = jnp.float32
BF16 = jnp.bfloat16

D_MODEL = 1024
PLE_DIM = 256
RET_HEADS = 4
RET_DK = 256
RET_DV = 512
RET_CHUNK = 128
MLA_HEADS = 8
MLA_NOPE = 128
MLA_ROPE = 64
MLA_V = 128
Q_LORA = 384
KV_LORA = 256
ROPE_BASE = 10000.0
EPS = 1e-6

RET_QK = RET_HEADS * RET_DK
RET_VW = RET_HEADS * RET_DV
MLA_QK = MLA_NOPE + MLA_ROPE
MLA_VW = MLA_HEADS * MLA_V
MLA_QPAD = 256
MLA_KVW = MLA_NOPE + MLA_V
Q_SCALE = MLA_QK ** -0.5 * math.log2(math.e)
LANE = 128
PAGES_PER_STEP = 32
N_SLOTS = 4
KEY_GROUPS = 2
KEY_SPLITS = 2
HEADS_PER_STEP = 2
SLAB_TN = 1792
WT_TILE = 512
WT_ROWS = 64
RET_SAMPLE_SEQS = 4
VMEM_LIMIT = 48 * 1024 * 1024
FUSED_VMEM_LIMIT = 56 * 1024 * 1024
FUSED_FINISH_ROWS = 256

_OFF_RQ = 0
_OFF_RK = _OFF_RQ + RET_QK
_OFF_RV = _OFF_RK + RET_QK
_OFF_RG = _OFF_RV + RET_VW
_OFF_CQ = _OFF_RG + RET_VW
_OFF_CKV = _OFF_CQ + Q_LORA
_OFF_KR = _OFF_CKV + KV_LORA
_OFF_MG = _OFF_KR + MLA_ROPE
_OFF_GA = _OFF_MG + MLA_VW
_OFF_GB = _OFF_GA + D_MODEL
_S_RV = 0
_S_RG = _S_RV + RET_VW
_S_MG = _S_RG + RET_VW
_S_GA = _S_MG + MLA_VW
_S_GB = _S_GA + D_MODEL
_S_W = _S_GB + D_MODEL

_NT = (((1,), (1,)), ((), ()))
_TN = (((0,), (0,)), ((), ()))


def _params(sem):
    return pltpu.CompilerParams(dimension_semantics=sem, vmem_limit_bytes=VMEM_LIMIT)


def _rms(x, w):
    return x * lax.rsqrt(jnp.mean(x * x, axis=-1, keepdims=True) + EPS) * w


def _silu(x):
    return x * jax.nn.sigmoid(x)


def _lane_blocks(s):
    return [s[:, c * LANE:(c + 1) * LANE] for c in range(s.shape[1] // LANE)]


def _lane_rowmax(s):
    blocks = _lane_blocks(s)
    mx = blocks[0]
    for blk in blocks[1:]:
        mx = jnp.maximum(mx, blk)
    return jnp.broadcast_to(jnp.max(mx, axis=-1, keepdims=True), mx.shape)


def _lane_exp(s, m):
    ps = [jnp.exp2(blk - m) for blk in _lane_blocks(s)]
    part = ps[0]
    for p in ps[1:]:
        part = part + p
    return jnp.concatenate([p.astype(BF16) for p in ps], axis=1), part


def _matmul_kernel(x_ref, w_ref, o_ref):
    o_ref[...] = jnp.dot(x_ref[...], w_ref[...], preferred_element_type=F32).astype(o_ref.dtype)


def _inproj_rope_kernel(x_ref, lnw_ref, w_ref, cos_ref, sin_ref, o_ref, xn_ref, *, n_q, k_scale, tn):
    j = pl.program_id(1)

    @pl.when(j == 0)
    def _():
        xn_ref[...] = _rms(x_ref[...], lnw_ref[...]).astype(BF16)

    acc = jnp.dot(xn_ref[...], w_ref[...], preferred_element_type=F32)
    scale = jnp.where(j >= n_q, k_scale, 1.0).astype(F32)
    cos = cos_ref[...] * scale
    sin = sin_ref[...] * scale
    for h in range(tn // RET_DK):
        lo = h * RET_DK
        a1 = acc[:, lo:lo + LANE]
        a2 = acc[:, lo + LANE:lo + RET_DK]
        o_ref[:, lo:lo + LANE] = (a1 * cos - a2 * sin).astype(o_ref.dtype)
        o_ref[:, lo + LANE:lo + RET_DK] = (a2 * cos + a1 * sin).astype(o_ref.dtype)


def _inproj_rope(x, ln_w, w, cos, sin, *, col0, n, tm, tn, n_q, k_scale):
    m, d = x.shape
    blk0 = col0 // tn
    p_tiles = cos.shape[0] // tm
    return pl.pallas_call(
        functools.partial(_inproj_rope_kernel, n_q=n_q, k_scale=k_scale, tn=tn),
        grid=(m // tm, n // tn),
        in_specs=[
            pl.BlockSpec((tm, d), lambda i, j: (i, 0)),
            pl.BlockSpec((1, d), lambda i, j: (0, 0)),
            pl.BlockSpec((d, tn), lambda i, j: (0, blk0 + j)),
            pl.BlockSpec((tm, LANE), lambda i, j: (i % p_tiles, 0)),
            pl.BlockSpec((tm, LANE), lambda i, j: (i % p_tiles, 0)),
        ],
        out_specs=[pl.BlockSpec((tm, tn), lambda i, j: (i, j)), pl.BlockSpec((tm, d), lambda i, j: (i, 0))],
        out_shape=[jax.ShapeDtypeStruct((m, n), BF16), jax.ShapeDtypeStruct((m, d), BF16)],
        compiler_params=_params(("arbitrary", "arbitrary")),
        name="inproj_rope",
    )(x, ln_w, w, cos, sin)


def _inproj(xn, w, *, col0, n, tm, tn):
    m, d = xn.shape
    blk0 = col0 // tn
    return pl.pallas_call(
        _matmul_kernel,
        grid=(m // tm, n // tn),
        in_specs=[pl.BlockSpec((tm, d), lambda i, j: (i, 0)), pl.BlockSpec((d, tn), lambda i, j: (0, blk0 + j))],
        out_specs=pl.BlockSpec((tm, tn), lambda i, j: (i, j)),
        out_shape=jax.ShapeDtypeStruct((m, n), BF16),
        compiler_params=_params(("arbitrary", "arbitrary")),
        name="inproj",
    )(xn, w)


def _wt_kernel(tbl_ref, *refs):
    del tbl_ref
    o_ref = refs[-1]
    rows = jnp.concatenate([r[...] for r in refs[:-1]], axis=0)
    o_ref[...] = rows.T.astype(BF16)


def _transpose_weight(w_t, src_rows):
    d = w_t.shape[1]
    parts = WT_TILE // WT_ROWS
    assert all(r % WT_ROWS == 0 for r in src_rows)
    tbl = np.asarray([r // WT_ROWS for r in src_rows], np.int32)
    grid_spec = pltpu.PrefetchScalarGridSpec(
        num_scalar_prefetch=1,
        grid=(len(src_rows),),
        in_specs=[pl.BlockSpec((WT_ROWS, d), lambda j, tbl, r=r: (tbl[j] + r, 0)) for r in range(parts)],
        out_specs=pl.BlockSpec((d, WT_TILE), lambda j, tbl: (0, j)),
    )
    return pl.pallas_call(
        _wt_kernel,
        grid_spec=grid_spec,
        out_shape=jax.ShapeDtypeStruct((d, len(src_rows) * WT_TILE), BF16),
        compiler_params=_params(("arbitrary",)),
        name="wt",
    )(tbl, *([w_t] * parts))


def _mla_prep_kernel(xn_ref, wc_ref, qnw_ref, wqa_ref, wqb_ref, kvnw_ref, cos_ref, sin_ref, *rest,
                     with_kv):
    if with_kv:
        wkv_ref, q_ref, ckv_ref, kr_ref, krp_ref, kv_ref = rest
    else:
        q_ref, ckv_ref, kr_ref = rest
    zc = jnp.dot(xn_ref[...], wc_ref[...], preferred_element_type=F32)
    cos = cos_ref[...]
    sin = sin_ref[...]
    cqn = _rms(zc[:, :Q_LORA], qnw_ref[...]).astype(BF16)
    qa = jnp.dot(cqn, wqa_ref[...], preferred_element_type=F32)
    qb = jnp.dot(cqn, wqb_ref[...], preferred_element_type=F32)
    qcos = cos * Q_SCALE
    qsin = sin * Q_SCALE
    for h in range(MLA_HEADS):
        lo = h * MLA_QPAD
        q_ref[:, lo:lo + LANE] = (qa[:, lo:lo + LANE] * Q_SCALE).astype(q_ref.dtype)
        q_ref[:, lo + LANE:lo + MLA_QPAD] = (
            qa[:, lo + LANE:lo + MLA_QPAD] * qcos + qb[:, h * LANE:(h + 1) * LANE] * qsin).astype(q_ref.dtype)
    c0 = Q_LORA
    ckvn = _rms(zc[:, c0:c0 + KV_LORA], kvnw_ref[...])
    ckv_ref[...] = ckvn
    k0 = c0 + KV_LORA
    krp = zc[:, k0:k0 + LANE] * cos + zc[:, k0 + LANE:k0 + 2 * LANE] * sin
    kr_ref[...] = krp[:, :MLA_ROPE]
    if with_kv:
        krp_ref[...] = krp.astype(BF16)
        kv_ref[...] = jnp.dot(ckvn.astype(BF16), wkv_ref[...], preferred_element_type=F32).astype(BF16)


def _mla_prep(xn, wc, qnw, wqa, wqb, kvnw, cos, sin, wkv, *, tm, with_kv):
    m, d = xn.shape
    p_tiles = cos.shape[0] // tm
    row = lambda i: (i, 0)
    fixed = lambda i: (0, 0)
    in_specs = [
        pl.BlockSpec((tm, d), row),
        pl.BlockSpec(wc.shape, fixed),
        pl.BlockSpec((1, Q_LORA), fixed),
        pl.BlockSpec(wqa.shape, fixed),
        pl.BlockSpec(wqb.shape, fixed),
        pl.BlockSpec((1, KV_LORA), fixed),
        pl.BlockSpec((tm, LANE), lambda i: (i % p_tiles, 0)),
        pl.BlockSpec((tm, LANE), lambda i: (i % p_tiles, 0)),
    ]
    args = [xn, wc, qnw, wqa, wqb, kvnw, cos, sin]
    qw = MLA_HEADS * MLA_QPAD
    out_specs = [pl.BlockSpec((tm, qw), row), pl.BlockSpec((tm, KV_LORA), row), pl.BlockSpec((tm, MLA_ROPE), row)]
    out_shape = [jax.ShapeDtypeStruct((m, qw), BF16), jax.ShapeDtypeStruct((m, KV_LORA), F32),
                 jax.ShapeDtypeStruct((m, MLA_ROPE), F32)]
    if with_kv:
        in_specs.append(pl.BlockSpec(wkv.shape, fixed))
        args.append(wkv)
        out_specs += [pl.BlockSpec((tm, LANE), row), pl.BlockSpec((tm, wkv.shape[1]), row)]
        out_shape += [jax.ShapeDtypeStruct((m, LANE), BF16), jax.ShapeDtypeStruct((m, wkv.shape[1]), BF16)]
    return pl.pallas_call(
        functools.partial(_mla_prep_kernel, with_kv=with_kv),
        grid=(m // tm,),
        in_specs=in_specs,
        out_specs=out_specs,
        out_shape=out_shape,
        compiler_params=_params(("arbitrary",)),
        name="mla_prep",
    )(*args)


def _ret_chunk(q, k, v, s_prev, dmask, qdec, kdec, gc):
    sc = lax.dot_general(q, k, _NT, preferred_element_type=F32) * dmask
    intra = jnp.dot(sc.astype(BF16), v, preferred_element_type=F32)
    cross = jnp.dot(q, s_prev.astype(BF16), preferred_element_type=F32) * qdec
    kd = (k.astype(F32) * kdec).astype(BF16)
    s_new = gc * s_prev + lax.dot_general(kd, v, _TN, preferred_element_type=F32)
    return intra + cross, s_new


def _ret_act(o, rg, gnw):
    mu = jnp.mean(o, axis=-1, keepdims=True)
    d = o - mu
    var = jnp.mean(d * d, axis=-1, keepdims=True)
    return d * lax.rsqrt(var + EPS) * gnw * _silu(rg.astype(F32))


def _ret_prompt_kernel(q_ref, k_ref, v_ref, rg_ref, gnw_ref, dm_ref, qd_ref, kd_ref, gc_ref,
                       o_ref, sfin_ref, s_ref, *, n_chunks, nh):
    s_ref[...] = jnp.zeros_like(s_ref)

    def chunk(c, carry):
        rows = pl.ds(pl.multiple_of(c * RET_CHUNK, RET_CHUNK), RET_CHUNK)
        for h in range(nh):
            qk_cols = slice(h * RET_DK, (h + 1) * RET_DK)
            v_cols = slice(h * RET_DV, (h + 1) * RET_DV)
            o, s_new = _ret_chunk(q_ref[rows, qk_cols], k_ref[rows, qk_cols], v_ref[rows, v_cols], s_ref[h],
                                  dm_ref[h], qd_ref[h], kd_ref[h], gc_ref[h])
            s_ref[h] = s_new
            o_ref[rows, v_cols] = _ret_act(o, rg_ref[rows, v_cols], gnw_ref[:, v_cols]).astype(o_ref.dtype)
        return carry

    lax.fori_loop(0, n_chunks, chunk, 0, unroll=4)
    sfin_ref[0] = s_ref[...]


def _ret_prompt(qk, slab, gn_w, dmask, qdec, kdec, gcs, *, batch, seq):
    nh = HEADS_PER_STEP
    qw, vw = nh * RET_DK, nh * RET_DV
    k_blk = RET_QK // qw
    rg_blk = _S_RG // vw
    return pl.pallas_call(
        functools.partial(_ret_prompt_kernel, n_chunks=seq // RET_CHUNK, nh=nh),
        grid=(batch, RET_HEADS // nh),
        in_specs=[
            pl.BlockSpec((seq, qw), lambda b, hp: (b, hp)),
            pl.BlockSpec((seq, qw), lambda b, hp: (b, k_blk + hp)),
            pl.BlockSpec((seq, vw), lambda b, hp: (b, hp)),
            pl.BlockSpec((seq, vw), lambda b, hp: (b, rg_blk + hp)),
            pl.BlockSpec((1, vw), lambda b, hp: (0, hp)),
            pl.BlockSpec((nh, RET_CHUNK, RET_CHUNK), lambda b, hp: (hp, 0, 0)),
            pl.BlockSpec((nh, RET_CHUNK, RET_DV), lambda b, hp: (hp, 0, 0)),
            pl.BlockSpec((nh, RET_CHUNK, RET_DK), lambda b, hp: (hp, 0, 0)),
            pl.BlockSpec((nh, 1, RET_DV), lambda b, hp: (hp, 0, 0)),
        ],
        out_specs=[
            pl.BlockSpec((seq, vw), lambda b, hp: (b, hp)),
            pl.BlockSpec((1, nh, RET_DK, RET_DV), lambda b, hp: (b, hp, 0, 0)),
        ],
        out_shape=[
            jax.ShapeDtypeStruct((batch * seq, RET_VW), BF16),
            jax.ShapeDtypeStruct((batch, RET_HEADS, RET_DK, RET_DV), F32),
        ],
        scratch_shapes=[pltpu.VMEM((nh, RET_DK, RET_DV), F32)],
        compiler_params=_params(("arbitrary", "arbitrary")),
        name="ret_prompt",
    )(qk, qk, slab, slab, gn_w, dmask, qdec, kdec, gcs)


def _ret_sample_kernel(qk_ref, vg_ref, gnw_ref, st_ref, dm_ref, qd_ref, kd_ref, gc_ref,
                       o_ref, snew_ref, qp_ref, kp_ref, vp_ref, act_ref, *, t, nb):
    qp_ref[...] = jnp.zeros_like(qp_ref)
    kp_ref[...] = jnp.zeros_like(kp_ref)
    vp_ref[...] = jnp.zeros_like(vp_ref)
    qk = qk_ref[...].astype(F32)
    vg = vg_ref[...].astype(F32)
    for bb in range(nb):
        rows = slice(bb * t, (bb + 1) * t)
        for h in range(RET_HEADS):
            qp_ref[0:t, :] = qk[rows, h * RET_DK:(h + 1) * RET_DK]
            kp_ref[0:t, :] = qk[rows, RET_QK + h * RET_DK:RET_QK + (h + 1) * RET_DK]
            vp_ref[0:t, :] = vg[rows, _S_RV + h * RET_DV:_S_RV + (h + 1) * RET_DV]
            o, s_new = _ret_chunk(qp_ref[...].astype(BF16), kp_ref[...].astype(BF16), vp_ref[...].astype(BF16),
                                  st_ref[bb, h], dm_ref[h], qd_ref[h], kd_ref[h], gc_ref[h])
            snew_ref[bb, h] = s_new
            act_ref[rows, h * RET_DV:(h + 1) * RET_DV] = _ret_act(
                o[0:t, :], vg[rows, _S_RG + h * RET_DV:_S_RG + (h + 1) * RET_DV],
                gnw_ref[:, h * RET_DV:(h + 1) * RET_DV])
    o_ref[...] = act_ref[...].astype(o_ref.dtype)


def _ret_sample(qk, slab, gn_w, state, dmask, qdec, kdec, gcs, *, batch, t):
    h = RET_HEADS
    whole = lambda b: (0, 0, 0)
    nb = _row_tile(batch, RET_SAMPLE_SEQS)
    return pl.pallas_call(
        functools.partial(_ret_sample_kernel, t=t, nb=nb),
        grid=(batch // nb,),
        in_specs=[
            pl.BlockSpec((nb * t, qk.shape[1]), lambda b: (b, 0)),
            pl.BlockSpec((nb * t, _S_MG), lambda b: (b, 0)),
            pl.BlockSpec((1, RET_VW), lambda b: (0, 0)),
            pl.BlockSpec((nb, h, RET_DK, RET_DV), lambda b: (b, 0, 0, 0)),
            pl.BlockSpec(dmask.shape, whole),
            pl.BlockSpec(qdec.shape, whole),
            pl.BlockSpec(kdec.shape, whole),
            pl.BlockSpec(gcs.shape, whole),
        ],
        out_specs=[
            pl.BlockSpec((nb * t, RET_VW), lambda b: (b, 0)),
            pl.BlockSpec((nb, h, RET_DK, RET_DV), lambda b: (b, 0, 0, 0)),
        ],
        out_shape=[
            jax.ShapeDtypeStruct((batch * t, RET_VW), BF16),
            jax.ShapeDtypeStruct(state.shape, state.dtype),
        ],
        scratch_shapes=[pltpu.VMEM((RET_CHUNK, RET_DK), F32), pltpu.VMEM((RET_CHUNK, RET_DK), F32),
                        pltpu.VMEM((RET_CHUNK, RET_DV), F32), pltpu.VMEM((nb * t, RET_VW), F32)],
        compiler_params=_params(("arbitrary",)),
        name="ret_sample",
    )(qk, slab, gn_w, state, dmask, qdec, kdec, gcs)


def _mla_prompt_kernel(q_ref, kv_ref, krp_ref, mg_ref, o_ref, m_ref, l_ref, acc_ref, *, tq, tk, nh):
    qi = pl.program_id(2)
    ratio = tq // tk

    def scores(kb, h):
        rows = pl.ds(pl.multiple_of(kb * tk, tk), tk)
        k = jnp.concatenate([kv_ref[rows, h * MLA_KVW:h * MLA_KVW + MLA_NOPE], krp_ref[rows, :]], axis=1)
        s = lax.dot_general(q_ref[:, h * MLA_QPAD:(h + 1) * MLA_QPAD], k, _NT, preferred_element_type=F32)
        return s, kv_ref[rows, h * MLA_KVW + MLA_NOPE:(h + 1) * MLA_KVW]

    def update(h, s, v, first=False):
        m_cur = _lane_rowmax(s)
        if first:
            m_new = m_cur
        else:
            m_old = m_ref[h]
            m_new = jnp.maximum(m_old, m_cur)
            alpha = jnp.exp2(m_old - m_new)
        p, l_part = _lane_exp(s, m_new)
        pv = jnp.dot(p, v, preferred_element_type=F32)
        m_ref[h] = m_new
        l_ref[h] = l_part if first else alpha * l_ref[h] + l_part
        acc_ref[h] = pv if first else alpha * acc_ref[h] + pv

    r_id = lax.broadcasted_iota(jnp.int32, (tq, tk), 0)
    c_id = lax.broadcasted_iota(jnp.int32, (tq, tk), 1)
    for d in range(ratio):
        for h in range(nh):
            s, v = scores(qi * ratio + d, h)
            update(h, jnp.where(c_id + d * tk <= r_id, s, -jnp.inf), v, first=d == 0)

    def body(kb, carry):
        for h in range(nh):
            s, v = scores(kb, h)
            update(h, s, v)
        return carry

    lax.fori_loop(0, qi * ratio, body, 0)
    for h in range(nh):
        cols = slice(h * MLA_V, (h + 1) * MLA_V)
        l = jnp.sum(l_ref[h], axis=-1, keepdims=True)
        o_ref[:, cols] = (acc_ref[h] / l * _silu(mg_ref[:, cols].astype(F32))).astype(o_ref.dtype)


def _mla_prompt(q, kv, krp, slab, *, batch, seq, tq, tk):
    nq = seq // tq
    nh = MLA_HEADS
    mg_blk = _S_MG // (nh * MLA_V)
    return pl.pallas_call(
        functools.partial(_mla_prompt_kernel, tq=tq, tk=tk, nh=nh),
        grid=(batch, MLA_HEADS // nh, nq),
        in_specs=[
            pl.BlockSpec((tq, nh * MLA_QPAD), lambda b, hp, i: (b * nq + i, hp)),
            pl.BlockSpec((seq, nh * MLA_KVW), lambda b, hp, i: (b, hp)),
            pl.BlockSpec((seq, LANE), lambda b, hp, i: (b, 0)),
            pl.BlockSpec((tq, nh * MLA_V), lambda b, hp, i: (b * nq + i, mg_blk + hp)),
        ],
        out_specs=pl.BlockSpec((tq, nh * MLA_V), lambda b, hp, i: (b * nq + i, hp)),
        out_shape=jax.ShapeDtypeStruct((batch * seq, MLA_VW), BF16),
        scratch_shapes=[pltpu.VMEM((nh, tq, LANE), F32), pltpu.VMEM((nh, tq, LANE), F32),
                        pltpu.VMEM((nh, tq, MLA_V), F32)],
        compiler_params=_params(("arbitrary", "arbitrary", "arbitrary")),
        name="mla_prompt",
    )(q, kv, krp, slab)


def _headproj_kernel(x_ref, w_ref, *rest, gated):
    if gated:
        g_ref, o_ref = rest
    else:
        (o_ref,) = rest
    y = jnp.dot(x_ref[...], w_ref[0], preferred_element_type=F32)
    if gated:
        y = y * _silu(g_ref[...].astype(F32))
    o_ref[...] = y.astype(o_ref.dtype)


def _headproj(x, w, *, x_stride, gate=None, gate_blk0=0):
    m = x.shape[0]
    nh, kdim, n = w.shape
    in_specs = [pl.BlockSpec((m, kdim), lambda h: (0, h * x_stride)),
                pl.BlockSpec((1, kdim, n), lambda h: (h, 0, 0))]
    args = [x, w]
    if gate is not None:
        in_specs.append(pl.BlockSpec((m, n), lambda h: (0, gate_blk0 + h)))
        args.append(gate)
    return pl.pallas_call(
        functools.partial(_headproj_kernel, gated=gate is not None),
        grid=(nh,),
        in_specs=in_specs,
        out_specs=pl.BlockSpec((m, n), lambda h: (0, h)),
        out_shape=jax.ShapeDtypeStruct((m, nh * n), BF16),
        compiler_params=_params(("arbitrary",)),
        name="headproj",
    )(*args)


def _page_copies(pt_ref, cache_c, cache_rt, cbuf, rbuf, sem, idx, slot, *, g, n_steps, page):
    b = idx // n_steps
    first = (idx % n_steps) * g
    out = []
    for j in range(g):
        p = pt_ref[b, first + j]
        out.append(pltpu.make_async_copy(cache_c.at[p], cbuf.at[slot, pl.ds(j * page, page), :], sem.at[slot, 0]))
        out.append(pltpu.make_async_copy(cache_rt.at[p], rbuf.at[slot, j], sem.at[slot, 1]))
    return out


def _start_all(copies):
    for c in copies:
        c.start(priority=1)


def _mla_sample_kernel(pt_ref, ql_ref, qr_ref, cn_ref, rn_ref, cache_c, cache_rt, o_ref,
                       cbuf, rbuf, sem, m_ref, l_ref, acc_ref, *, t, page, g, n_steps):
    i = pl.program_id(0)
    last = pl.num_programs(0) - 1
    step = i % n_steps
    slot = i % N_SLOTS
    copies = functools.partial(_page_copies, pt_ref, cache_c, cache_rt, cbuf, rbuf, sem,
                               g=g, n_steps=n_steps, page=page)

    @pl.when(i == 0)
    def _():
        for ahead in range(N_SLOTS - 1):
            _start_all(copies(jnp.minimum(ahead, last), ahead))

    ql = ql_ref[0]
    qr = qr_ref[0]
    rows = ql.shape[0]

    @pl.when(step == 0)
    def _():
        qlf = ql.astype(F32)
        qrf = qr.astype(F32)
        tok = lax.broadcasted_iota(jnp.int32, (rows, 1), 0) // (rows // t)
        cn = cn_ref[0].astype(BF16).astype(F32)
        rn = rn_ref[0].astype(BF16).astype(F32)
        s_cols = []
        for j in range(t):
            sj = (jnp.sum(qlf * cn[j:j + 1, :], axis=-1, keepdims=True)
                  + jnp.sum(qrf * rn[j:j + 1, :], axis=-1, keepdims=True))
            s_cols.append(jnp.where(tok >= j, sj, -jnp.inf))
        m0 = s_cols[0]
        for j in range(1, t):
            m0 = jnp.maximum(m0, s_cols[j])
        l0 = jnp.zeros((rows, 1), F32)
        a0 = jnp.zeros((rows, KV_LORA), F32)
        for j in range(t):
            pj = jnp.exp2(s_cols[j] - m0)
            l0 = l0 + pj
            a0 = a0 + pj.astype(BF16).astype(F32) * cn[j:j + 1, :]
        lane = lax.broadcasted_iota(jnp.int32, (rows, LANE), 1)
        m_ref[...] = jnp.broadcast_to(m0, (rows, LANE))
        l_ref[...] = jnp.where(lane == 0, l0, 0.0)
        acc_ref[...] = a0

    for c in copies(i, slot):
        c.wait()
    ppk = g // (KEY_GROUPS * KEY_SPLITS)
    sub = ppk * page
    m_run, l_run, acc_run = m_ref[...], l_ref[...], acc_ref[...]
    for grp in range(KEY_GROUPS):
        cbs, parts = [], []
        for k in range(grp * KEY_SPLITS, (grp + 1) * KEY_SPLITS):
            c32 = cbuf[slot, k * sub:(k + 1) * sub, :]
            rb = jnp.concatenate([rbuf[slot, k * ppk + j] for j in range(ppk)], axis=1).astype(BF16)
            cbs.append(c32.astype(BF16))
            ct = c32.T.astype(BF16)
            parts.append(jnp.dot(ql, ct, preferred_element_type=F32)
                         + jnp.dot(qr, rb, preferred_element_type=F32))
        s = jnp.concatenate(parts, axis=1)
        m_new = jnp.maximum(m_run, _lane_rowmax(s))
        alpha = jnp.exp2(m_run - m_new)
        pb, l_part = _lane_exp(s, m_new)
        pv = [jnp.dot(pb[:, k * sub:(k + 1) * sub], cbs[k], preferred_element_type=F32) for k in range(KEY_SPLITS)]
        while len(pv) > 1:
            pv = [pv[j] + pv[j + 1] for j in range(0, len(pv), 2)]
        m_run = m_new
        l_run = alpha * l_run + l_part
        acc_run = jnp.concatenate([alpha] * (KV_LORA // LANE), axis=1) * acc_run + pv[0]
    m_ref[...] = m_run
    l_ref[...] = l_run
    acc_ref[...] = acc_run

    ahead = N_SLOTS - 1
    _start_all(copies(jnp.minimum(i + ahead, last), (i + ahead) % N_SLOTS))

    @pl.when(step == n_steps - 1)
    def _():
        o_ref[0] = (acc_ref[...] / jnp.sum(l_ref[...], axis=-1, keepdims=True)).astype(o_ref.dtype)

    @pl.when(i == last)
    def _():
        for extra in range(1, N_SLOTS):
            for c in copies(last, (last + extra) % N_SLOTS):
                c.wait()


def _mla_sample(page_table, ql, qr, c_new, r_new, cache_c, cache_rt, *, t):
    batch, n_pages = page_table.shape
    page = cache_c.shape[1]
    g = PAGES_PER_STEP
    n_steps = n_pages // g
    rows = ql.shape[1]
    per_b = lambda i, pt: (i // n_steps, 0, 0)
    grid_spec = pltpu.PrefetchScalarGridSpec(
        num_scalar_prefetch=1,
        grid=(batch * n_steps,),
        in_specs=[
            pl.BlockSpec((1, rows, KV_LORA), per_b),
            pl.BlockSpec((1, rows, MLA_ROPE), per_b),
            pl.BlockSpec((1, t, KV_LORA), per_b),
            pl.BlockSpec((1, t, MLA_ROPE), per_b),
            pl.BlockSpec(memory_space=pl.ANY),
            pl.BlockSpec(memory_space=pl.ANY),
        ],
        out_specs=pl.BlockSpec((1, rows, KV_LORA), per_b),
        scratch_shapes=[
            pltpu.VMEM((N_SLOTS, g * page, KV_LORA), F32),
            pltpu.VMEM((N_SLOTS, g, MLA_ROPE, page), F32),
            pltpu.SemaphoreType.DMA((N_SLOTS, 2)),
            pltpu.VMEM((rows, LANE), F32), pltpu.VMEM((rows, LANE), F32), pltpu.VMEM((rows, KV_LORA), F32),
        ],
    )
    return pl.pallas_call(
        functools.partial(_mla_sample_kernel, t=t, page=page, g=g, n_steps=n_steps),
        grid_spec=grid_spec,
        out_shape=jax.ShapeDtypeStruct((batch, rows, KV_LORA), BF16),
        compiler_params=_params(("arbitrary",)),
        name="mla_sample",
    )(page_table, ql, qr, c_new, r_new, cache_c, cache_rt)


def _finish_kernel(x_ref, ar_ref, am_ref, ga_ref, gb_ref, ple_ref, wdr_ref, wdm_ref, wo_ref, wpg_ref, wpp_ref,
                   fw_ref, o_ref, *, final):
    h_ret = jnp.dot(ar_ref[...], wdr_ref[...], preferred_element_type=F32)
    h_mla = jnp.dot(am_ref[...], wdm_ref[...], preferred_element_type=F32)
    merged = (jax.nn.sigmoid(ga_ref[...].astype(F32)) * h_ret
              + jax.nn.sigmoid(gb_ref[...].astype(F32)) * h_mla)
    x2 = x_ref[...] + jnp.dot(merged.astype(BF16), wo_ref[...], preferred_element_type=F32)
    gate = jax.nn.sigmoid(jnp.dot(x2.astype(BF16), wpg_ref[...], preferred_element_type=F32))
    y = x2 + gate * jnp.dot(ple_ref[...].astype(BF16), wpp_ref[...], preferred_element_type=F32)
    if final:
        y = _rms(y, fw_ref[...])
    o_ref[...] = y


def _finish(x, act_ret, act_mla, slab, ple, wdr, wdm, wo, wpg, wpp, fw, *, tm, final):
    m, d = x.shape
    row = lambda i: (i, 0)
    fixed = lambda i: (0, 0)
    once = pl.Buffered(1)
    return pl.pallas_call(
        functools.partial(_finish_kernel, final=final),
        grid=(m // tm,),
        in_specs=[
            pl.BlockSpec((tm, d), row),
            pl.BlockSpec((tm, RET_VW), row),
            pl.BlockSpec((tm, MLA_VW), row),
            pl.BlockSpec((tm, d), lambda i: (i, _S_GA // D_MODEL)),
            pl.BlockSpec((tm, d), lambda i: (i, _S_GB // D_MODEL)),
            pl.BlockSpec((tm, PLE_DIM), row),
            pl.BlockSpec(wdr.shape, fixed, pipeline_mode=once),
            pl.BlockSpec(wdm.shape, fixed, pipeline_mode=once),
            pl.BlockSpec(wo.shape, fixed, pipeline_mode=once),
            pl.BlockSpec(wpg.shape, fixed, pipeline_mode=once),
            pl.BlockSpec(wpp.shape, fixed, pipeline_mode=once),
            pl.BlockSpec((1, d), fixed),
        ],
        out_specs=pl.BlockSpec((tm, d), row),
        out_shape=jax.ShapeDtypeStruct((m, d), F32),
        compiler_params=_params(("arbitrary",)),
        name="finish",
    )(x, act_ret, act_mla, slab, slab, ple, wdr, wdm, wo, wpg, wpp, fw)


def _sample_attn_prompt_finish_kernel(pt_ref, ql_ref, qr_ref, cn_ref, rn_ref, cache_c, cache_rt,
                                      x_ref, ar_ref, am_ref, ga_ref, gb_ref, ple_ref, wdr_ref, wdm_ref, wo_ref,
                                      wpg_ref, wpp_ref, fw_ref, o_ref, y_ref, cbuf, rbuf, sem, m_ref, l_ref, acc_ref,
                                      *, t, page, g, n_steps, every, final):
    _mla_sample_kernel(pt_ref, ql_ref, qr_ref, cn_ref, rn_ref, cache_c, cache_rt, o_ref,
                       cbuf, rbuf, sem, m_ref, l_ref, acc_ref, t=t, page=page, g=g, n_steps=n_steps)

    @pl.when(pl.program_id(0) % every == 0)
    def _():
        _finish_kernel(x_ref, ar_ref, am_ref, ga_ref, gb_ref, ple_ref, wdr_ref, wdm_ref, wo_ref, wpg_ref, wpp_ref,
                       fw_ref, y_ref, final=final)


def _sample_attn_prompt_finish(page_table, ql, qr, c_new, r_new, cache_c, cache_rt,
                               x, act_ret, act_mla, slab, ple, wdr, wdm, wo, wpg, wpp, fw, *, t, tm, final):
    batch, n_pages = page_table.shape
    page = cache_c.shape[1]
    g = PAGES_PER_STEP
    n_steps = n_pages // g
    rows = ql.shape[1]
    m, d = x.shape
    total = batch * n_steps
    assert m % tm == 0 and total % (m // tm) == 0
    every = total // (m // tm)
    per_b = lambda i, pt: (i // n_steps, 0, 0)
    row = lambda i, pt: (i // every, 0)
    fixed = lambda i, pt: (0, 0)
    once = pl.Buffered(1)
    grid_spec = pltpu.PrefetchScalarGridSpec(
        num_scalar_prefetch=1,
        grid=(total,),
        in_specs=[
            pl.BlockSpec((1, rows, KV_LORA), per_b),
            pl.BlockSpec((1, rows, MLA_ROPE), per_b),
            pl.BlockSpec((1, t, KV_LORA), per_b),
            pl.BlockSpec((1, t, MLA_ROPE), per_b),
            pl.BlockSpec(memory_space=pl.ANY),
            pl.BlockSpec(memory_space=pl.ANY),
            pl.BlockSpec((tm, d), row),
            pl.BlockSpec((tm, RET_VW), row),
            pl.BlockSpec((tm, MLA_VW), row),
            pl.BlockSpec((tm, d), lambda i, pt: (i // every, _S_GA // D_MODEL)),
            pl.BlockSpec((tm, d), lambda i, pt: (i // every, _S_GB // D_MODEL)),
            pl.BlockSpec((tm, PLE_DIM), row),
            pl.BlockSpec(wdr.shape, fixed, pipeline_mode=once),
            pl.BlockSpec(wdm.shape, fixed, pipeline_mode=once),
            pl.BlockSpec(wo.shape, fixed, pipeline_mode=once),
            pl.BlockSpec(wpg.shape, fixed, pipeline_mode=once),
            pl.BlockSpec(wpp.shape, fixed, pipeline_mode=once),
            pl.BlockSpec((1, d), fixed),
        ],
        out_specs=[pl.BlockSpec((1, rows, KV_LORA), per_b), pl.BlockSpec((tm, d), row)],
        scratch_shapes=[
            pltpu.VMEM((N_SLOTS, g * page, KV_LORA), F32),
            pltpu.VMEM((N_SLOTS, g, MLA_ROPE, page), F32),
            pltpu.SemaphoreType.DMA((N_SLOTS, 2)),
            pltpu.VMEM((rows, LANE), F32), pltpu.VMEM((rows, LANE), F32), pltpu.VMEM((rows, KV_LORA), F32),
        ],
    )
    return pl.pallas_call(
        functools.partial(_sample_attn_prompt_finish_kernel, t=t, page=page, g=g, n_steps=n_steps, every=every,
                          final=final),
        grid_spec=grid_spec,
        out_shape=[jax.ShapeDtypeStruct((batch, rows, KV_LORA), BF16), jax.ShapeDtypeStruct((m, d), F32)],
        compiler_params=pltpu.CompilerParams(dimension_semantics=("arbitrary",), vmem_limit_bytes=FUSED_VMEM_LIMIT),
        name="mla_sample_finish",
    )(page_table, ql, qr, c_new, r_new, cache_c, cache_rt, x, act_ret, act_mla, slab, slab, ple,
      wdr, wdm, wo, wpg, wpp, fw)


def _rope_tables(pos, half):
    inv_freq = ROPE_BASE ** (-np.arange(half, dtype=np.float64) / half)
    ang = np.asarray(pos, np.float64)[:, None] * inv_freq[None, :]
    return np.cos(ang).astype(np.float32), np.sin(ang).astype(np.float32)


def _mla_rope_tables(pos):
    cos, sin = _rope_tables(pos, MLA_ROPE // 2)
    z = np.zeros((len(pos), LANE - MLA_ROPE), np.float32)
    return np.concatenate([cos, cos, z], axis=1), np.concatenate([sin, sin, z], axis=1)


def _decay_tables(c, rows):
    log_g = np.log1p(-np.exp2(-5.0 - np.arange(RET_HEADS, dtype=np.float64)))
    idx = np.arange(c, dtype=np.float64)
    diff = idx[:, None] - idx[None, :]
    dmask = np.where(diff >= 0, np.exp(np.maximum(diff, 0.0)[None] * log_g[:, None, None]), 0.0)
    qdec = np.exp((idx[None, :] + 1.0) * log_g[:, None])[:, :, None]
    kdec = np.exp((c - 1.0 - idx)[None, :] * log_g[:, None])[:, :, None]
    pad = rows - c
    dmask = np.pad(dmask, ((0, 0), (0, pad), (0, pad)))
    qdec = np.broadcast_to(np.pad(qdec, ((0, 0), (0, pad), (0, 0))), (RET_HEADS, rows, RET_DV))
    kdec = np.broadcast_to(np.pad(kdec, ((0, 0), (0, pad), (0, 0))), (RET_HEADS, rows, RET_DK))
    gcs = np.broadcast_to(np.exp(c * log_g)[:, None, None], (RET_HEADS, 1, RET_DV))
    return tuple(np.ascontiguousarray(a, np.float32) for a in (dmask, qdec, kdec, gcs))


def _rot_cols(w):
    half = w.shape[-1] // 2
    return jnp.concatenate([-w[..., half:], w[..., :half]], axis=-1)


def _layer_weights(w_in, w_uq, w_ukv, w_down_ret, w_down_mla, w_out, w_ple_gate, w_ple_proj):
    d = w_in.shape[0]
    src = [r for lo, hi in ((_OFF_RV, _OFF_CQ), (_OFF_MG, w_in.shape[1]), (_OFF_RQ, _OFF_RV))
           for r in range(lo, hi, WT_TILE)]
    n_main = len(src) * WT_TILE
    src += list(range(_OFF_CQ, _OFF_MG, WT_TILE))
    w_main = _transpose_weight(w_in.T, src)
    c_cols = w_main[:, n_main:n_main + _OFF_MG - _OFF_CQ]
    kr = c_cols[:, _OFF_KR - _OFF_CQ:]
    zpad = jnp.zeros((d, LANE - MLA_ROPE), BF16)
    w_c = jnp.concatenate([c_cols, zpad, _rot_cols(kr), zpad], axis=1)
    uq = w_uq.reshape(Q_LORA, MLA_HEADS, MLA_QK)
    uq_rope = uq[:, :, MLA_NOPE:]
    zq = jnp.zeros((Q_LORA, MLA_HEADS, MLA_QPAD - MLA_QK), F32)
    w_qa = jnp.concatenate([uq, zq], axis=2).reshape(Q_LORA, MLA_HEADS * MLA_QPAD).astype(BF16)
    w_qb = jnp.concatenate([_rot_cols(uq_rope), zq], axis=2).reshape(Q_LORA, MLA_HEADS * LANE).astype(BF16)
    ukv = w_ukv.reshape(KV_LORA, MLA_HEADS, MLA_KVW)
    w_uk_t = jnp.transpose(ukv[:, :, :MLA_NOPE], (1, 2, 0)).astype(BF16)
    w_uv = jnp.transpose(ukv[:, :, MLA_NOPE:], (1, 0, 2)).astype(BF16)
    return dict(w_main=w_main, w_c=w_c, w_qa=w_qa, w_qb=w_qb, w_kv=w_ukv.astype(BF16),
                w_uk_t=w_uk_t, w_uv=w_uv, wdr=w_down_ret.astype(BF16), wdm=w_down_mla.astype(BF16),
                wo=w_out.astype(BF16), wpg=w_ple_gate.astype(BF16), wpp=w_ple_proj.astype(BF16))


def _row_tile(m, cap):
    t = min(m, cap)
    assert m % t == 0
    return t


def _project(x, ln_w, lw, cos_r, sin_r):
    tm = _row_tile(cos_r.shape[0], 1024)
    assert x.shape[0] % tm == 0
    qk, xn = _inproj_rope(x, ln_w, lw["w_main"], cos_r, sin_r, col0=_S_W, n=2 * RET_QK, tm=tm, tn=1024,
                          n_q=RET_QK // 1024, k_scale=RET_DK ** -0.5)
    slab = _inproj(xn, lw["w_main"], col0=0, n=_S_W, tm=tm, tn=SLAB_TN)
    return qk, slab, xn


def kernel(x_prompt, x_sample, cache_ckv, cache_krope, state_ret, page_table, p_prompt, p_sample, ln_w, w_in,
           q_norm_w, w_uq, kv_norm_w, w_ukv, ret_gn_w, w_down_ret, w_down_mla, w_out, w_ple_gate, w_ple_proj,
           final_norm_w):
    depth = w_in.shape[0]
    b, s, d = x_prompt.shape
    bd, t, _ = x_sample.shape
    n_pages = page_table.shape[1]
    page = cache_ckv.shape[2]
    past_len = n_pages * page
    assert s % RET_CHUNK == 0 and n_pages % PAGES_PER_STEP == 0

    pos_p = np.arange(s)
    pos_s = np.tile(past_len + np.arange(t), bd)
    cos_rp, sin_rp = _rope_tables(pos_p, RET_DK // 2)
    cos_rs, sin_rs = _rope_tables(pos_s, RET_DK // 2)
    cos_mp, sin_mp = _mla_rope_tables(pos_p)
    cos_ms, sin_ms = _mla_rope_tables(pos_s)
    dec_p = _decay_tables(RET_CHUNK, RET_CHUNK)
    dec_s = _decay_tables(t, RET_CHUNK)

    y_p = x_prompt.reshape(b * s, d)
    y_s = x_sample.reshape(bd * t, d)
    tq = _row_tile(s, 512)
    tk = tq
    outs = [[] for _ in range(6)]
    for i in range(depth):
        lw = _layer_weights(w_in[i], w_uq[i], w_ukv[i], w_down_ret[i], w_down_mla[i], w_out[i], w_ple_gate[i],
                            w_ple_proj[i])
        lnw = ln_w[i][None, :]
        qnw = q_norm_w[i][None, :]
        kvnw = kv_norm_w[i][None, :]
        gnw = ret_gn_w[i][None, :]
        fw = final_norm_w[None, :]
        final = i == depth - 1

        qk, slab_p, xn = _project(y_p, lnw, lw, cos_rp, sin_rp)
        q, ckv, kr, krp, kv = _mla_prep(xn, lw["w_c"], qnw, lw["w_qa"], lw["w_qb"], kvnw, cos_mp, sin_mp,
                                        lw["w_kv"], tm=_row_tile(s, 512), with_kv=True)
        act_ret_p, s_fin = _ret_prompt(qk, slab_p, gnw, *dec_p, batch=b, seq=s)
        act_mla_p = _mla_prompt(q, kv, krp, slab_p, batch=b, seq=s, tq=tq, tk=tk)
        outs[0].append(ckv.reshape(b, s, KV_LORA))
        outs[1].append(kr.reshape(b, s, MLA_ROPE))
        outs[2].append(s_fin.astype(x_prompt.dtype))

        ms = bd * t
        qk, slab, xn = _project(y_s, lnw, lw, cos_rs, sin_rs)
        q, ckv, kr = _mla_prep(xn, lw["w_c"], qnw, lw["w_qa"], lw["w_qb"], kvnw, cos_ms, sin_ms, None,
                               tm=ms, with_kv=False)
        act_ret, s_new = _ret_sample(qk, slab, gnw, state_ret[i], *dec_s, batch=bd, t=t)
        q_lat = _headproj(q, lw["w_uk_t"], x_stride=MLA_QPAD // LANE)
        q_lat = q_lat.reshape(bd, t * MLA_HEADS, KV_LORA)
        q_rope = q.reshape(bd, t * MLA_HEADS, MLA_QPAD)[..., MLA_NOPE:MLA_QK]
        cache_rt = jnp.swapaxes(cache_krope[i], 1, 2)
        o_lat, y_p = _sample_attn_prompt_finish(
            page_table, q_lat, q_rope, ckv.reshape(bd, t, KV_LORA), kr.reshape(bd, t, MLA_ROPE), cache_ckv[i],
            cache_rt, y_p, act_ret_p, act_mla_p, slab_p, p_prompt[i].reshape(b * s, -1), lw["wdr"], lw["wdm"],
            lw["wo"], lw["wpg"], lw["wpp"], fw, t=t, tm=_row_tile(b * s, FUSED_FINISH_ROWS), final=final)
        o_lat = o_lat.reshape(ms, MLA_HEADS * KV_LORA)
        act_mla = _headproj(o_lat, lw["w_uv"], x_stride=1, gate=slab, gate_blk0=_S_MG // MLA_V)
        y_s = _finish(y_s, act_ret, act_mla, slab, p_sample[i].reshape(ms, -1), lw["wdr"], lw["wdm"], lw["wo"],
                      lw["wpg"], lw["wpp"], fw, tm=_row_tile(ms, 512), final=final)
        outs[3].append(ckv.reshape(bd, t, KV_LORA))
        outs[4].append(kr.reshape(bd, t, MLA_ROPE))
        outs[5].append(s_new.astype(state_ret.dtype))

    y_prompt = y_p.reshape(b, s, d)
    y_sample = y_s.reshape(bd, t, d)
    ckv_p, kr_p, ret_p, ckv_s, kr_s, ret_s = [jnp.stack(o) for o in outs]
    return (y_prompt, y_sample, ckv_p, kr_p, ret_p, ckv_s, kr_s, ret_s)
```
